```python
import jax, jax.numpy as jnp
from jax import lax
import numpy as np

D_MODEL = 2048
BATCH = 4
SEQ = 2048
DEPTH = 2
DEC_BATCH = 128
DEC_SEQ = 4
PAST_LEN = 2048
PAGE_SIZE = 128

N_MIXERS = 2
N_HEADS = 16
HEAD_DIM = D_MODEL // N_HEADS
ROT_DIM = HEAD_DIM // 4
ROPE_THETA = 500000.0
MOBA_BLOCK = 256
MOBA_TOPK = 3
MOBA_Q_CHUNK = 64
GMLP_CHUNK = 128
GMLP_GROUPS = 8
D_GMLP = D_MODEL
MOE_GROUPS = 4
MOE_EXPERTS_PER_GROUP = 8
N_EXPERTS = MOE_GROUPS * MOE_EXPERTS_PER_GROUP
MOE_TOPK = 2
D_EXPERT = D_MODEL // 4
MOE_BLOCK = 128
N_ATTN_LAYERS = (DEPTH + 1) // 2
N_GMLP_LAYERS = DEPTH // 2
DEEPNORM_ALPHA = (2.0 * DEPTH) ** 0.25
DEEPNORM_BETA = (8.0 * DEPTH) ** -0.25
LN_EPS = 1e-5
ADA_STD = 0.25

kernel_name = 'moba_gmlp_hier_moe_decoder_step'


def layer_norm(x, g, b):
    xf = x.astype(jnp.float32)
    mu = jnp.mean(xf, axis=-1, keepdims=True)
    var = jnp.mean(jnp.square(xf - mu), axis=-1, keepdims=True)
    return ((xf - mu) * lax.rsqrt(var + LN_EPS) * g.astype(jnp.float32) + b.astype(jnp.float32)).astype(x.dtype)


def rotary(x, pos):
    half = ROT_DIM // 2
    inv = ROPE_THETA ** (-jnp.arange(half, dtype=jnp.float32) * 2.0 / ROT_DIM)
    ang = pos.astype(jnp.float32)[:, None] * inv[None, :]
    cos = jnp.cos(ang)[:, None, :]
    sin = jnp.sin(ang)[:, None, :]
    xr = x[..., :ROT_DIM].astype(jnp.float32)
    x1, x2 = xr[..., :half], xr[..., half:]
    rot = jnp.concatenate([x1 * cos - x2 * sin, x2 * cos + x1 * sin], axis=-1)
    return jnp.concatenate([rot.astype(x.dtype), x[..., ROT_DIM:]], axis=-1)


def moba_attend(q, t, kb, vb, kb_h, vb_h, km):
    nq = q.shape[0]
    nbk = km.shape[0]
    cur = t // MOBA_BLOCK
    gate = jnp.einsum('qhd,jhd->qhj', q.astype(jnp.float32), km)
    fully_past = jnp.arange(nbk)[None, None, :] < cur[:, None, None]
    gate = jnp.where(fully_past, gate, -jnp.inf)
    n_sel = min(MOBA_TOPK, nbk)
    _, sel = lax.top_k(gate, n_sel)
    sel_ok = sel < cur[:, None, None]
    k_sel = kb_h[jnp.arange(N_HEADS)[None, :, None], sel]
    v_sel = vb_h[jnp.arange(N_HEADS)[None, :, None], sel]
    k_own = kb[cur]
    v_own = vb[cur]
    scale = HEAD_DIM ** -0.5
    s_sel = jnp.einsum('qhd,qhnkd->qhnk', q, k_sel, preferred_element_type=jnp.float32) * scale
    s_sel = jnp.where(sel_ok[..., None], s_sel, -jnp.inf).reshape(nq, N_HEADS, n_sel * MOBA_BLOCK)
    s_own = jnp.einsum('qhd,qkhd->qhk', q, k_own, preferred_element_type=jnp.float32) * scale
    own_ok = (cur[:, None] * MOBA_BLOCK + jnp.arange(MOBA_BLOCK)[None, :]) <= t[:, None]
    s_own = jnp.where(own_ok[:, None, :], s_own, -jnp.inf)
    p = jax.nn.softmax(jnp.concatenate([s_sel, s_own], axis=-1), axis=-1)
    p_sel = p[..., :n_sel * MOBA_BLOCK].reshape(nq, N_HEADS, n_sel, MOBA_BLOCK).astype(v_sel.dtype)
    p_own = p[..., n_sel * MOBA_BLOCK:].astype(v_own.dtype)
    return (jnp.einsum('qhnk,qhnkd->qhd', p_sel, v_sel) + jnp.einsum('qhk,qkhd->qhd', p_own, v_own)).astype(q.dtype)


def moba_one_seq(q, q_pos, k, v):
    L = k.shape[0]
    nbk = -(-L // MOBA_BLOCK)
    pad = nbk * MOBA_BLOCK - L
    kb = jnp.pad(k, ((0, pad), (0, 0), (0, 0))).reshape(nbk, MOBA_BLOCK, N_HEADS, HEAD_DIM)
    vb = jnp.pad(v, ((0, pad), (0, 0), (0, 0))).reshape(nbk, MOBA_BLOCK, N_HEADS, HEAD_DIM)
    km = jnp.mean(kb.astype(jnp.float32), axis=1)
    kb_h = kb.transpose(2, 0, 1, 3)
    vb_h = vb.transpose(2, 0, 1, 3)
    nq = q.shape[0]
    chunk = MOBA_Q_CHUNK if nq % MOBA_Q_CHUNK == 0 else nq
    qc = q.reshape(nq // chunk, chunk, N_HEADS, HEAD_DIM)
    tc = q_pos.reshape(nq // chunk, chunk)
    out = lax.map(lambda a: moba_attend(a[0], a[1], kb, vb, kb_h, vb_h, km), (qc, tc))
    return out.reshape(nq, N_HEADS, HEAD_DIM)


def moba_mixer(hp, hs, pos_p, pos_s, k_pool, v_pool, page_table, w_qkv, w_o):
    def project(h, pos):
        b, s, _ = h.shape
        qkv = (h @ w_qkv).reshape(b, s, 3, N_HEADS, HEAD_DIM)
        return rotary(qkv[:, :, 0], pos), rotary(qkv[:, :, 1], pos), qkv[:, :, 2]
    qp, kp, vp = project(hp, pos_p)
    qs, ks, vs = project(hs, pos_s)
    op = lax.map(lambda a: moba_one_seq(a[0], pos_p, a[1], a[2]), (qp, kp, vp))

    def sample_seq(a):
        q, k_new, v_new, pages = a
        k = jnp.concatenate([k_pool[pages].reshape(-1, N_HEADS, HEAD_DIM), k_new], axis=0)
        v = jnp.concatenate([v_pool[pages].reshape(-1, N_HEADS, HEAD_DIM), v_new], axis=0)
        return moba_one_seq(q, pos_s, k, v)
    osamp = lax.map(sample_seq, (qs, ks, vs, page_table))
    yp = op.reshape(hp.shape) @ w_o
    ys = osamp.reshape(hs.shape) @ w_o
    return yp, ys, kp, vp, ks, vs


def gmlp_mixer(h, w_in, vn_g, vn_b, w_s, b_s, w_out):
    b, t, _ = h.shape
    uv = jax.nn.gelu(h @ w_in)
    u, v = uv[..., :D_GMLP], uv[..., D_GMLP:]
    v = layer_norm(v, vn_g, vn_b)
    n_chunks = -(-t // GMLP_CHUNK)
    vpad = jnp.pad(v, ((0, 0), (0, n_chunks * GMLP_CHUNK - t), (0, 0)))
    vpad = vpad.reshape(b, n_chunks, GMLP_CHUNK, GMLP_GROUPS, D_GMLP // GMLP_GROUPS)
    w = w_s * jnp.tril(jnp.ones((GMLP_CHUNK, GMLP_CHUNK), w_s.dtype))
    mixed = jnp.einsum('gij,bnjgd->bnigd', w, vpad) + b_s.T[None, None, :, :, None]
    mixed = mixed.reshape(b, n_chunks * GMLP_CHUNK, D_GMLP)[:, :t]
    return (u * mixed) @ w_out, v


def hier_moe(x, w_rg, b_rg, w_re, b_re, w_gate, w_up, w_down):
    T = x.shape[0]
    xf = x.astype(jnp.float32)
    lg = xf @ w_rg.astype(jnp.float32) + b_rg.astype(jnp.float32)
    pg = jax.nn.softmax(lg, axis=-1)
    grp = jnp.argmax(lg, axis=-1)
    le = (xf @ w_re.astype(jnp.float32) + b_re.astype(jnp.float32)).reshape(T, MOE_GROUPS, MOE_EXPERTS_PER_GROUP)
    le = jnp.take_along_axis(le, grp[:, None, None], axis=1)[:, 0]
    pe = jax.nn.softmax(le, axis=-1)
    top_p, top_i = lax.top_k(pe, MOE_TOPK)
    gate = jnp.take_along_axis(pg, grp[:, None], axis=1) * top_p / jnp.sum(top_p, axis=-1, keepdims=True)
    eid = grp[:, None] * MOE_EXPERTS_PER_GROUP + top_i
    A = T * MOE_TOPK
    flat_e = eid.reshape(A)
    flat_g = gate.reshape(A)
    order = jnp.argsort(flat_e)
    se = flat_e[order]
    tok = order // MOE_TOPK
    counts = jnp.bincount(flat_e, length=N_EXPERTS)
    padded = (counts + MOE_BLOCK - 1) // MOE_BLOCK * MOE_BLOCK
    pend = jnp.cumsum(padded)
    pstart = pend - padded
    cstart = jnp.cumsum(counts) - counts
    dest = pstart[se] + jnp.arange(A) - cstart[se]
    nblk = -(-A // MOE_BLOCK) + N_EXPERTS
    nrows = nblk * MOE_BLOCK
    row_tok = jnp.full((nrows,), T, jnp.int32).at[dest].set(tok.astype(jnp.int32))
    row_gate = jnp.zeros((nrows,), jnp.float32).at[dest].set(flat_g[order])
    blk_e = jnp.minimum(jnp.searchsorted(pend, jnp.arange(nblk) * MOE_BLOCK, side='right'), N_EXPERTS - 1)
    x_pad = jnp.concatenate([x, jnp.zeros((1, x.shape[1]), x.dtype)], axis=0)
    xb = x_pad[row_tok].reshape(nblk, MOE_BLOCK, x.shape[1])

    def expert_block(a):
        xblk, e = a
        return (jax.nn.silu(xblk @ w_gate[e]) * (xblk @ w_up[e])) @ w_down[e]
    yb = lax.map(expert_block, (xb, blk_e)).reshape(nrows, x.shape[1])
    out = jnp.zeros((T + 1, x.shape[1]), jnp.float32).at[row_tok].add(yb.astype(jnp.float32) * row_gate[:, None])
    return out[:T].astype(x.dtype)


def adaln(c, w, b):
    m = jax.nn.silu(c) @ w + b
    return jnp.split(m[:, None, :], 6, axis=-1)


def post_norm(x, y, gate, g, b):
    return layer_norm(DEEPNORM_ALPHA * x + (1.0 + gate) * y, g, b)


def setup_inputs(seed: int = 0) -> dict:
    key = jax.random.key(seed)
    ks = jax.random.split(key, 32)
    f32 = jnp.float32

    def nrm(k, shape, std):
        return jax.random.normal(k, shape, f32) * std
    n_pages = PAST_LEN // PAGE_SIZE
    n_used = DEC_BATCH * n_pages
    n_phys = n_used + max(1, n_used // 4)
    page_table = jax.random.permutation(ks[0], n_phys)[:n_used].reshape(DEC_BATCH, n_pages).astype(jnp.int32)
    cache_shape = (N_ATTN_LAYERS, n_phys, PAGE_SIZE, N_HEADS, HEAD_DIM)
    sd = D_MODEL ** -0.5
    w_qk = nrm(ks[1], (N_ATTN_LAYERS, D_MODEL, 2 * D_MODEL), sd)
    w_v = nrm(ks[2], (N_ATTN_LAYERS, D_MODEL, D_MODEL), sd * DEEPNORM_BETA)
    return {
        'x_prompt': nrm(ks[3], (BATCH, SEQ, D_MODEL), 1.0),
        'x_sample': nrm(ks[4], (DEC_BATCH, DEC_SEQ, D_MODEL), 1.0),
        'cache_k': nrm(ks[5], cache_shape, 1.0),
        'cache_v': nrm(ks[6], cache_shape, DEEPNORM_BETA),
        'page_table': page_table,
        'c_prompt': nrm(ks[7], (BATCH, D_MODEL), 1.0),
        'c_sample': nrm(ks[8], (DEC_BATCH, D_MODEL), 1.0),
        'w_ada': nrm(ks[9], (DEPTH, D_MODEL, 6 * D_MODEL), ADA_STD * sd),
        'b_ada': nrm(ks[10], (DEPTH, 6 * D_MODEL), 0.01),
        'ln_g': 1.0 + nrm(ks[11], (DEPTH, 2, D_MODEL), 0.02),
        'ln_b': nrm(ks[12], (DEPTH, 2, D_MODEL), 0.02),
        'attn_w_qkv': jnp.concatenate([w_qk, w_v], axis=-1),
        'attn_w_o': nrm(ks[13], (N_ATTN_LAYERS, D_MODEL, D_MODEL), sd * DEEPNORM_BETA),
        'gmlp_w_in': nrm(ks[14], (N_GMLP_LAYERS, D_MODEL, 2 * D_GMLP), sd),
        'gmlp_vn_g': 1.0 + nrm(ks[15], (N_GMLP_LAYERS, D_GMLP), 0.02),
        'gmlp_vn_b': nrm(ks[16], (N_GMLP_LAYERS, D_GMLP), 0.02),
        'gmlp_w_s': nrm(ks[17], (N_GMLP_LAYERS, GMLP_GROUPS, GMLP_CHUNK, GMLP_CHUNK), GMLP_CHUNK ** -0.5),
        'gmlp_b_s': 1.0 + nrm(ks[18], (N_GMLP_LAYERS, GMLP_GROUPS, GMLP_CHUNK), 0.05),
        'gmlp_w_out': nrm(ks[19], (N_GMLP_LAYERS, D_GMLP, D_MODEL), D_GMLP ** -0.5 * DEEPNORM_BETA),
        'moe_w_rg': nrm(ks[20], (DEPTH, D_MODEL, MOE_GROUPS), sd),
        'moe_b_rg': nrm(ks[21], (DEPTH, MOE_GROUPS), 0.01),
        'moe_w_re': nrm(ks[22], (DEPTH, D_MODEL, N_EXPERTS), sd),
        'moe_b_re': nrm(ks[23], (DEPTH, N_EXPERTS), 0.01),
        'moe_w_gate': nrm(ks[24], (DEPTH, N_EXPERTS, D_MODEL, D_EXPERT), sd),
        'moe_w_up': nrm(ks[25], (DEPTH, N_EXPERTS, D_MODEL, D_EXPERT), sd),
        'moe_w_down': nrm(ks[26], (DEPTH, N_EXPERTS, D_EXPERT, D_MODEL), D_EXPERT ** -0.5 * DEEPNORM_BETA),
    }


def reference(x_prompt, x_sample, cache_k, cache_v, page_table, c_prompt, c_sample, w_ada, b_ada, ln_g, ln_b,
              attn_w_qkv, attn_w_o, gmlp_w_in, gmlp_vn_g, gmlp_vn_b, gmlp_w_s, gmlp_b_s, gmlp_w_out,
              moe_w_rg, moe_b_rg, moe_w_re, moe_b_re, moe_w_gate, moe_w_up, moe_w_down):
    pos_p = jnp.arange(SEQ, dtype=jnp.int32)
    pos_s = PAST_LEN + jnp.arange(DEC_SEQ, dtype=jnp.int32)
    xp, xs = x_prompt, x_sample
    new_kp, new_vp, new_ks, new_vs, new_gv = [], [], [], [], []
    for l in range(DEPTH):
        mp = adaln(c_prompt, w_ada[l], b_ada[l])
        ms = adaln(c_sample, w_ada[l], b_ada[l])
        hp = xp * (1.0 + mp[1]) + mp[0]
        hs = xs * (1.0 + ms[1]) + ms[0]
        if l % N_MIXERS == 0:
            la = l // N_MIXERS
            yp, ys, kp, vp, ksn, vsn = moba_mixer(hp, hs, pos_p, pos_s, cache_k[la], cache_v[la], page_table,
                                                 attn_w_qkv[la], attn_w_o[la])
            new_kp.append(kp)
            new_vp.append(vp)
            new_ks.append(ksn)
            new_vs.append(vsn)
        else:
            lb = l // N_MIXERS
            yp, _ = gmlp_mixer(hp, gmlp_w_in[lb], gmlp_vn_g[lb], gmlp_vn_b[lb], gmlp_w_s[lb], gmlp_b_s[lb], gmlp_w_out[lb])
            ys, gv = gmlp_mixer(hs, gmlp_w_in[lb], gmlp_vn_g[lb], gmlp_vn_b[lb], gmlp_w_s[lb], gmlp_b_s[lb], gmlp_w_out[lb])
            new_gv.append(gv)
        xp = post_norm(xp, yp, mp[2], ln_g[l, 0], ln_b[l, 0])
        xs = post_norm(xs, ys, ms[2], ln_g[l, 0], ln_b[l, 0])
        hp = xp * (1.0 + mp[4]) + mp[3]
        hs = xs * (1.0 + ms[4]) + ms[3]
        flat = jnp.concatenate([hp.reshape(-1, D_MODEL), hs.reshape(-1, D_MODEL)], axis=0)
        f = hier_moe(flat, moe_w_rg[l], moe_b_rg[l], moe_w_re[l], moe_b_re[l], moe_w_gate[l], moe_w_up[l], moe_w_down[l])
        fp = f[:BATCH * SEQ].reshape(BATCH, SEQ, D_MODEL)
        fs = f[BATCH * SEQ:].reshape(DEC_BATCH, DEC_SEQ, D_MODEL)
        xp = post_norm(xp, fp, mp[5], ln_g[l, 1], ln_b[l, 1])
        xs = post_norm(xs, fs, ms[5], ln_g[l, 1], ln_b[l, 1])
    return (xp, xs, jnp.stack(new_kp), jnp.stack(new_vp), jnp.stack(new_ks), jnp.stack(new_vs), jnp.stack(new_gv))
```

```python
import functools

import jax
import jax.numpy as jnp
from jax import lax
from jax.experimental import pallas as pl
from jax.experimental.pallas import tpu as pltpu

F32, BF16, I32 = jnp.float32, jnp.bfloat16, jnp.int32

N_HEADS = 16
MOBA_BLOCK = 256
MOBA_TOPK = 3
ROPE_THETA = 500000.0
GMLP_CHUNK = 128
GMLP_GROUPS = 8
MOE_GROUPS = 4
MOE_EXPERTS_PER_GROUP = 8
LN_EPS = 1e-5

LANES = 128
SUBLANES = 8
VMEM_LIMIT = 56 * 1024 * 1024

ROW_TILE = 512
SAMPLE_TILE = 128
MOE_ROWS = 128
COMBINE_TILE = 256
PAGES_PER_STEP = 4

NT_DIMS = (((1,), (1,)), ((), ()))


def _cparams(sem):
    return pltpu.CompilerParams(dimension_semantics=sem, vmem_limit_bytes=VMEM_LIMIT)


def _layer_norm(z, g, b):
    mu = jnp.mean(z, axis=-1, keepdims=True)
    zc = z - mu
    var = jnp.mean(zc * zc, axis=-1, keepdims=True)
    return zc * lax.rsqrt(var + LN_EPS) * g + b


def _adaln_body(c_ref, w_ref, b_ref, o_ref):
    a = jax.nn.silu(c_ref[...]).astype(BF16)
    o_ref[...] = jnp.dot(a, w_ref[...].astype(BF16), preferred_element_type=F32) + b_ref[...]


def _adaln(c_all, w_ada, b_ada):
    depth, d, n = w_ada.shape
    rows = c_all.shape[0]
    tn = 1024
    return pl.pallas_call(
        _adaln_body,
        grid=(depth, n // tn),
        in_specs=[
            pl.BlockSpec((rows, d), lambda l, j: (0, 0)),
            pl.BlockSpec((None, d, tn), lambda l, j: (l, 0, j)),
            pl.BlockSpec((None, 1, tn), lambda l, j: (l, 0, j)),
        ],
        out_specs=pl.BlockSpec((None, rows, tn), lambda l, j: (l, 0, j)),
        out_shape=jax.ShapeDtypeStruct((depth, rows, n), F32),
        compiler_params=_cparams(("arbitrary", "arbitrary")),
        name="adaln",
    )(c_all, w_ada, b_ada.reshape(depth, 1, n))


class _Mod:
    def __init__(self, arr, d, tiles_per_seq):
        self.arr = arr
        self.d = d
        self.tiles_per_seq = tiles_per_seq

    def spec(self, k, ngrid):
        d, tps = self.d, self.tiles_per_seq
        if tps is None:
            rows = self.arr.shape[0]
            if ngrid == 1:
                return pl.BlockSpec((rows, d), lambda i: (0, k))
            if ngrid == 2:
                return pl.BlockSpec((rows, d), lambda i, j: (0, k))
            return pl.BlockSpec((rows, d), lambda i, p, j: (0, k))
        if ngrid == 1:
            return pl.BlockSpec((None, 1, d), lambda i: (i // tps, 0, k))
        if ngrid == 2:
            return pl.BlockSpec((None, 1, d), lambda i, j: (i // tps, 0, k))
        return pl.BlockSpec((None, 1, d), lambda i, p, j: (i // tps, 0, k))


def _qkv_body(x_ref, sh_ref, sc_ref, w_ref, c_ref, s1_ref, s2_ref, q_ref, k_ref, v_ref, h_scr):
    p = pl.program_id(1)
    j = pl.program_id(2)

    @pl.when((p == 0) & (j == 0))
    def _():
        h_scr[...] = (x_ref[...] * (1.0 + sc_ref[...]) + sh_ref[...]).astype(BF16)

    y = jnp.dot(h_scr[...], w_ref[...], preferred_element_type=F32)
    tn = y.shape[1]

    def rope(y):
        reps = tn // LANES
        c = jnp.tile(c_ref[...], (1, reps))
        s1 = jnp.tile(s1_ref[...], (1, reps))
        s2 = jnp.tile(s2_ref[...], (1, reps))
        half = c_ref.shape[1] // 8
        return y * c + pltpu.roll(y, tn - half, 1) * s1 + pltpu.roll(y, half, 1) * s2

    @pl.when(p == 0)
    def _():
        q_ref[...] = rope(y).astype(q_ref.dtype)

    @pl.when(p == 1)
    def _():
        k_ref[...] = rope(y)

    @pl.when(p == 2)
    def _():
        v_ref[...] = y


def _qkv(x, mod, w_qkv_bf16, tabs, tab_index, tm, q_dtype):
    rows, d = x.shape
    tn = 512
    ncol = d // tn
    c_tab, s1_tab, s2_tab = tabs
    tab_spec = pl.BlockSpec((tm, LANES), lambda i, p, j: (tab_index(i), 0))
    out_q = pl.BlockSpec((tm, tn), lambda i, p, j: (i, jnp.where(p == 0, j, ncol - 1)))
    out_k = pl.BlockSpec((tm, tn), lambda i, p, j: (i, jnp.where(p < 1, 0, jnp.where(p == 1, j, ncol - 1))))
    out_v = pl.BlockSpec((tm, tn), lambda i, p, j: (i, jnp.where(p < 2, 0, j)))
    return pl.pallas_call(
        _qkv_body,
        grid=(rows // tm, 3, ncol),
        in_specs=[
            pl.BlockSpec((tm, d), lambda i, p, j: (i, 0)),
            mod.spec(0, 3),
            mod.spec(1, 3),
            pl.BlockSpec((d, tn), lambda i, p, j: (0, p * ncol + j)),
            tab_spec, tab_spec, tab_spec,
        ],
        out_specs=[out_q, out_k, out_v],
        out_shape=[
            jax.ShapeDtypeStruct((rows, d), q_dtype),
            jax.ShapeDtypeStruct((rows, d), F32),
            jax.ShapeDtypeStruct((rows, d), F32),
        ],
        scratch_shapes=[pltpu.VMEM((tm, d), BF16)],
        compiler_params=_cparams(("arbitrary", "arbitrary", "arbitrary")),
        name="qkv_rope",
    )(x, mod.arr, mod.arr, w_qkv_bf16, c_tab, s1_tab, s2_tab)


def _rope_tables(pos, head_dim):
    rot = head_dim // 4
    half = rot // 2
    inv = ROPE_THETA ** (-jnp.arange(half, dtype=F32) * 2.0 / rot)
    ang = pos.astype(F32)[:, None] * inv[None, :]
    cos, sin = jnp.cos(ang), jnp.sin(ang)
    n = pos.shape[0]
    ones = jnp.ones((n, head_dim - rot), F32)
    zeros = jnp.zeros((n, head_dim - rot), F32)
    zh = jnp.zeros((n, half), F32)
    c = jnp.concatenate([cos, cos, ones], axis=1)
    s1 = jnp.concatenate([-sin, zh, zeros], axis=1)
    s2 = jnp.concatenate([zh, sin, zeros], axis=1)
    return c, s1, s2


def _attn_prompt_body(q_ref, k_ref, v_ref, o_ref, kb_scr, vb_scr, sel_scr):
    seq, dh = q_ref.shape
    nblk = seq // MOBA_BLOCK
    scale = dh ** -0.5
    kf = k_ref[...]
    kb_scr[...] = kf.astype(BF16)
    vb_scr[...] = v_ref[...].astype(BF16)

    km = kf.reshape(nblk, MOBA_BLOCK, dh).sum(axis=1) * (1.0 / MOBA_BLOCK)
    g = lax.dot_general(km, q_ref[...].astype(F32), NT_DIMS, precision=lax.Precision.HIGHEST,
                        preferred_element_type=F32)
    blk = lax.broadcasted_iota(I32, (nblk, seq), 0)
    cur = lax.broadcasted_iota(I32, (nblk, seq), 1) // MOBA_BLOCK
    past = blk < cur
    g = jnp.where(past, g, -jnp.inf)
    cnt = jnp.zeros((nblk, seq), F32)
    for k in range(nblk):
        gk = g[k:k + 1, :]
        beats = (gk > g) | ((gk == g) & (k < blk))
        cnt = cnt + beats.astype(F32)
    sel = ((cnt < MOBA_TOPK) & past).astype(F32)
    sel_scr[...] = jnp.concatenate([sel, jnp.zeros((LANES - nblk, seq), F32)], axis=0).T

    row = lax.broadcasted_iota(I32, (MOBA_BLOCK, MOBA_BLOCK), 0)
    col = lax.broadcasted_iota(I32, (MOBA_BLOCK, MOBA_BLOCK), 1)
    causal = row >= col

    for c in range(nblk):
        lo, hi = c * MOBA_BLOCK, (c + 1) * MOBA_BLOCK
        qc = q_ref[lo:hi, :]
        selc = sel_scr[lo:hi, :]
        slabs = []
        for j in range(c + 1):
            s = lax.dot_general(qc, kb_scr[j * MOBA_BLOCK:(j + 1) * MOBA_BLOCK, :], NT_DIMS,
                                preferred_element_type=F32) * scale
            if j < c:
                s = jnp.where(selc[:, j:j + 1] > 0.5, s, -jnp.inf)
            else:
                s = jnp.where(causal, s, -jnp.inf)
            slabs.append(s)
        m = slabs[0].max(axis=1, keepdims=True)
        for s in slabs[1:]:
            m = jnp.maximum(m, s.max(axis=1, keepdims=True))
        l = jnp.zeros((MOBA_BLOCK, 1), F32)
        acc = jnp.zeros((MOBA_BLOCK, dh), F32)
        for j, s in enumerate(slabs):
            p = jnp.exp(s - m)
            l = l + p.sum(axis=1, keepdims=True)
            acc = acc + jnp.dot(p.astype(BF16), vb_scr[j * MOBA_BLOCK:(j + 1) * MOBA_BLOCK, :],
                                preferred_element_type=F32)
        o_ref[lo:hi, :] = (acc / l).astype(o_ref.dtype)


def _attn_prompt(q, k, v, batch, seq):
    rows, d = q.shape
    dh = d // N_HEADS
    spec = pl.BlockSpec((seq, dh), lambda b, h: (b, h))
    return pl.pallas_call(
        _attn_prompt_body,
        grid=(batch, N_HEADS),
        in_specs=[spec, spec, spec],
        out_specs=spec,
        out_shape=jax.ShapeDtypeStruct((rows, d), BF16),
        scratch_shapes=[pltpu.VMEM((seq, dh), BF16), pltpu.VMEM((seq, dh), BF16), pltpu.VMEM((seq, LANES), F32)],
        compiler_params=_cparams(("arbitrary", "arbitrary")),
        name="moba_prompt",
    )(q, k, v)


def _attn_sample_body(pt_ref, q_ref, kn_ref, vn_ref, hm_ref, *rest):
    npg = PAGES_PER_STEP
    k_refs = rest[:npg]
    v_refs = rest[npg:2 * npg]
    o_ref = rest[2 * npg]
    s_scr, p_scr, ksum_scr, acc_scr, pn_scr = rest[2 * npg + 1:]

    t = pl.program_id(1)
    n_pages = s_scr.shape[0]
    k_steps = n_pages // npg
    nq, dh = q_ref.shape
    page_rows = s_scr.shape[2]
    page = page_rows // N_HEADS
    pages_per_blk = MOBA_BLOCK // page
    nblk = n_pages // pages_per_blk
    scale = dh ** -0.5
    qf = q_ref[...]
    qb = qf.astype(BF16)

    @pl.when(t < k_steps)
    def _():
        sums = []
        for i in range(npg):
            kp = k_refs[i][...]
            sums.append(jnp.sum(kp, axis=0))
            kb = kp.reshape(page_rows, dh).astype(BF16)
            s_scr[t * npg + i] = lax.dot_general(qb, kb, NT_DIMS, preferred_element_type=F32)
        for b in range(npg // pages_per_blk):
            tot = sums[b * pages_per_blk]
            for i in range(1, pages_per_blk):
                tot = tot + sums[b * pages_per_blk + i]
            ksum_scr[t * (npg // pages_per_blk) + b] = tot

    @pl.when(t == k_steps - 1)
    def _():
        km = ksum_scr[...].reshape(nblk * N_HEADS, dh) * (1.0 / MOBA_BLOCK)
        g = lax.dot_general(qf, km, NT_DIMS, precision=lax.Precision.HIGHEST, preferred_element_type=F32)
        width = nblk * N_HEADS
        rowh = lax.broadcasted_iota(I32, (nq, width), 0) % N_HEADS
        lane = lax.broadcasted_iota(I32, (nq, width), 1)
        own = (lane % N_HEADS) == rowh
        blk = lane // N_HEADS
        g = jnp.where(own, g, -jnp.inf)
        cnt = jnp.zeros((nq, width), F32)
        for dlt in range(1, nblk):
            lower = pltpu.roll(g, dlt * N_HEADS, 1)
            cnt = cnt + ((blk >= dlt) & (lower >= g)).astype(F32)
            upper = pltpu.roll(g, width - dlt * N_HEADS, 1)
            cnt = cnt + ((blk + dlt < nblk) & (upper > g)).astype(F32)
        sel = jnp.where(own & (cnt < MOBA_TOPK), 1.0, 0.0)
        hm = hm_ref[...]

        kn = kn_ref[...]
        sn = lax.dot_general(qb, kn.astype(BF16), NT_DIMS, preferred_element_type=F32) * scale
        r = lax.broadcasted_iota(I32, (nq, nq), 0)
        cidx = lax.broadcasted_iota(I32, (nq, nq), 1)
        ok = ((r % N_HEADS) == (cidx % N_HEADS)) & ((cidx // N_HEADS) <= (r // N_HEADS))
        sn = jnp.where(ok, sn, -jnp.inf)
        m = sn.max(axis=1, keepdims=True)

        biases = []
        for j in range(nblk):
            picked = sel[:, j * N_HEADS:(j + 1) * N_HEADS].max(axis=1, keepdims=True)
            biases.append(jnp.where(picked > 0.5, 0.0, -jnp.inf))
        for pg in range(n_pages):
            s = s_scr[pg] * scale + hm + biases[pg // pages_per_blk]
            s_scr[pg] = s
            m = jnp.maximum(m, s.max(axis=1, keepdims=True))
        l = jnp.exp(sn - m).sum(axis=1, keepdims=True)
        for pg in range(n_pages):
            l = l + jnp.exp(s_scr[pg] - m).sum(axis=1, keepdims=True)
        inv = 1.0 / l
        for pg in range(n_pages):
            p_scr[pg] = (jnp.exp(s_scr[pg] - m) * inv).astype(BF16)
        pn = (jnp.exp(sn - m) * inv).astype(BF16)
        acc_scr[...] = jnp.dot(pn, vn_ref[...].astype(BF16), preferred_element_type=F32)

    @pl.when(t >= k_steps)
    def _():
        acc = acc_scr[...]
        for i in range(npg):
            vb = v_refs[i][...].reshape(page_rows, dh).astype(BF16)
            acc = acc + jnp.dot(p_scr[(t - k_steps) * npg + i], vb, preferred_element_type=F32)
        acc_scr[...] = acc

    @pl.when(t == 2 * k_steps - 1)
    def _():
        o_ref[...] = acc_scr[...]


def _attn_sample(q_rows, k_new, v_new, cache_k, cache_v, layer, page_table):
    nseq, nq, dh = q_rows.shape
    _, n_phys, page, heads, _ = cache_k.shape
    n_pages = page_table.shape[1]
    npg = PAGES_PER_STEP
    k_steps = n_pages // npg
    page_rows = page * heads
    rowh = jnp.arange(nq, dtype=I32)[:, None] % heads
    colh = jnp.arange(page_rows, dtype=I32)[None, :] % heads
    head_mask = jnp.where(rowh == colh, 0.0, -jnp.inf).astype(F32)

    def kmap(i):
        return lambda b, t, pt: (layer, pt[b * n_pages + jnp.minimum(t, k_steps - 1) * npg + i], 0, 0, 0)

    def vmap_(i):
        return lambda b, t, pt: (layer, pt[b * n_pages + jnp.maximum(t - k_steps, 0) * npg + i], 0, 0, 0)

    seq_spec = pl.BlockSpec((None, nq, dh), lambda b, t, pt: (b, 0, 0))
    page_block = (None, None, page, heads, dh)
    grid_spec = pltpu.PrefetchScalarGridSpec(
        num_scalar_prefetch=1,
        grid=(nseq, 2 * k_steps),
        in_specs=[seq_spec, seq_spec, seq_spec,
                  pl.BlockSpec((nq, page_rows), lambda b, t, pt: (0, 0))]
                 + [pl.BlockSpec(page_block, kmap(i)) for i in range(npg)]
                 + [pl.BlockSpec(page_block, vmap_(i)) for i in range(npg)],
        out_specs=seq_spec,
        scratch_shapes=[
            pltpu.VMEM((n_pages, nq, page_rows), F32),
            pltpu.VMEM((n_pages, nq, page_rows), BF16),
            pltpu.VMEM((n_pages * page // MOBA_BLOCK, heads, dh), F32),
            pltpu.VMEM((nq, dh), F32),
            pltpu.VMEM((nq, nq), BF16),
        ],
    )
    return pl.pallas_call(
        _attn_sample_body,
        grid_spec=grid_spec,
        out_shape=jax.ShapeDtypeStruct((nseq, nq, dh), F32),
        compiler_params=_cparams(("arbitrary", "arbitrary")),
        name="moba_sample",
    )(page_table.reshape(-1), q_rows, k_new, v_new, head_mask,
      *([cache_k] * npg), *([cache_v] * npg))


def _post_body(alpha, prologue, n_pro, *refs):
    pro_refs = refs[:n_pro]
    (x_ref, w_ref, gt_ref, lng_ref, lnb_ref, sh_ref, sc_ref, wrh_ref, wrl_ref, br_ref,
     x1_ref, h2c_ref, lg_ref) = refs[n_pro:]
    a = prologue(*pro_refs)
    y = jnp.dot(a, w_ref[...], preferred_element_type=F32)
    z = alpha * x_ref[...] + (1.0 + gt_ref[...]) * y
    x1 = _layer_norm(z, lng_ref[...], lnb_ref[...])
    x1_ref[...] = x1
    h2 = x1 * (1.0 + sc_ref[...]) + sh_ref[...]
    tm, d = h2.shape
    nchunk = d // LANES
    for c in range(nchunk):
        h2c_ref[pl.ds(c, tm, stride=nchunk), :] = h2[:, c * LANES:(c + 1) * LANES]
    hi = h2.astype(BF16)
    lo = (h2 - hi.astype(F32)).astype(BF16)
    wh, wl = wrh_ref[...], wrl_ref[...]
    lg = (lax.dot_general(wh, hi, NT_DIMS, preferred_element_type=F32)
          + lax.dot_general(wl, hi, NT_DIMS, preferred_element_type=F32)
          + lax.dot_general(wh, lo, NT_DIMS, preferred_element_type=F32))
    lg_ref[...] = lg + br_ref[...]


def _pro_identity(a_ref):
    return a_ref[...]


def _pro_gmlp_prompt(u_ref, v_ref, ws_ref, bexp_ref):
    tm, d = u_ref.shape
    gw = d // GMLP_GROUPS
    row = lax.broadcasted_iota(I32, (GMLP_CHUNK, GMLP_CHUNK), 0)
    col = lax.broadcasted_iota(I32, (GMLP_CHUNK, GMLP_CHUNK), 1)
    tri = row >= col
    wms = [jnp.where(tri, ws_ref[g], 0.0).astype(BF16) for g in range(GMLP_GROUPS)]
    outs = []
    for ch in range(tm // GMLP_CHUNK):
        lo, hi = ch * GMLP_CHUNK, (ch + 1) * GMLP_CHUNK
        cols = [jnp.dot(wms[g], v_ref[lo:hi, g * gw:(g + 1) * gw], preferred_element_type=F32)
                for g in range(GMLP_GROUPS)]
        mixed = jnp.concatenate(cols, axis=1) + bexp_ref[...]
        outs.append((u_ref[lo:hi, :].astype(F32) * mixed).astype(BF16))
    return jnp.concatenate(outs, axis=0)


def _pro_gmlp_sample(u_ref, v_ref, wexp_ref, bexp_ref):
    s = pl.program_id(0)
    tm = u_ref.shape[0]
    n_pos = v_ref.shape[0] // tm
    mixed = bexp_ref[pl.ds(s, 1), :]
    for j in range(n_pos):
        w = wexp_ref[pl.ds(s * n_pos + j, 1), :] * (j <= s).astype(F32)
        mixed = mixed + w * v_ref[j * tm:(j + 1) * tm, :]
    return (u_ref[...].astype(F32) * mixed).astype(BF16)


def _post(alpha, prologue, pro_args, pro_specs, x, w_bf16, mod, ln_g, ln_b, wr_hi, wr_lo, br, tm):
    rows, d = x.shape
    ne = wr_hi.shape[0]
    nchunk = d // LANES
    vec = pl.BlockSpec((1, d), lambda i: (0, 0))
    return pl.pallas_call(
        functools.partial(_post_body, alpha, prologue, len(pro_args)),
        grid=(rows // tm,),
        in_specs=list(pro_specs) + [
            pl.BlockSpec((tm, d), lambda i: (i, 0)),
            pl.BlockSpec(w_bf16.shape, lambda i: (0, 0)),
            mod.spec(2, 1), vec, vec, mod.spec(3, 1), mod.spec(4, 1),
            pl.BlockSpec((ne, d), lambda i: (0, 0)),
            pl.BlockSpec((ne, d), lambda i: (0, 0)),
            pl.BlockSpec((ne, 1), lambda i: (0, 0)),
        ],
        out_specs=[
            pl.BlockSpec((tm, d), lambda i: (i, 0)),
            pl.BlockSpec((tm * nchunk, LANES), lambda i: (i, 0)),
            pl.BlockSpec((ne, tm), lambda i: (0, i)),
        ],
        out_shape=[
            jax.ShapeDtypeStruct((rows, d), F32),
            jax.ShapeDtypeStruct((rows * nchunk, LANES), F32),
            jax.ShapeDtypeStruct((ne, rows), F32),
        ],
        compiler_params=_cparams(("arbitrary",)),
        name="proj_postnorm_router",
    )(*pro_args, x, w_bf16, mod.arr, ln_g, ln_b, mod.arr, mod.arr, wr_hi, wr_lo, br)


def _route_body(lg_ref, eid_ref, gate_ref, dest_ref, meta_ref):
    ne = MOE_GROUPS * MOE_EXPERTS_PER_GROUP
    epg = MOE_EXPERTS_PER_GROUP
    lgt = lg_ref[...]
    t = lgt.shape[1]
    row8 = lax.broadcasted_iota(I32, (SUBLANES, t), 0)
    lgp = jnp.where(row8 < MOE_GROUPS, lgt[0:SUBLANES], -jnp.inf)
    mg = lgp.max(axis=0, keepdims=True)
    grp = jnp.where(lgp == mg, row8, SUBLANES).min(axis=0, keepdims=True)
    pg = 1.0 / jnp.exp(lgp - mg).sum(axis=0, keepdims=True)

    le = jnp.zeros((epg, t), F32)
    for g in range(MOE_GROUPS):
        le = jnp.where(grp == g, lgt[SUBLANES + g * epg:SUBLANES + (g + 1) * epg], le)
    m1 = le.max(axis=0, keepdims=True)
    i1 = jnp.where(le == m1, row8, epg).min(axis=0, keepdims=True)
    le2 = jnp.where(row8 == i1, -jnp.inf, le)
    m2 = le2.max(axis=0, keepdims=True)
    i2 = jnp.where(le2 == m2, row8, epg).min(axis=0, keepdims=True)
    e = jnp.exp(m2 - m1)
    g1 = pg / (1.0 + e)
    g2 = pg * e / (1.0 + e)
    e1 = grp * epg + i1
    e2 = grp * epg + i2
    eid_ref[0:1, :] = e1
    eid_ref[1:2, :] = e2
    gate_ref[0:1, :] = g1
    gate_ref[1:2, :] = g2

    rows = lax.broadcasted_iota(I32, (ne, t), 0)
    oh1 = (rows == e1).astype(F32)
    oh2 = (rows == e2).astype(F32)
    oh = (oh1 + oh2).astype(BF16)
    cw = 512
    ur = lax.broadcasted_iota(I32, (cw, cw), 0)
    uc = lax.broadcasted_iota(I32, (cw, cw), 1)
    upper = (ur < uc).astype(BF16)
    carry = jnp.zeros((ne, 1), F32)
    pref = []
    for ci in range(t // cw):
        ohc = oh[:, ci * cw:(ci + 1) * cw]
        pref.append(jnp.dot(ohc, upper, preferred_element_type=F32) + carry)
        carry = carry + ohc.astype(F32).sum(axis=1, keepdims=True)
    cnt_before = jnp.concatenate(pref, axis=1)

    counts = carry
    shift = MOE_ROWS.bit_length() - 1
    padded = (((counts.astype(I32) + (MOE_ROWS - 1)) >> shift) << shift).astype(F32)
    lr = lax.broadcasted_iota(I32, (ne, ne), 0)
    lc = lax.broadcasted_iota(I32, (ne, ne), 1)
    lower = (lc < lr).astype(F32)
    pstart = jnp.dot(lower, jnp.broadcast_to(padded, (ne, LANES)), precision=lax.Precision.HIGHEST,
                     preferred_element_type=F32)[:, 0:1]
    pend = pstart + padded
    base = pstart + cnt_before
    d1 = (oh1 * base).sum(axis=0, keepdims=True)
    d2 = (oh2 * base).sum(axis=0, keepdims=True)
    dest_ref[0:1, :] = d1.astype(I32)
    dest_ref[1:2, :] = d2.astype(I32)

    mw = meta_ref.shape[1]
    blk_start = (lax.broadcasted_iota(I32, (ne, mw), 1) * MOE_ROWS).astype(F32)
    blk_e = jnp.minimum((pend <= blk_start).astype(F32).sum(axis=0, keepdims=True), ne - 1.0)
    last = lax.broadcasted_iota(I32, (ne, mw), 0) == ne - 1
    nused = jnp.where(last, jnp.broadcast_to(pend, (ne, mw)), 0.0).sum(axis=0, keepdims=True) * (1.0 / MOE_ROWS)
    lane_e = lax.broadcasted_iota(I32, (ne, mw), 1)
    diag = lane_e == lax.broadcasted_iota(I32, (ne, mw), 0)
    fill_lo = jnp.where(diag, jnp.broadcast_to(pstart + counts, (ne, mw)), 0.0).sum(axis=0, keepdims=True)
    fill_hi = jnp.where(diag, jnp.broadcast_to(pend, (ne, mw)), 0.0).sum(axis=0, keepdims=True)
    mrow = lax.broadcasted_iota(I32, (SUBLANES, mw), 0)
    meta = jnp.where(mrow == 0, blk_e, jnp.where(mrow == 1, nused, jnp.where(mrow == 2, fill_lo, fill_hi)))
    meta_ref[...] = meta.astype(I32)


def _route(logits_t, n_blocks_max):
    ne_pad, t = logits_t.shape
    mw = 256
    assert n_blocks_max <= mw
    return pl.pallas_call(
        _route_body,
        out_shape=[
            jax.ShapeDtypeStruct((2, t), I32),
            jax.ShapeDtypeStruct((2, t), F32),
            jax.ShapeDtypeStruct((2, t), I32),
            jax.ShapeDtypeStruct((SUBLANES, mw), I32),
        ],
        compiler_params=pltpu.CompilerParams(vmem_limit_bytes=VMEM_LIMIT),
        name="moe_route",
    )(logits_t)


def _dispatch_body(nchunk, n_prompt_tiles, fill_ref, dest_ref, hp_ref, hs_ref, xb_ref, zero_scr, sem):
    i = pl.program_id(0)
    tm = dest_ref.shape[1]
    n_ranges = fill_ref.shape[1]

    def slab(ref, row):
        return ref.at[pl.ds(pl.multiple_of(row * nchunk, nchunk), nchunk)]

    def issue(src_ref, base):
        def body(tok, carry):
            for k in range(2):
                pltpu.make_async_copy(slab(src_ref, base + tok), slab(xb_ref, dest_ref[k, tok]), sem).start()
            return carry
        lax.fori_loop(0, tm, body, 0)

    @pl.when(i < n_prompt_tiles)
    def _():
        issue(hp_ref, i * tm)

    @pl.when(i >= n_prompt_tiles)
    def _():
        issue(hs_ref, (i - n_prompt_tiles) * tm)

    def drain(n):
        def body(_, carry):
            pltpu.make_async_copy(slab(hp_ref, 0), slab(xb_ref, 0), sem).wait()
            return carry
        lax.fori_loop(0, n, body, 0)

    drain(2 * tm)

    @pl.when(i == 0)
    def _():
        zero_scr[...] = jnp.zeros_like(zero_scr)
        total = 0
        for e in range(n_ranges):
            lo, hi = fill_ref[0, e], fill_ref[1, e]

            def body(r, carry):
                pltpu.make_async_copy(zero_scr, slab(xb_ref, r), sem).start()
                return carry
            lax.fori_loop(lo, hi, body, 0)
            total = total + (hi - lo)

        def wbody(_, carry):
            pltpu.make_async_copy(zero_scr, slab(xb_ref, 0), sem).wait()
            return carry
        lax.fori_loop(0, total, wbody, 0)


def _dispatch(fill, dest_tiles, h2c_p, h2c_s, n_rows, nchunk):
    n_tiles, _, tm = dest_tiles.shape
    n_prompt_tiles = h2c_p.shape[0] // (nchunk * tm)
    grid_spec = pltpu.PrefetchScalarGridSpec(
        num_scalar_prefetch=1,
        grid=(n_tiles,),
        in_specs=[
            pl.BlockSpec((None, 2, tm), lambda i, f: (i, 0, 0), memory_space=pltpu.SMEM),
            pl.BlockSpec(memory_space=pl.ANY),
            pl.BlockSpec(memory_space=pl.ANY),
        ],
        out_specs=pl.BlockSpec(memory_space=pl.ANY),
        scratch_shapes=[pltpu.VMEM((nchunk, LANES), F32), pltpu.SemaphoreType.DMA(())],
    )
    return pl.pallas_call(
        functools.partial(_dispatch_body, nchunk, n_prompt_tiles),
        grid_spec=grid_spec,
        out_shape=jax.ShapeDtypeStruct((n_rows * nchunk, LANES), F32),
        compiler_params=_cparams(("arbitrary",)),
        name="moe_dispatch",
    )(fill, dest_tiles, h2c_p, h2c_s)


def _expert_body(blk_ref, nused_ref, x_ref, wg_ref, wu_ref, wd_ref, y_ref, wg_scr, wu_scr, wd_scr):
    i = pl.program_id(0)
    nchunk = x_ref.shape[0] // MOE_ROWS

    @pl.when(i < nused_ref[0])
    def _():
        prev = blk_ref[jnp.maximum(i - 1, 0)]

        @pl.when((i == 0) | (blk_ref[i] != prev))
        def _():
            wg_scr[...] = wg_ref[...].astype(BF16)
            wu_scr[...] = wu_ref[...].astype(BF16)
            wd_scr[...] = wd_ref[...].astype(BF16)

        x = jnp.concatenate([x_ref[pl.ds(c, MOE_ROWS, stride=nchunk), :] for c in range(nchunk)],
                            axis=1).astype(BF16)
        g = jnp.dot(x, wg_scr[...], preferred_element_type=F32)
        u = jnp.dot(x, wu_scr[...], preferred_element_type=F32)
        a = (jax.nn.silu(g) * u).astype(BF16)
        y = jnp.dot(a, wd_scr[...], preferred_element_type=F32)
        for c in range(nchunk):
            y_ref[pl.ds(c, MOE_ROWS, stride=nchunk), :] = y[:, c * LANES:(c + 1) * LANES]

    @pl.when(i >= nused_ref[0])
    def _():
        y_ref[...] = jnp.zeros_like(y_ref)


def _experts(blk_e, nused, xb, w_gate, w_up, w_down, n_blocks):
    ne, d, de = w_gate.shape
    nchunk = d // LANES
    rows_c = MOE_ROWS * nchunk

    def blk(i, be, nu):
        return jnp.minimum(i, nu[0] - 1)

    grid_spec = pltpu.PrefetchScalarGridSpec(
        num_scalar_prefetch=2,
        grid=(n_blocks,),
        in_specs=[
            pl.BlockSpec((rows_c, LANES), lambda i, be, nu: (blk(i, be, nu), 0)),
            pl.BlockSpec((None, d, de), lambda i, be, nu: (be[blk(i, be, nu)], 0, 0)),
            pl.BlockSpec((None, d, de), lambda i, be, nu: (be[blk(i, be, nu)], 0, 0)),
            pl.BlockSpec((None, de, d), lambda i, be, nu: (be[blk(i, be, nu)], 0, 0)),
        ],
        out_specs=pl.BlockSpec((rows_c, LANES), lambda i, be, nu: (i, 0)),
        scratch_shapes=[pltpu.VMEM((d, de), BF16), pltpu.VMEM((d, de), BF16), pltpu.VMEM((de, d), BF16)],
    )
    return pl.pallas_call(
        _expert_body,
        grid_spec=grid_spec,
        out_shape=jax.ShapeDtypeStruct(xb.shape, F32),
        compiler_params=_cparams(("arbitrary",)),
        name="moe_experts",
    )(blk_e, nused, xb, w_gate, w_up, w_down)


def _combine_body(alpha, dest_ref, x_ref, gts_ref, gt_ref, lng_ref, lnb_ref, yb_ref, o_ref, buf, sem):
    tm, d = x_ref.shape
    nchunk = d // LANES

    def slab(ref, row):
        return ref.at[pl.ds(pl.multiple_of(row * nchunk, nchunk), nchunk)]

    def body(tok, carry):
        for k in range(2):
            pltpu.make_async_copy(slab(yb_ref, dest_ref[k, tok]), slab(buf.at[k], tok), sem).start()
        return carry
    lax.fori_loop(0, tm, body, 0)

    def wbody(_, carry):
        pltpu.make_async_copy(slab(yb_ref, 0), slab(buf.at[0], 0), sem).wait()
        return carry
    lax.fori_loop(0, 2 * tm, wbody, 0)

    gts = gts_ref[...]
    f = jnp.zeros((tm, d), F32)
    for k in range(2):
        yk = jnp.concatenate([buf[k, pl.ds(c, tm, stride=nchunk), :] for c in range(nchunk)], axis=1)
        f = f + gts[:, k:k + 1] * yk
    z = alpha * x_ref[...] + (1.0 + gt_ref[...]) * f
    o_ref[...] = _layer_norm(z, lng_ref[...], lnb_ref[...])


def _combine(alpha, dest_tiles, x1, gates_t, mod, ln_g, ln_b, yb):
    rows, d = x1.shape
    n_tiles, _, tm = dest_tiles.shape
    nchunk = d // LANES
    vec = pl.BlockSpec((1, d), lambda i: (0, 0))
    if mod.tiles_per_seq is not None:
        mod = _Mod(mod.arr, d, mod.tiles_per_seq * (ROW_TILE // tm))
    return pl.pallas_call(
        functools.partial(_combine_body, alpha),
        grid=(n_tiles,),
        in_specs=[
            pl.BlockSpec((None, 2, tm), lambda i: (i, 0, 0), memory_space=pltpu.SMEM),
            pl.BlockSpec((tm, d), lambda i: (i, 0)),
            pl.BlockSpec((tm, 2), lambda i: (i, 0)),
            mod.spec(5, 1), vec, vec,
            pl.BlockSpec(memory_space=pl.ANY),
        ],
        out_specs=pl.BlockSpec((tm, d), lambda i: (i, 0)),
        out_shape=jax.ShapeDtypeStruct((rows, d), F32),
        scratch_shapes=[pltpu.VMEM((2, tm * nchunk, LANES), F32), pltpu.SemaphoreType.DMA(())],
        compiler_params=_cparams(("arbitrary",)),
        name="moe_combine_postnorm",
    )(dest_tiles, x1, gates_t, mod.arr, ln_g, ln_b, yb)


def _gmlp_in_body(x_ref, sh_ref, sc_ref, w_ref, vg_ref, vb_ref, u_ref, v_ref, h_scr, v_scr):
    j = pl.program_id(1)
    half = v_scr.shape[0]

    @pl.when(j == 0)
    def _():
        h_scr[...] = (x_ref[...] * (1.0 + sc_ref[...]) + sh_ref[...]).astype(BF16)

    y = jax.nn.gelu(jnp.dot(h_scr[...], w_ref[...], preferred_element_type=F32))

    @pl.when(j < half)
    def _():
        u_ref[...] = y.astype(u_ref.dtype)

    @pl.when(j >= half)
    def _():
        v_scr[j - half] = y

    @pl.when(j == 2 * half - 1)
    def _():
        v = jnp.concatenate([v_scr[c] for c in range(half)], axis=1)
        v_ref[...] = _layer_norm(v, vg_ref[...], vb_ref[...]).astype(v_ref.dtype)


def _gmlp_in(x, mod, w_in_bf16, vn_g, vn_b, tm, v_dtype):
    rows, d = x.shape
    tn = 512
    half = d // tn
    vec = pl.BlockSpec((1, d), lambda i, j: (0, 0))
    return pl.pallas_call(
        _gmlp_in_body,
        grid=(rows // tm, 2 * half),
        in_specs=[
            pl.BlockSpec((tm, d), lambda i, j: (i, 0)),
            mod.spec(0, 2), mod.spec(1, 2),
            pl.BlockSpec((d, tn), lambda i, j: (0, j)),
            vec, vec,
        ],
        out_specs=[
            pl.BlockSpec((tm, tn), lambda i, j: (i, jnp.minimum(j, half - 1))),
            pl.BlockSpec((tm, d), lambda i, j: (i, 0)),
        ],
        out_shape=[jax.ShapeDtypeStruct((rows, d), BF16), jax.ShapeDtypeStruct((rows, d), v_dtype)],
        scratch_shapes=[pltpu.VMEM((tm, d), BF16), pltpu.VMEM((half, tm, tn), F32)],
        compiler_params=_cparams(("arbitrary", "arbitrary")),
        name="gmlp_in",
    )(x, mod.arr, mod.arr, w_in_bf16, vn_g, vn_b)


def _moe_and_norm(alpha, x1_p, x1_s, h2c_p, h2c_s, lg_p, lg_s, mod_p, mod_s, ln_g, ln_b,
                  w_gate, w_up, w_down):
    t_p, d = x1_p.shape
    t_s = x1_s.shape[0]
    t = t_p + t_s
    nchunk = d // LANES
    ne = w_gate.shape[0]
    n_blocks = -(-(2 * t) // MOE_ROWS) + ne
    n_rows = n_blocks * MOE_ROWS

    eid, gates, dest, meta = _route(jnp.concatenate([lg_p, lg_s], axis=1), n_blocks)
    blk_e = meta[0, :n_blocks]
    nused = meta[1, :1]
    tail = jnp.stack([nused * MOE_ROWS, jnp.full((1,), n_rows, I32)])
    fill = jnp.concatenate([meta[2:4, :ne], tail], axis=1)

    def tiles(a, tm):
        return a.reshape(2, -1, tm).transpose(1, 0, 2)

    xb = _dispatch(fill, tiles(dest, ROW_TILE), h2c_p, h2c_s, n_rows, nchunk)
    yb = _experts(blk_e, nused, xb, w_gate, w_up, w_down, n_blocks)
    gates_t = gates.T
    tm = COMBINE_TILE
    x2_p = _combine(alpha, tiles(dest[:, :t_p], tm), x1_p, gates_t[:t_p], mod_p, ln_g, ln_b, yb)
    x2_s = _combine(alpha, tiles(dest[:, t_p:], min(tm, SAMPLE_TILE)), x1_s, gates_t[t_p:], mod_s, ln_g, ln_b, yb)
    return x2_p, x2_s


def _router_weights(w_rg, b_rg, w_re, b_re):
    d = w_rg.shape[0]
    pad = jnp.zeros((SUBLANES - MOE_GROUPS, d), F32)
    w = jnp.concatenate([w_rg.T, pad, w_re.T], axis=0)
    b = jnp.concatenate([b_rg, jnp.zeros((SUBLANES - MOE_GROUPS,), F32), b_re])[:, None]
    hi = w.astype(BF16)
    lo = (w - hi.astype(F32)).astype(BF16)
    return hi, lo, b


def kernel(x_prompt, x_sample, cache_k, cache_v, page_table, c_prompt, c_sample, w_ada, b_ada, ln_g, ln_b,
           attn_w_qkv, attn_w_o, gmlp_w_in, gmlp_vn_g, gmlp_vn_b, gmlp_w_s, gmlp_b_s, gmlp_w_out,
           moe_w_rg, moe_b_rg, moe_w_re, moe_b_re, moe_w_gate, moe_w_up, moe_w_down):
    batch, seq, d = x_prompt.shape
    nseq, dec_seq, _ = x_sample.shape
    depth = w_ada.shape[0]
    dh = d // N_HEADS
    past_len = page_table.shape[1] * cache_k.shape[2]
    alpha = (2.0 * depth) ** 0.25
    tiles_per_seq = seq // ROW_TILE

    xp = x_prompt.reshape(batch * seq, d)
    xs = x_sample.transpose(1, 0, 2).reshape(dec_seq * nseq, d)

    pad_rows = (-(batch + nseq)) % SUBLANES
    c_all = jnp.concatenate([c_prompt, c_sample, jnp.zeros((pad_rows, d), F32)], axis=0)
    m_all = _adaln(c_all, w_ada, b_ada)

    tabs_p = _rope_tables(jnp.arange(seq, dtype=I32), dh)
    tabs_s = _rope_tables(past_len + jnp.repeat(jnp.arange(dec_seq, dtype=I32), nseq), dh)

    new_kp, new_vp, new_ks, new_vs, new_gv = [], [], [], [], []
    for l in range(depth):
        mod_p = _Mod(m_all[l, :batch].reshape(batch, 1, 6 * d), d, tiles_per_seq)
        mod_s = _Mod(m_all[l, batch:batch + nseq], d, None)
        lng0, lnb0 = ln_g[l, 0][None, :], ln_b[l, 0][None, :]
        lng1, lnb1 = ln_g[l, 1][None, :], ln_b[l, 1][None, :]
        wr_hi, wr_lo, br = _router_weights(moe_w_rg[l], moe_b_rg[l], moe_w_re[l], moe_b_re[l])
        post = functools.partial(_post, alpha)

        if l % 2 == 0:
            la = l // 2
            w_qkv = attn_w_qkv[la].astype(BF16)
            w_o = attn_w_o[la].astype(BF16)
            q_p, k_p, v_p = _qkv(xp, mod_p, w_qkv, tabs_p, lambda i: i % tiles_per_seq, ROW_TILE, BF16)
            q_s, k_s, v_s = _qkv(xs, mod_s, w_qkv, tabs_s, lambda i: i, SAMPLE_TILE, F32)
            o_p = _attn_prompt(q_p, k_p, v_p, batch, seq)

            def seq_rows(a):
                return a.reshape(dec_seq, nseq, N_HEADS, dh).transpose(1, 0, 2, 3).reshape(nseq, dec_seq * N_HEADS, dh)

            o_s = _attn_sample(seq_rows(q_s), seq_rows(k_s), seq_rows(v_s), cache_k, cache_v, la, page_table)
            o_s = o_s.reshape(nseq, dec_seq, d).transpose(1, 0, 2).reshape(dec_seq * nseq, d).astype(BF16)
            new_kp.append(k_p.reshape(batch, seq, N_HEADS, dh))
            new_vp.append(v_p.reshape(batch, seq, N_HEADS, dh))
            new_ks.append(k_s.reshape(dec_seq, nseq, N_HEADS, dh).transpose(1, 0, 2, 3))
            new_vs.append(v_s.reshape(dec_seq, nseq, N_HEADS, dh).transpose(1, 0, 2, 3))
            x1_p, h2c_p, lg_p = post(_pro_identity, (o_p,), [pl.BlockSpec((ROW_TILE, d), lambda i: (i, 0))],
                                     xp, w_o, mod_p, lng0, lnb0, wr_hi, wr_lo, br, ROW_TILE)
            x1_s, h2c_s, lg_s = post(_pro_identity, (o_s,), [pl.BlockSpec((SAMPLE_TILE, d), lambda i: (i, 0))],
                                     xs, w_o, mod_s, lng0, lnb0, wr_hi, wr_lo, br, SAMPLE_TILE)
        else:
            lb = l // 2
            w_in = gmlp_w_in[lb].astype(BF16)
            w_out = gmlp_w_out[lb].astype(BF16)
            vg, vb = gmlp_vn_g[lb][None, :], gmlp_vn_b[lb][None, :]
            u_p, vn_p = _gmlp_in(xp, mod_p, w_in, vg, vb, ROW_TILE, BF16)
            u_s, vn_s = _gmlp_in(xs, mod_s, w_in, vg, vb, SAMPLE_TILE, F32)
            new_gv.append(vn_s.reshape(dec_seq, nseq, d).transpose(1, 0, 2))
            gw = d // GMLP_GROUPS
            bexp = jnp.repeat(gmlp_b_s[lb].T, gw, axis=1)
            wexp = jnp.repeat(gmlp_w_s[lb][:, :dec_seq, :dec_seq].transpose(1, 2, 0).reshape(dec_seq * dec_seq, -1),
                              gw, axis=1)
            tile = pl.BlockSpec((ROW_TILE, d), lambda i: (i, 0))
            x1_p, h2c_p, lg_p = post(
                _pro_gmlp_prompt, (u_p, vn_p, gmlp_w_s[lb], bexp),
                [tile, tile, pl.BlockSpec(gmlp_w_s[lb].shape, lambda i: (0, 0, 0)),
                 pl.BlockSpec(bexp.shape, lambda i: (0, 0))],
                xp, w_out, mod_p, lng0, lnb0, wr_hi, wr_lo, br, ROW_TILE)
            x1_s, h2c_s, lg_s = post(
                _pro_gmlp_sample, (u_s, vn_s, wexp, bexp[:SUBLANES]),
                [pl.BlockSpec((SAMPLE_TILE, d), lambda i: (i, 0)),
                 pl.BlockSpec(vn_s.shape, lambda i: (0, 0)),
                 pl.BlockSpec(wexp.shape, lambda i: (0, 0)),
                 pl.BlockSpec((SUBLANES, d), lambda i: (0, 0))],
                xs, w_out, mod_s, lng0, lnb0, wr_hi, wr_lo, br, SAMPLE_TILE)

        xp, xs = _moe_and_norm(alpha, x1_p, x1_s, h2c_p, h2c_s, lg_p, lg_s, mod_p, mod_s, lng1, lnb1,
                               moe_w_gate[l], moe_w_up[l], moe_w_down[l])

    y_p = xp.reshape(batch, seq, d)
    y_s = xs.reshape(dec_seq, nseq, d).transpose(1, 0, 2)
    return (y_p, y_s, jnp.stack(new_kp), jnp.stack(new_vp), jnp.stack(new_ks), jnp.stack(new_vs),
            jnp.stack(new_gv))
```

```python
import functools

import jax
import jax.numpy as jnp
from jax import lax
from jax.experimental import pallas as pl
from jax.experimental.pallas import tpu as pltpu

F32, BF16, I32 = jnp.float32, jnp.bfloat16, jnp.int32

N_HEADS = 16
MOBA_BLOCK = 256
MOBA_TOPK = 3
ROPE_THETA = 500000.0
GMLP_CHUNK = 128
GMLP_GROUPS = 8
MOE_GROUPS = 4
MOE_EXPERTS_PER_GROUP = 8
LN_EPS = 1e-5

LANES = 128
SUBLANES = 8
VMEM_LIMIT = 56 * 1024 * 1024

ROW_TILE = 512
IN_TILE = 1024
SAMPLE_TILE = 128
MOE_ROWS = 128
COMBINE_TILE = 256
PAGES_PER_STEP = 4

NT_DIMS = (((1,), (1,)), ((), ()))


def _cparams(sem):
    return pltpu.CompilerParams(dimension_semantics=sem, vmem_limit_bytes=VMEM_LIMIT)


def _layer_norm(z, g, b):
    mu = jnp.mean(z, axis=-1, keepdims=True)
    zc = z - mu
    var = jnp.mean(zc * zc, axis=-1, keepdims=True)
    return zc * lax.rsqrt(var + LN_EPS) * g + b


def _adaln_body(c_ref, w_ref, b_ref, o_ref):
    a = jax.nn.silu(c_ref[...]).astype(BF16)
    o_ref[...] = jnp.dot(a, w_ref[...].astype(BF16), preferred_element_type=F32) + b_ref[...]


def _adaln(c_all, w_ada, b_ada):
    depth, d, n = w_ada.shape
    rows = c_all.shape[0]
    tn = 1024
    return pl.pallas_call(
        _adaln_body,
        grid=(depth, n // tn),
        in_specs=[
            pl.BlockSpec((rows, d), lambda l, j: (0, 0)),
            pl.BlockSpec((None, d, tn), lambda l, j: (l, 0, j)),
            pl.BlockSpec((None, 1, tn), lambda l, j: (l, 0, j)),
        ],
        out_specs=pl.BlockSpec((None, rows, tn), lambda l, j: (l, 0, j)),
        out_shape=jax.ShapeDtypeStruct((depth, rows, n), F32),
        compiler_params=_cparams(("arbitrary", "arbitrary")),
        name="adaln",
    )(c_all, w_ada, b_ada.reshape(depth, 1, n))


class _Mod:
    def __init__(self, arr, d, tiles_per_seq):
        self.arr = arr
        self.d = d
        self.tiles_per_seq = tiles_per_seq

    def spec(self, k, ngrid):
        d, tps = self.d, self.tiles_per_seq
        if tps is None:
            rows = self.arr.shape[0]
            if ngrid == 1:
                return pl.BlockSpec((rows, d), lambda i: (0, k))
            if ngrid == 2:
                return pl.BlockSpec((rows, d), lambda i, j: (0, k))
            return pl.BlockSpec((rows, d), lambda i, p, j: (0, k))
        if ngrid == 1:
            return pl.BlockSpec((None, 1, d), lambda i: (i // tps, 0, k))
        if ngrid == 2:
            return pl.BlockSpec((None, 1, d), lambda i, j: (i // tps, 0, k))
        return pl.BlockSpec((None, 1, d), lambda i, p, j: (i // tps, 0, k))


def _qkv_body(x_ref, sh_ref, sc_ref, w_ref, c_ref, s1_ref, s2_ref, q_ref, k_ref, v_ref, h_scr):
    p = pl.program_id(1)
    j = pl.program_id(2)

    @pl.when((p == 0) & (j == 0))
    def _():
        h_scr[...] = (x_ref[...] * (1.0 + sc_ref[...]) + sh_ref[...]).astype(BF16)

    y = jnp.dot(h_scr[...], w_ref[...], preferred_element_type=F32)
    tn = y.shape[1]

    def rope(y):
        reps = tn // LANES
        c = jnp.tile(c_ref[...], (1, reps))
        s1 = jnp.tile(s1_ref[...], (1, reps))
        s2 = jnp.tile(s2_ref[...], (1, reps))
        half = c_ref.shape[1] // 8
        return y * c + pltpu.roll(y, tn - half, 1) * s1 + pltpu.roll(y, half, 1) * s2

    @pl.when(p == 0)
    def _():
        q_ref[...] = rope(y).astype(q_ref.dtype)

    @pl.when(p == 1)
    def _():
        k_ref[...] = rope(y)

    @pl.when(p == 2)
    def _():
        v_ref[...] = y


def _qkv(x, mod, w_qkv_bf16, tabs, tab_index, tm, q_dtype):
    rows, d = x.shape
    tn = 512
    ncol = d // tn
    c_tab, s1_tab, s2_tab = tabs
    tab_spec = pl.BlockSpec((tm, LANES), lambda i, p, j: (tab_index(i), 0))
    out_q = pl.BlockSpec((tm, tn), lambda i, p, j: (i, jnp.where(p == 0, j, ncol - 1)))
    out_k = pl.BlockSpec((tm, tn), lambda i, p, j: (i, jnp.where(p < 1, 0, jnp.where(p == 1, j, ncol - 1))))
    out_v = pl.BlockSpec((tm, tn), lambda i, p, j: (i, jnp.where(p < 2, 0, j)))
    return pl.pallas_call(
        _qkv_body,
        grid=(rows // tm, 3, ncol),
        in_specs=[
            pl.BlockSpec((tm, d), lambda i, p, j: (i, 0)),
            mod.spec(0, 3),
            mod.spec(1, 3),
            pl.BlockSpec((d, tn), lambda i, p, j: (0, p * ncol + j)),
            tab_spec, tab_spec, tab_spec,
        ],
        out_specs=[out_q, out_k, out_v],
        out_shape=[
            jax.ShapeDtypeStruct((rows, d), q_dtype),
            jax.ShapeDtypeStruct((rows, d), F32),
            jax.ShapeDtypeStruct((rows, d), F32),
        ],
        scratch_shapes=[pltpu.VMEM((tm, d), BF16)],
        compiler_params=_cparams(("arbitrary", "arbitrary", "arbitrary")),
        name="qkv_rope",
    )(x, mod.arr, mod.arr, w_qkv_bf16, c_tab, s1_tab, s2_tab)


def _rope_tables(pos, head_dim):
    rot = head_dim // 4
    half = rot // 2
    inv = ROPE_THETA ** (-jnp.arange(half, dtype=F32) * 2.0 / rot)
    ang = pos.astype(F32)[:, None] * inv[None, :]
    cos, sin = jnp.cos(ang), jnp.sin(ang)
    n = pos.shape[0]
    ones = jnp.ones((n, head_dim - rot), F32)
    zeros = jnp.zeros((n, head_dim - rot), F32)
    zh = jnp.zeros((n, half), F32)
    c = jnp.concatenate([cos, cos, ones], axis=1)
    s1 = jnp.concatenate([-sin, zh, zeros], axis=1)
    s2 = jnp.concatenate([zh, sin, zeros], axis=1)
    return c, s1, s2


def _attn_prompt_body(q_ref, k_ref, v_ref, o_ref, kb_scr, vb_scr, sel_scr):
    seq, dh = q_ref.shape
    nblk = seq // MOBA_BLOCK
    scale = dh ** -0.5
    kf = k_ref[...]
    kb_scr[...] = kf.astype(BF16)
    vb_scr[...] = v_ref[...].astype(BF16)

    km = kf.reshape(nblk, MOBA_BLOCK, dh).sum(axis=1) * (1.0 / MOBA_BLOCK)
    g = lax.dot_general(km, q_ref[...].astype(F32), NT_DIMS, precision=lax.Precision.HIGHEST,
                        preferred_element_type=F32)
    blk = lax.broadcasted_iota(I32, (nblk, seq), 0)
    cur = lax.broadcasted_iota(I32, (nblk, seq), 1) // MOBA_BLOCK
    past = blk < cur
    g = jnp.where(past, g, -jnp.inf)
    cnt = jnp.zeros((nblk, seq), F32)
    for k in range(nblk):
        gk = g[k:k + 1, :]
        beats = (gk > g) | ((gk == g) & (k < blk))
        cnt = cnt + beats.astype(F32)
    sel = ((cnt < MOBA_TOPK) & past).astype(F32)
    sel_scr[...] = jnp.concatenate([sel, jnp.zeros((LANES - nblk, seq), F32)], axis=0).T

    row = lax.broadcasted_iota(I32, (MOBA_BLOCK, MOBA_BLOCK), 0)
    col = lax.broadcasted_iota(I32, (MOBA_BLOCK, MOBA_BLOCK), 1)
    causal = row >= col

    for c in range(nblk):
        lo, hi = c * MOBA_BLOCK, (c + 1) * MOBA_BLOCK
        qc = q_ref[lo:hi, :]
        selc = sel_scr[lo:hi, :]
        slabs = []
        for j in range(c + 1):
            s = lax.dot_general(qc, kb_scr[j * MOBA_BLOCK:(j + 1) * MOBA_BLOCK, :], NT_DIMS,
                                preferred_element_type=F32) * scale
            if j < c:
                s = jnp.where(selc[:, j:j + 1] > 0.5, s, -jnp.inf)
            else:
                s = jnp.where(causal, s, -jnp.inf)
            slabs.append(s)
        m = slabs[0].max(axis=1, keepdims=True)
        for s in slabs[1:]:
            m = jnp.maximum(m, s.max(axis=1, keepdims=True))
        l = jnp.zeros((MOBA_BLOCK, 1), F32)
        acc = jnp.zeros((MOBA_BLOCK, dh), F32)
        for j, s in enumerate(slabs):
            p = jnp.exp(s - m)
            l = l + p.sum(axis=1, keepdims=True)
            acc = acc + jnp.dot(p.astype(BF16), vb_scr[j * MOBA_BLOCK:(j + 1) * MOBA_BLOCK, :],
                                preferred_element_type=F32)
        o_ref[lo:hi, :] = (acc / l).astype(o_ref.dtype)


def _attn_prompt(q, k, v, batch, seq):
    rows, d = q.shape
    dh = d // N_HEADS
    spec = pl.BlockSpec((seq, dh), lambda b, h: (b, h))
    return pl.pallas_call(
        _attn_prompt_body,
        grid=(batch, N_HEADS),
        in_specs=[spec, spec, spec],
        out_specs=spec,
        out_shape=jax.ShapeDtypeStruct((rows, d), BF16),
        scratch_shapes=[pltpu.VMEM((seq, dh), BF16), pltpu.VMEM((seq, dh), BF16), pltpu.VMEM((seq, LANES), F32)],
        compiler_params=_cparams(("arbitrary", "arbitrary")),
        name="moba_prompt",
    )(q, k, v)


def _attn_sample_body(pt_ref, q_ref, kn_ref, vn_ref, hm_ref, *rest):
    npg = PAGES_PER_STEP
    k_refs = rest[:npg]
    v_refs = rest[npg:2 * npg]
    o_ref = rest[2 * npg]
    s_scr, p_scr, ksum_scr, acc_scr, linv_scr = rest[2 * npg + 1:]

    t = pl.program_id(1)
    n_pages = s_scr.shape[0]
    k_steps = n_pages // npg
    nq, dh = q_ref.shape
    page_rows = s_scr.shape[2]
    page = page_rows // N_HEADS
    pages_per_blk = MOBA_BLOCK // page
    nblk = n_pages // pages_per_blk
    scale = dh ** -0.5
    qf = q_ref[...]
    qb = qf.astype(BF16)

    @pl.when(t < k_steps)
    def _():
        sums = []
        for i in range(npg):
            kp = k_refs[i][...]
            sums.append(jnp.sum(kp, axis=0))
            kb = kp.reshape(page_rows, dh).astype(BF16)
            s_scr[t * npg + i] = lax.dot_general(qb, kb, NT_DIMS, preferred_element_type=F32)
        for b in range(npg // pages_per_blk):
            tot = sums[b * pages_per_blk]
            for i in range(1, pages_per_blk):
                tot = tot + sums[b * pages_per_blk + i]
            ksum_scr[t * (npg // pages_per_blk) + b] = tot

    @pl.when(t == k_steps - 1)
    def _():
        km = ksum_scr[...].reshape(nblk * N_HEADS, dh) * (1.0 / MOBA_BLOCK)
        g = lax.dot_general(qf, km, NT_DIMS, precision=lax.Precision.HIGHEST, preferred_element_type=F32)
        width = nblk * N_HEADS
        rowh = lax.broadcasted_iota(I32, (nq, width), 0) % N_HEADS
        lane = lax.broadcasted_iota(I32, (nq, width), 1)
        own = (lane % N_HEADS) == rowh
        blk = lane // N_HEADS
        g = jnp.where(own, g, -jnp.inf)
        cnt = jnp.zeros((nq, width), F32)
        for dlt in range(1, nblk):
            lower = pltpu.roll(g, dlt * N_HEADS, 1)
            cnt = cnt + ((blk >= dlt) & (lower >= g)).astype(F32)
            upper = pltpu.roll(g, width - dlt * N_HEADS, 1)
            cnt = cnt + ((blk + dlt < nblk) & (upper > g)).astype(F32)
        sel = jnp.where(own & (cnt < MOBA_TOPK), 1.0, 0.0)
        hm = hm_ref[...]

        kn = kn_ref[...]
        sn = lax.dot_general(qb, kn.astype(BF16), NT_DIMS, preferred_element_type=F32) * scale
        r = lax.broadcasted_iota(I32, (nq, nq), 0)
        cidx = lax.broadcasted_iota(I32, (nq, nq), 1)
        ok = ((r % N_HEADS) == (cidx % N_HEADS)) & ((cidx // N_HEADS) <= (r // N_HEADS))
        sn = jnp.where(ok, sn, -jnp.inf)
        m = sn.max(axis=1, keepdims=True)

        biases = []
        for j in range(nblk):
            picked = sel[:, j * N_HEADS:(j + 1) * N_HEADS].max(axis=1, keepdims=True)
            biases.append(jnp.where(picked > 0.5, 0.0, -jnp.inf))
        for pg in range(n_pages):
            s = s_scr[pg] * scale + hm + biases[pg // pages_per_blk]
            s_scr[pg] = s
            m = jnp.maximum(m, s.max(axis=1, keepdims=True))
        en = jnp.exp(sn - m)
        l = en.sum(axis=1, keepdims=True)
        for pg in range(n_pages):
            e = jnp.exp(s_scr[pg] - m)
            l = l + e.sum(axis=1, keepdims=True)
            p_scr[pg] = e.astype(BF16)
        linv_scr[...] = jnp.broadcast_to(1.0 / l, linv_scr.shape)
        acc_scr[...] = jnp.dot(en.astype(BF16), vn_ref[...].astype(BF16), preferred_element_type=F32)

    @pl.when(t >= k_steps)
    def _():
        acc = acc_scr[...]
        for i in range(npg):
            vb = v_refs[i][...].reshape(page_rows, dh).astype(BF16)
            acc = acc + jnp.dot(p_scr[(t - k_steps) * npg + i], vb, preferred_element_type=F32)
        acc_scr[...] = acc

    @pl.when(t == 2 * k_steps - 1)
    def _():
        o_ref[...] = acc_scr[...] * linv_scr[...]


def _attn_sample(q_rows, k_new, v_new, cache_k, cache_v, layer, page_table):
    nseq, nq, dh = q_rows.shape
    _, n_phys, page, heads, _ = cache_k.shape
    n_pages = page_table.shape[1]
    npg = PAGES_PER_STEP
    k_steps = n_pages // npg
    page_rows = page * heads
    rowh = jnp.arange(nq, dtype=I32)[:, None] % heads
    colh = jnp.arange(page_rows, dtype=I32)[None, :] % heads
    head_mask = jnp.where(rowh == colh, 0.0, -jnp.inf).astype(F32)

    def kmap(i):
        return lambda b, t, pt: (layer, pt[b * n_pages + jnp.minimum(t, k_steps - 1) * npg + i], 0, 0, 0)

    def vmap_(i):
        def index(b, t, pt):
            in_v = t >= k_steps
            seq_i = jnp.where(in_v, b, jnp.maximum(b - 1, 0))
            step = jnp.where(in_v, t - k_steps, k_steps - 1)
            return (layer, pt[seq_i * n_pages + step * npg + i], 0, 0, 0)
        return index

    seq_spec = pl.BlockSpec((None, nq, dh), lambda b, t, pt: (b, 0, 0))
    page_block = (None, None, page, heads, dh)
    grid_spec = pltpu.PrefetchScalarGridSpec(
        num_scalar_prefetch=1,
        grid=(nseq, 2 * k_steps),
        in_specs=[seq_spec, seq_spec, seq_spec,
                  pl.BlockSpec((nq, page_rows), lambda b, t, pt: (0, 0))]
                 + [pl.BlockSpec(page_block, kmap(i)) for i in range(npg)]
                 + [pl.BlockSpec(page_block, vmap_(i)) for i in range(npg)],
        out_specs=seq_spec,
        scratch_shapes=[
            pltpu.VMEM((n_pages, nq, page_rows), F32),
            pltpu.VMEM((n_pages, nq, page_rows), BF16),
            pltpu.VMEM((n_pages * page // MOBA_BLOCK, heads, dh), F32),
            pltpu.VMEM((nq, dh), F32),
            pltpu.VMEM((nq, dh), F32),
        ],
    )
    return pl.pallas_call(
        _attn_sample_body,
        grid_spec=grid_spec,
        out_shape=jax.ShapeDtypeStruct((nseq, nq, dh), F32),
        compiler_params=_cparams(("arbitrary", "arbitrary")),
        name="moba_sample",
    )(page_table.reshape(-1), q_rows, k_new, v_new, head_mask,
      *([cache_k] * npg), *([cache_v] * npg))


def _post_body(alpha, prologue, n_pro, *refs):
    pro_refs = refs[:n_pro]
    (x_ref, w_ref, gt_ref, lng_ref, lnb_ref, sh_ref, sc_ref, wrh_ref, wrl_ref, br_ref,
     x1_ref, h2c_ref, lg_ref) = refs[n_pro:]
    a = prologue(*pro_refs)
    y = jnp.dot(a, w_ref[...], preferred_element_type=F32)
    z = alpha * x_ref[...] + (1.0 + gt_ref[...]) * y
    x1 = _layer_norm(z, lng_ref[...], lnb_ref[...])
    x1_ref[...] = x1
    h2 = x1 * (1.0 + sc_ref[...]) + sh_ref[...]
    tm, d = h2.shape
    nchunk = d // LANES
    for c in range(nchunk):
        h2c_ref[pl.ds(c, tm, stride=nchunk), :] = h2[:, c * LANES:(c + 1) * LANES]
    hi = h2.astype(BF16)
    lo = (h2 - hi.astype(F32)).astype(BF16)
    wh, wl = wrh_ref[...], wrl_ref[...]
    lg = (lax.dot_general(wh, hi, NT_DIMS, preferred_element_type=F32)
          + lax.dot_general(wl, hi, NT_DIMS, preferred_element_type=F32)
          + lax.dot_general(wh, lo, NT_DIMS, preferred_element_type=F32))
    lg_ref[...] = lg + br_ref[...]


def _pro_identity(a_ref):
    return a_ref[...]


def _pro_gmlp_prompt(u_ref, v_ref, ws_ref, bexp_ref):
    tm, d = u_ref.shape
    gw = d // GMLP_GROUPS
    row = lax.broadcasted_iota(I32, (GMLP_CHUNK, GMLP_CHUNK), 0)
    col = lax.broadcasted_iota(I32, (GMLP_CHUNK, GMLP_CHUNK), 1)
    tri = row >= col
    wms = [jnp.where(tri, ws_ref[g], 0.0).astype(BF16) for g in range(GMLP_GROUPS)]
    outs = []
    for ch in range(tm // GMLP_CHUNK):
        lo, hi = ch * GMLP_CHUNK, (ch + 1) * GMLP_CHUNK
        cols = [jnp.dot(wms[g], v_ref[lo:hi, g * gw:(g + 1) * gw], preferred_element_type=F32)
                for g in range(GMLP_GROUPS)]
        mixed = jnp.concatenate(cols, axis=1) + bexp_ref[...]
        outs.append((u_ref[lo:hi, :].astype(F32) * mixed).astype(BF16))
    return jnp.concatenate(outs, axis=0)


def _pro_gmlp_sample(u_ref, v_ref, wexp_ref, bexp_ref):
    s = pl.program_id(0)
    tm = u_ref.shape[0]
    n_pos = v_ref.shape[0] // tm
    mixed = bexp_ref[pl.ds(s, 1), :]
    for j in range(n_pos):
        w = wexp_ref[pl.ds(s * n_pos + j, 1), :] * (j <= s).astype(F32)
        mixed = mixed + w * v_ref[j * tm:(j + 1) * tm, :]
    return (u_ref[...].astype(F32) * mixed).astype(BF16)


def _post(alpha, prologue, pro_args, pro_specs, x, w_bf16, mod, ln_g, ln_b, wr_hi, wr_lo, br, tm):
    rows, d = x.shape
    ne = wr_hi.shape[0]
    nchunk = d // LANES
    vec = pl.BlockSpec((1, d), lambda i: (0, 0))
    return pl.pallas_call(
        functools.partial(_post_body, alpha, prologue, len(pro_args)),
        grid=(rows // tm,),
        in_specs=list(pro_specs) + [
            pl.BlockSpec((tm, d), lambda i: (i, 0)),
            pl.BlockSpec(w_bf16.shape, lambda i: (0, 0)),
            mod.spec(2, 1), vec, vec, mod.spec(3, 1), mod.spec(4, 1),
            pl.BlockSpec((ne, d), lambda i: (0, 0)),
            pl.BlockSpec((ne, d), lambda i: (0, 0)),
            pl.BlockSpec((ne, 1), lambda i: (0, 0)),
        ],
        out_specs=[
            pl.BlockSpec((tm, d), lambda i: (i, 0)),
            pl.BlockSpec((tm * nchunk, LANES), lambda i: (i, 0)),
            pl.BlockSpec((ne, tm), lambda i: (0, i)),
        ],
        out_shape=[
            jax.ShapeDtypeStruct((rows, d), F32),
            jax.ShapeDtypeStruct((rows * nchunk, LANES), F32),
            jax.ShapeDtypeStruct((ne, rows), F32),
        ],
        compiler_params=_cparams(("arbitrary",)),
        name="proj_postnorm_router",
    )(*pro_args, x, w_bf16, mod.arr, ln_g, ln_b, mod.arr, mod.arr, wr_hi, wr_lo, br)


def _route_body(lg_ref, eid_ref, gate_ref, dest_ref, meta_ref):
    ne = MOE_GROUPS * MOE_EXPERTS_PER_GROUP
    epg = MOE_EXPERTS_PER_GROUP
    lgt = lg_ref[...]
    t = lgt.shape[1]
    row8 = lax.broadcasted_iota(I32, (SUBLANES, t), 0)
    lgp = jnp.where(row8 < MOE_GROUPS, lgt[0:SUBLANES], -jnp.inf)
    mg = lgp.max(axis=0, keepdims=True)
    grp = jnp.where(lgp == mg, row8, SUBLANES).min(axis=0, keepdims=True)
    pg = 1.0 / jnp.exp(lgp - mg).sum(axis=0, keepdims=True)

    le = jnp.zeros((epg, t), F32)
    for g in range(MOE_GROUPS):
        le = jnp.where(grp == g, lgt[SUBLANES + g * epg:SUBLANES + (g + 1) * epg], le)
    m1 = le.max(axis=0, keepdims=True)
    i1 = jnp.where(le == m1, row8, epg).min(axis=0, keepdims=True)
    le2 = jnp.where(row8 == i1, -jnp.inf, le)
    m2 = le2.max(axis=0, keepdims=True)
    i2 = jnp.where(le2 == m2, row8, epg).min(axis=0, keepdims=True)
    e = jnp.exp(m2 - m1)
    g1 = pg / (1.0 + e)
    g2 = pg * e / (1.0 + e)
    e1 = grp * epg + i1
    e2 = grp * epg + i2
    eid_ref[0:1, :] = e1
    eid_ref[1:2, :] = e2
    gate_ref[0:1, :] = g1
    gate_ref[1:2, :] = g2

    rows = lax.broadcasted_iota(I32, (ne, t), 0)
    oh1 = (rows == e1).astype(F32)
    oh2 = (rows == e2).astype(F32)
    oh = (oh1 + oh2).astype(BF16)
    cw = 512
    ur = lax.broadcasted_iota(I32, (cw, cw), 0)
    uc = lax.broadcasted_iota(I32, (cw, cw), 1)
    upper = (ur < uc).astype(BF16)
    carry = jnp.zeros((ne, 1), F32)
    pref = []
    for ci in range(t // cw):
        ohc = oh[:, ci * cw:(ci + 1) * cw]
        pref.append(jnp.dot(ohc, upper, preferred_element_type=F32) + carry)
        carry = carry + ohc.astype(F32).sum(axis=1, keepdims=True)
    cnt_before = jnp.concatenate(pref, axis=1)

    counts = carry
    shift = MOE_ROWS.bit_length() - 1
    padded = (((counts.astype(I32) + (MOE_ROWS - 1)) >> shift) << shift).astype(F32)
    lr = lax.broadcasted_iota(I32, (ne, ne), 0)
    lc = lax.broadcasted_iota(I32, (ne, ne), 1)
    lower = (lc < lr).astype(F32)
    pstart = jnp.dot(lower, jnp.broadcast_to(padded, (ne, LANES)), precision=lax.Precision.HIGHEST,
                     preferred_element_type=F32)[:, 0:1]
    pend = pstart + padded
    base = pstart + cnt_before
    d1 = (oh1 * base).sum(axis=0, keepdims=True)
    d2 = (oh2 * base).sum(axis=0, keepdims=True)
    dest_ref[0:1, :] = d1.astype(I32)
    dest_ref[1:2, :] = d2.astype(I32)

    mw = meta_ref.shape[1]
    blk_start = (lax.broadcasted_iota(I32, (ne, mw), 1) * MOE_ROWS).astype(F32)
    blk_e = jnp.minimum((pend <= blk_start).astype(F32).sum(axis=0, keepdims=True), ne - 1.0)
    last = lax.broadcasted_iota(I32, (ne, mw), 0) == ne - 1
    nused = jnp.where(last, jnp.broadcast_to(pend, (ne, mw)), 0.0).sum(axis=0, keepdims=True) * (1.0 / MOE_ROWS)
    lane_e = lax.broadcasted_iota(I32, (ne, mw), 1)
    diag = lane_e == lax.broadcasted_iota(I32, (ne, mw), 0)
    fill_lo = jnp.where(diag, jnp.broadcast_to(pstart + counts, (ne, mw)), 0.0).sum(axis=0, keepdims=True)
    fill_hi = jnp.where(diag, jnp.broadcast_to(pend, (ne, mw)), 0.0).sum(axis=0, keepdims=True)
    mrow = lax.broadcasted_iota(I32, (SUBLANES, mw), 0)
    meta = jnp.where(mrow == 0, blk_e, jnp.where(mrow == 1, nused, jnp.where(mrow == 2, fill_lo, fill_hi)))
    meta_ref[...] = meta.astype(I32)


def _route(logits_t, n_blocks_max):
    ne_pad, t = logits_t.shape
    mw = 256
    assert n_blocks_max <= mw
    return pl.pallas_call(
        _route_body,
        out_shape=[
            jax.ShapeDtypeStruct((2, t), I32),
            jax.ShapeDtypeStruct((2, t), F32),
            jax.ShapeDtypeStruct((2, t), I32),
            jax.ShapeDtypeStruct((SUBLANES, mw), I32),
        ],
        compiler_params=pltpu.CompilerParams(vmem_limit_bytes=VMEM_LIMIT),
        name="moe_route",
    )(logits_t)


def _dispatch_body(nchunk, n_prompt_tiles, fill_ref, dest_ref, hp_ref, hs_ref, xb_ref, zero_scr, sem):
    i = pl.program_id(0)
    tm = dest_ref.shape[1]
    n_ranges = fill_ref.shape[1]

    def slab(ref, row):
        return ref.at[pl.ds(pl.multiple_of(row * nchunk, nchunk), nchunk)]

    def issue(src_ref):
        def body(tok, carry):
            for k in range(2):
                pltpu.make_async_copy(slab(src_ref, tok), slab(xb_ref, dest_ref[k, tok]), sem).start()
            return carry
        lax.fori_loop(0, tm, body, 0)

    @pl.when(i < n_prompt_tiles)
    def _():
        issue(hp_ref)

    @pl.when(i >= n_prompt_tiles)
    def _():
        issue(hs_ref)

    def drain(_, carry):
        pltpu.make_async_copy(slab(hp_ref, 0), slab(xb_ref, 0), sem).wait()
        return carry
    lax.fori_loop(0, 2 * tm, drain, 0)

    @pl.when(i == 0)
    def _():
        zero_scr[...] = jnp.zeros_like(zero_scr)
        total = 0
        for e in range(n_ranges):
            lo, hi = fill_ref[0, e], fill_ref[1, e]

            def body(r, carry):
                pltpu.make_async_copy(zero_scr, slab(xb_ref, r), sem).start()
                return carry
            lax.fori_loop(lo, hi, body, 0)
            total = total + (hi - lo)

        def wbody(_, carry):
            pltpu.make_async_copy(zero_scr, slab(xb_ref, 0), sem).wait()
            return carry
        lax.fori_loop(0, total, wbody, 0)


def _dispatch(fill, dest_tiles, h2c_p, h2c_s, n_rows, nchunk):
    n_tiles, _, tm = dest_tiles.shape
    n_prompt_tiles = h2c_p.shape[0] // (nchunk * tm)
    grid_spec = pltpu.PrefetchScalarGridSpec(
        num_scalar_prefetch=1,
        grid=(n_tiles,),
        in_specs=[
            pl.BlockSpec((None, 2, tm), lambda i, f: (i, 0, 0), memory_space=pltpu.SMEM),
            pl.BlockSpec((tm * nchunk, LANES), lambda i, f: (jnp.minimum(i, n_prompt_tiles - 1), 0)),
            pl.BlockSpec((tm * nchunk, LANES), lambda i, f: (jnp.maximum(i - n_prompt_tiles, 0), 0)),
        ],
        out_specs=pl.BlockSpec(memory_space=pl.ANY),
        scratch_shapes=[pltpu.VMEM((nchunk, LANES), F32), pltpu.SemaphoreType.DMA(())],
    )
    return pl.pallas_call(
        functools.partial(_dispatch_body, nchunk, n_prompt_tiles),
        grid_spec=grid_spec,
        out_shape=jax.ShapeDtypeStruct((n_rows * nchunk, LANES), F32),
        compiler_params=_cparams(("arbitrary",)),
        name="moe_dispatch",
    )(fill, dest_tiles, h2c_p, h2c_s)


def _expert_body(blk_ref, nused_ref, x_ref, wg_ref, wu_ref, wd_ref, y_ref, wg_scr, wu_scr, wd_scr):
    i = pl.program_id(0)
    nchunk = x_ref.shape[0] // MOE_ROWS

    @pl.when(i < nused_ref[0])
    def _():
        prev = blk_ref[jnp.maximum(i - 1, 0)]

        @pl.when((i == 0) | (blk_ref[i] != prev))
        def _():
            wg_scr[...] = wg_ref[...].astype(BF16)
            wu_scr[...] = wu_ref[...].astype(BF16)
            wd_scr[...] = wd_ref[...].astype(BF16)

        x = jnp.concatenate([x_ref[pl.ds(c, MOE_ROWS, stride=nchunk), :] for c in range(nchunk)],
                            axis=1).astype(BF16)
        g = jnp.dot(x, wg_scr[...], preferred_element_type=F32)
        u = jnp.dot(x, wu_scr[...], preferred_element_type=F32)
        a = (jax.nn.silu(g) * u).astype(BF16)
        y = jnp.dot(a, wd_scr[...], preferred_element_type=F32)
        for c in range(nchunk):
            y_ref[pl.ds(c, MOE_ROWS, stride=nchunk), :] = y[:, c * LANES:(c + 1) * LANES]

    @pl.when(i >= nused_ref[0])
    def _():
        y_ref[...] = jnp.zeros_like(y_ref)


def _experts(blk_e, nused, xb, w_gate, w_up, w_down, layer, n_blocks):
    _, ne, d, de = w_gate.shape
    nchunk = d // LANES
    rows_c = MOE_ROWS * nchunk

    def blk(i, be, nu):
        return jnp.minimum(i, nu[0] - 1)

    grid_spec = pltpu.PrefetchScalarGridSpec(
        num_scalar_prefetch=2,
        grid=(n_blocks,),
        in_specs=[
            pl.BlockSpec((rows_c, LANES), lambda i, be, nu: (blk(i, be, nu), 0)),
            pl.BlockSpec((None, None, d, de), lambda i, be, nu: (layer, be[blk(i, be, nu)], 0, 0)),
            pl.BlockSpec((None, None, d, de), lambda i, be, nu: (layer, be[blk(i, be, nu)], 0, 0)),
            pl.BlockSpec((None, None, de, d), lambda i, be, nu: (layer, be[blk(i, be, nu)], 0, 0)),
        ],
        out_specs=pl.BlockSpec((rows_c, LANES), lambda i, be, nu: (i, 0)),
        scratch_shapes=[pltpu.VMEM((d, de), BF16), pltpu.VMEM((d, de), BF16), pltpu.VMEM((de, d), BF16)],
    )
    return pl.pallas_call(
        _expert_body,
        grid_spec=grid_spec,
        out_shape=jax.ShapeDtypeStruct(xb.shape, F32),
        compiler_params=_cparams(("arbitrary",)),
        name="moe_experts",
    )(blk_e, nused, xb, w_gate, w_up, w_down)


def _combine_body(alpha, dest_ref, x_ref, gts_ref, gt_ref, lng_ref, lnb_ref, yb_ref, o_ref, buf, sem):
    tm, d = x_ref.shape
    nchunk = d // LANES

    def slab(ref, row):
        return ref.at[pl.ds(pl.multiple_of(row * nchunk, nchunk), nchunk)]

    def body(tok, carry):
        for k in range(2):
            pltpu.make_async_copy(slab(yb_ref, dest_ref[k, tok]), slab(buf.at[k], tok), sem).start()
        return carry
    lax.fori_loop(0, tm, body, 0)

    def wbody(_, carry):
        pltpu.make_async_copy(slab(yb_ref, 0), slab(buf.at[0], 0), sem).wait()
        return carry
    lax.fori_loop(0, 2 * tm, wbody, 0)

    gts = gts_ref[...]
    f = jnp.zeros((tm, d), F32)
    for k in range(2):
        yk = jnp.concatenate([buf[k, pl.ds(c, tm, stride=nchunk), :] for c in range(nchunk)], axis=1)
        f = f + gts[:, k:k + 1] * yk
    z = alpha * x_ref[...] + (1.0 + gt_ref[...]) * f
    o_ref[...] = _layer_norm(z, lng_ref[...], lnb_ref[...])


def _combine(alpha, dest_tiles, x1, gates_t, mod, ln_g, ln_b, yb):
    rows, d = x1.shape
    n_tiles, _, tm = dest_tiles.shape
    nchunk = d // LANES
    vec = pl.BlockSpec((1, d), lambda i: (0, 0))
    if mod.tiles_per_seq is not None:
        mod = _Mod(mod.arr, d, mod.tiles_per_seq * (ROW_TILE // tm))
    return pl.pallas_call(
        functools.partial(_combine_body, alpha),
        grid=(n_tiles,),
        in_specs=[
            pl.BlockSpec((None, 2, tm), lambda i: (i, 0, 0), memory_space=pltpu.SMEM),
            pl.BlockSpec((tm, d), lambda i: (i, 0)),
            pl.BlockSpec((tm, 2), lambda i: (i, 0)),
            mod.spec(5, 1), vec, vec,
            pl.BlockSpec(memory_space=pl.ANY),
        ],
        out_specs=pl.BlockSpec((tm, d), lambda i: (i, 0)),
        out_shape=jax.ShapeDtypeStruct((rows, d), F32),
        scratch_shapes=[pltpu.VMEM((2, tm * nchunk, LANES), F32), pltpu.SemaphoreType.DMA(())],
        compiler_params=_cparams(("arbitrary",)),
        name="moe_combine_postnorm",
    )(dest_tiles, x1, gates_t, mod.arr, ln_g, ln_b, yb)


def _gmlp_in_body(x_ref, sh_ref, sc_ref, w_ref, vg_ref, vb_ref, u_ref, v_ref, h_scr, v_scr):
    j = pl.program_id(1)
    half = v_scr.shape[0]

    @pl.when(j == 0)
    def _():
        h_scr[...] = (x_ref[...] * (1.0 + sc_ref[...]) + sh_ref[...]).astype(BF16)

    y = jax.nn.gelu(jnp.dot(h_scr[...], w_ref[...], preferred_element_type=F32))

    @pl.when(j < half)
    def _():
        u_ref[...] = y.astype(u_ref.dtype)

    @pl.when(j >= half)
    def _():
        v_scr[j - half] = y

    @pl.when(j == 2 * half - 1)
    def _():
        v = jnp.concatenate([v_scr[c] for c in range(half)], axis=1)
        v_ref[...] = _layer_norm(v, vg_ref[...], vb_ref[...]).astype(v_ref.dtype)


def _gmlp_in(x, mod, w_in_bf16, vn_g, vn_b, tm, v_dtype):
    rows, d = x.shape
    tn = 512
    half = d // tn
    vec = pl.BlockSpec((1, d), lambda i, j: (0, 0))
    return pl.pallas_call(
        _gmlp_in_body,
        grid=(rows // tm, 2 * half),
        in_specs=[
            pl.BlockSpec((tm, d), lambda i, j: (i, 0)),
            mod.spec(0, 2), mod.spec(1, 2),
            pl.BlockSpec((d, tn), lambda i, j: (0, j)),
            vec, vec,
        ],
        out_specs=[
            pl.BlockSpec((tm, tn), lambda i, j: (i, jnp.minimum(j, half - 1))),
            pl.BlockSpec((tm, d), lambda i, j: (i, 0)),
        ],
        out_shape=[jax.ShapeDtypeStruct((rows, d), BF16), jax.ShapeDtypeStruct((rows, d), v_dtype)],
        scratch_shapes=[pltpu.VMEM((tm, d), BF16), pltpu.VMEM((half, tm, tn), F32)],
        compiler_params=_cparams(("arbitrary", "arbitrary")),
        name="gmlp_in",
    )(x, mod.arr, mod.arr, w_in_bf16, vn_g, vn_b)


def _moe_and_norm(alpha, x1_p, x1_s, h2c_p, h2c_s, lg_p, lg_s, mod_p, mod_s, ln_g, ln_b,
                  w_gate, w_up, w_down, layer):
    t_p, d = x1_p.shape
    t_s = x1_s.shape[0]
    t = t_p + t_s
    nchunk = d // LANES
    ne = w_gate.shape[1]
    n_blocks = -(-(2 * t) // MOE_ROWS) + ne
    n_rows = n_blocks * MOE_ROWS

    eid, gates, dest, meta = _route(jnp.concatenate([lg_p, lg_s], axis=1), n_blocks)
    blk_e = meta[0, :n_blocks]
    nused = meta[1, :1]
    tail = jnp.stack([nused * MOE_ROWS, jnp.full((1,), n_rows, I32)])
    fill = jnp.concatenate([meta[2:4, :ne], tail], axis=1)

    def tiles(a, tm):
        return a.reshape(2, -1, tm).transpose(1, 0, 2)

    xb = _dispatch(fill, tiles(dest, ROW_TILE), h2c_p, h2c_s, n_rows, nchunk)
    yb = _experts(blk_e, nused, xb, w_gate, w_up, w_down, layer, n_blocks)
    gates_t = gates.T
    tm = COMBINE_TILE
    x2_p = _combine(alpha, tiles(dest[:, :t_p], tm), x1_p, gates_t[:t_p], mod_p, ln_g, ln_b, yb)
    x2_s = _combine(alpha, tiles(dest[:, t_p:], min(tm, SAMPLE_TILE)), x1_s, gates_t[t_p:], mod_s, ln_g, ln_b, yb)
    return x2_p, x2_s


def _router_weights(w_rg, b_rg, w_re, b_re):
    d = w_rg.shape[0]
    pad = jnp.zeros((SUBLANES - MOE_GROUPS, d), F32)
    w = jnp.concatenate([w_rg.T, pad, w_re.T], axis=0)
    b = jnp.concatenate([b_rg, jnp.zeros((SUBLANES - MOE_GROUPS,), F32), b_re])[:, None]
    hi = w.astype(BF16)
    lo = (w - hi.astype(F32)).astype(BF16)
    return hi, lo, b


def kernel(x_prompt, x_sample, cache_k, cache_v, page_table, c_prompt, c_sample, w_ada, b_ada, ln_g, ln_b,
           attn_w_qkv, attn_w_o, gmlp_w_in, gmlp_vn_g, gmlp_vn_b, gmlp_w_s, gmlp_b_s, gmlp_w_out,
           moe_w_rg, moe_b_rg, moe_w_re, moe_b_re, moe_w_gate, moe_w_up, moe_w_down):
    batch, seq, d = x_prompt.shape
    nseq, dec_seq, _ = x_sample.shape
    depth = w_ada.shape[0]
    dh = d // N_HEADS
    past_len = page_table.shape[1] * cache_k.shape[2]
    alpha = (2.0 * depth) ** 0.25
    tiles_per_seq = seq // ROW_TILE

    xp = x_prompt.reshape(batch * seq, d)
    xs = x_sample.transpose(1, 0, 2).reshape(dec_seq * nseq, d)

    pad_rows = (-(batch + nseq)) % SUBLANES
    c_all = jnp.concatenate([c_prompt, c_sample, jnp.zeros((pad_rows, d), F32)], axis=0)
    m_all = _adaln(c_all, w_ada, b_ada)

    tabs_p = _rope_tables(jnp.arange(seq, dtype=I32), dh)
    tabs_s = _rope_tables(past_len + jnp.repeat(jnp.arange(dec_seq, dtype=I32), nseq), dh)

    new_kp, new_vp, new_ks, new_vs, new_gv = [], [], [], [], []
    for l in range(depth):
        mod_p = _Mod(m_all[l, :batch].reshape(batch, 1, 6 * d), d, tiles_per_seq)
        mod_in = _Mod(mod_p.arr, d, seq // IN_TILE)
        mod_s = _Mod(m_all[l, batch:batch + nseq], d, None)
        lng0, lnb0 = ln_g[l, 0][None, :], ln_b[l, 0][None, :]
        lng1, lnb1 = ln_g[l, 1][None, :], ln_b[l, 1][None, :]
        wr_hi, wr_lo, br = _router_weights(moe_w_rg[l], moe_b_rg[l], moe_w_re[l], moe_b_re[l])
        post = functools.partial(_post, alpha)

        if l % 2 == 0:
            la = l // 2
            w_qkv = attn_w_qkv[la].astype(BF16)
            w_o = attn_w_o[la].astype(BF16)
            q_p, k_p, v_p = _qkv(xp, mod_in, w_qkv, tabs_p, lambda i: i % (seq // IN_TILE), IN_TILE, BF16)
            q_s, k_s, v_s = _qkv(xs, mod_s, w_qkv, tabs_s, lambda i: i, SAMPLE_TILE, F32)
            o_p = _attn_prompt(q_p, k_p, v_p, batch, seq)

            def seq_rows(a):
                return a.reshape(dec_seq, nseq, N_HEADS, dh).transpose(1, 0, 2, 3).reshape(nseq, dec_seq * N_HEADS, dh)

            o_s = _attn_sample(seq_rows(q_s), seq_rows(k_s), seq_rows(v_s), cache_k, cache_v, la, page_table)
            o_s = o_s.reshape(nseq, dec_seq, d).transpose(1, 0, 2).reshape(dec_seq * nseq, d).astype(BF16)
            new_kp.append(k_p.reshape(batch, seq, N_HEADS, dh))
            new_vp.append(v_p.reshape(batch, seq, N_HEADS, dh))
            new_ks.append(k_s.reshape(dec_seq, nseq, N_HEADS, dh).transpose(1, 0, 2, 3))
            new_vs.append(v_s.reshape(dec_seq, nseq, N_HEADS, dh).transpose(1, 0, 2, 3))
            x1_p, h2c_p, lg_p = post(_pro_identity, (o_p,), [pl.BlockSpec((ROW_TILE, d), lambda i: (i, 0))],
                                     xp, w_o, mod_p, lng0, lnb0, wr_hi, wr_lo, br, ROW_TILE)
            x1_s, h2c_s, lg_s = post(_pro_identity, (o_s,), [pl.BlockSpec((SAMPLE_TILE, d), lambda i: (i, 0))],
                                     xs, w_o, mod_s, lng0, lnb0, wr_hi, wr_lo, br, SAMPLE_TILE)
        else:
            lb = l // 2
            w_in = gmlp_w_in[lb].astype(BF16)
            w_out = gmlp_w_out[lb].astype(BF16)
            vg, vb = gmlp_vn_g[lb][None, :], gmlp_vn_b[lb][None, :]
            u_p, vn_p = _gmlp_in(xp, mod_in, w_in, vg, vb, IN_TILE, BF16)
            u_s, vn_s = _gmlp_in(xs, mod_s, w_in, vg, vb, SAMPLE_TILE, F32)
            new_gv.append(vn_s.reshape(dec_seq, nseq, d).transpose(1, 0, 2))
            gw = d // GMLP_GROUPS
            bexp = jnp.repeat(gmlp_b_s[lb].T, gw, axis=1)
            wexp = jnp.repeat(gmlp_w_s[lb][:, :dec_seq, :dec_seq].transpose(1, 2, 0).reshape(dec_seq * dec_seq, -1),
                              gw, axis=1)
            tile = pl.BlockSpec((ROW_TILE, d), lambda i: (i, 0))
            x1_p, h2c_p, lg_p = post(
                _pro_gmlp_prompt, (u_p, vn_p, gmlp_w_s[lb], bexp),
                [tile, tile, pl.BlockSpec(gmlp_w_s[lb].shape, lambda i: (0, 0, 0)),
                 pl.BlockSpec(bexp.shape, lambda i: (0, 0))],
                xp, w_out, mod_p, lng0, lnb0, wr_hi, wr_lo, br, ROW_TILE)
            x1_s, h2c_s, lg_s = post(
                _pro_gmlp_sample, (u_s, vn_s, wexp, bexp[:SUBLANES]),
                [pl.BlockSpec((SAMPLE_TILE, d), lambda i: (i, 0)),
                 pl.BlockSpec(vn_s.shape, lambda i: (0, 0)),
                 pl.BlockSpec(wexp.shape, lambda i: (0, 0)),
                 pl.BlockSpec((SUBLANES, d), lambda i: (0, 0))],
                xs, w_out, mod_s, lng0, lnb0, wr_hi, wr_lo, br, SAMPLE_TILE)

        xp, xs = _moe_and_norm(alpha, x1_p, x1_s, h2c_p, h2c_s, lg_p, lg_s, mod_p, mod_s, lng1, lnb1,
                               moe_w_gate, moe_w_up, moe_w_down, l)

    y_p = xp.reshape(batch, seq, d)
    y_s = xs.reshape(dec_seq, nseq, d).transpose(1, 0, 2)
    return (y_p, y_s, jnp.stack(new_kp), jnp.stack(new_vp), jnp.stack(new_ks), jnp.stack(new_vs),
            jnp.stack(new_gv))
```

```python
import functools

import jax
import jax.numpy as jnp
from jax import lax
from jax.experimental import pallas as pl
from jax.experimental.pallas import tpu as pltpu

F32, BF16, I32 = jnp.float32, jnp.bfloat16, jnp.int32

N_HEADS = 16
MOBA_BLOCK = 256
MOBA_TOPK = 3
ROPE_THETA = 500000.0
GMLP_CHUNK = 128
GMLP_GROUPS = 8
MOE_GROUPS = 4
MOE_EXPERTS_PER_GROUP = 8
LN_EPS = 1e-5

LANES = 128
SUBLANES = 8
VMEM_LIMIT = 56 * 1024 * 1024

ROW_TILE = 512
IN_TILE = 1024
SAMPLE_TILE = 128
MOE_ROWS = 128
COMBINE_TILE = 256
PAGES_PER_STEP = 8

NT_DIMS = (((1,), (1,)), ((), ()))


def _cparams(sem):
    return pltpu.CompilerParams(dimension_semantics=sem, vmem_limit_bytes=VMEM_LIMIT)


def _layer_norm(z, g, b):
    mu = jnp.mean(z, axis=-1, keepdims=True)
    zc = z - mu
    var = jnp.mean(zc * zc, axis=-1, keepdims=True)
    return zc * lax.rsqrt(var + LN_EPS) * g + b


def _adaln_body(c_ref, w_ref, b_ref, o_ref):
    a = jax.nn.silu(c_ref[...]).astype(BF16)
    o_ref[...] = jnp.dot(a, w_ref[...].astype(BF16), preferred_element_type=F32) + b_ref[...]


def _adaln(c_all, w_ada, b_ada):
    depth, d, n = w_ada.shape
    rows = c_all.shape[0]
    tn = 1024
    return pl.pallas_call(
        _adaln_body,
        grid=(depth, n // tn),
        in_specs=[
            pl.BlockSpec((rows, d), lambda l, j: (0, 0)),
            pl.BlockSpec((None, d, tn), lambda l, j: (l, 0, j)),
            pl.BlockSpec((None, 1, tn), lambda l, j: (l, 0, j)),
        ],
        out_specs=pl.BlockSpec((None, rows, tn), lambda l, j: (l, 0, j)),
        out_shape=jax.ShapeDtypeStruct((depth, rows, n), F32),
        compiler_params=_cparams(("arbitrary", "arbitrary")),
        name="adaln",
    )(c_all, w_ada, b_ada.reshape(depth, 1, n))


class _Mod:
    def __init__(self, arr, d, tiles_per_seq):
        self.arr = arr
        self.d = d
        self.tiles_per_seq = tiles_per_seq

    def spec(self, k, ngrid):
        d, tps = self.d, self.tiles_per_seq
        if tps is None:
            rows = self.arr.shape[0]
            if ngrid == 1:
                return pl.BlockSpec((rows, d), lambda i: (0, k))
            if ngrid == 2:
                return pl.BlockSpec((rows, d), lambda i, j: (0, k))
            return pl.BlockSpec((rows, d), lambda i, p, j: (0, k))
        if ngrid == 1:
            return pl.BlockSpec((None, 1, d), lambda i: (i // tps, 0, k))
        if ngrid == 2:
            return pl.BlockSpec((None, 1, d), lambda i, j: (i // tps, 0, k))
        return pl.BlockSpec((None, 1, d), lambda i, p, j: (i // tps, 0, k))


def _qkv_body(x_ref, sh_ref, sc_ref, w_ref, c_ref, s1_ref, s2_ref, q_ref, k_ref, v_ref, h_scr):
    p = pl.program_id(1)
    j = pl.program_id(2)

    @pl.when((p == 0) & (j == 0))
    def _():
        h_scr[...] = (x_ref[...] * (1.0 + sc_ref[...]) + sh_ref[...]).astype(BF16)

    y = jnp.dot(h_scr[...], w_ref[...], preferred_element_type=F32)
    tn = y.shape[1]

    def rope(y):
        reps = tn // LANES
        c = jnp.tile(c_ref[...], (1, reps))
        s1 = jnp.tile(s1_ref[...], (1, reps))
        s2 = jnp.tile(s2_ref[...], (1, reps))
        half = c_ref.shape[1] // 8
        return y * c + pltpu.roll(y, tn - half, 1) * s1 + pltpu.roll(y, half, 1) * s2

    @pl.when(p == 0)
    def _():
        q_ref[...] = rope(y).astype(q_ref.dtype)

    @pl.when(p == 1)
    def _():
        k_ref[...] = rope(y)

    @pl.when(p == 2)
    def _():
        v_ref[...] = y


def _qkv(x, mod, w_qkv_bf16, tabs, tab_index, tm, q_dtype):
    rows, d = x.shape
    tn = 512
    ncol = d // tn
    c_tab, s1_tab, s2_tab = tabs
    tab_spec = pl.BlockSpec((tm, LANES), lambda i, p, j: (tab_index(i), 0))
    out_q = pl.BlockSpec((tm, tn), lambda i, p, j: (i, jnp.where(p == 0, j, ncol - 1)))
    out_k = pl.BlockSpec((tm, tn), lambda i, p, j: (i, jnp.where(p < 1, 0, jnp.where(p == 1, j, ncol - 1))))
    out_v = pl.BlockSpec((tm, tn), lambda i, p, j: (i, jnp.where(p < 2, 0, j)))
    return pl.pallas_call(
        _qkv_body,
        grid=(rows // tm, 3, ncol),
        in_specs=[
            pl.BlockSpec((tm, d), lambda i, p, j: (i, 0)),
            mod.spec(0, 3),
            mod.spec(1, 3),
            pl.BlockSpec((d, tn), lambda i, p, j: (0, p * ncol + j)),
            tab_spec, tab_spec, tab_spec,
        ],
        out_specs=[out_q, out_k, out_v],
        out_shape=[
            jax.ShapeDtypeStruct((rows, d), q_dtype),
            jax.ShapeDtypeStruct((rows, d), F32),
            jax.ShapeDtypeStruct((rows, d), F32),
        ],
        scratch_shapes=[pltpu.VMEM((tm, d), BF16)],
        compiler_params=_cparams(("arbitrary", "arbitrary", "arbitrary")),
        name="qkv_rope",
    )(x, mod.arr, mod.arr, w_qkv_bf16, c_tab, s1_tab, s2_tab)


def _rope_tables(pos, head_dim):
    rot = head_dim // 4
    half = rot // 2
    inv = ROPE_THETA ** (-jnp.arange(half, dtype=F32) * 2.0 / rot)
    ang = pos.astype(F32)[:, None] * inv[None, :]
    cos, sin = jnp.cos(ang), jnp.sin(ang)
    n = pos.shape[0]
    ones = jnp.ones((n, head_dim - rot), F32)
    zeros = jnp.zeros((n, head_dim - rot), F32)
    zh = jnp.zeros((n, half), F32)
    c = jnp.concatenate([cos, cos, ones], axis=1)
    s1 = jnp.concatenate([-sin, zh, zeros], axis=1)
    s2 = jnp.concatenate([zh, sin, zeros], axis=1)
    return c, s1, s2


def _attn_prompt_body(q_ref, k_ref, v_ref, o_ref, kb_scr, vb_scr, sel_scr):
    seq, dh = q_ref.shape
    nblk = seq // MOBA_BLOCK
    scale = dh ** -0.5
    kf = k_ref[...]
    kb_scr[...] = kf.astype(BF16)
    vb_scr[:, :dh] = v_ref[...].astype(BF16)
    vb_scr[:, dh:] = jnp.ones((seq, dh), BF16)

    km = kf.reshape(nblk, MOBA_BLOCK, dh).sum(axis=1) * (1.0 / MOBA_BLOCK)
    g = lax.dot_general(km, q_ref[...].astype(F32), NT_DIMS, precision=lax.Precision.HIGHEST,
                        preferred_element_type=F32)
    blk = lax.broadcasted_iota(I32, (nblk, seq), 0)
    cur = lax.broadcasted_iota(I32, (nblk, seq), 1) // MOBA_BLOCK
    past = blk < cur
    g = jnp.where(past, g, -jnp.inf)
    cnt = jnp.zeros((nblk, seq), F32)
    for k in range(nblk):
        gk = g[k:k + 1, :]
        beats = (gk > g) | ((gk == g) & (k < blk))
        cnt = cnt + beats.astype(F32)
    sel = ((cnt < MOBA_TOPK) & past).astype(F32)
    sel_scr[...] = jnp.concatenate([sel, jnp.zeros((LANES - nblk, seq), F32)], axis=0).T

    row = lax.broadcasted_iota(I32, (MOBA_BLOCK, MOBA_BLOCK), 0)
    col = lax.broadcasted_iota(I32, (MOBA_BLOCK, MOBA_BLOCK), 1)
    causal = row >= col

    for c in range(nblk):
        lo, hi = c * MOBA_BLOCK, (c + 1) * MOBA_BLOCK
        qc = q_ref[lo:hi, :]
        selc = sel_scr[lo:hi, :]
        slabs = []
        for j in range(c + 1):
            s = lax.dot_general(qc, kb_scr[j * MOBA_BLOCK:(j + 1) * MOBA_BLOCK, :], NT_DIMS,
                                preferred_element_type=F32) * scale
            if j < c:
                s = jnp.where(selc[:, j:j + 1] > 0.5, s, -jnp.inf)
            else:
                s = jnp.where(causal, s, -jnp.inf)
            slabs.append(s)
        mx = slabs[0]
        for s in slabs[1:]:
            mx = jnp.maximum(mx, s)
        m = mx.max(axis=1, keepdims=True)
        acc = jnp.zeros((MOBA_BLOCK, 2 * dh), F32)
        for j, s in enumerate(slabs):
            p = jnp.exp(s - m)
            acc = acc + jnp.dot(p.astype(BF16), vb_scr[j * MOBA_BLOCK:(j + 1) * MOBA_BLOCK, :],
                                preferred_element_type=F32)
        o_ref[lo:hi, :] = (acc[:, :dh] / acc[:, dh:]).astype(o_ref.dtype)


def _attn_prompt(q, k, v, batch, seq):
    rows, d = q.shape
    dh = d // N_HEADS
    spec = pl.BlockSpec((seq, dh), lambda b, h: (b, h))
    return pl.pallas_call(
        _attn_prompt_body,
        grid=(batch, N_HEADS),
        in_specs=[spec, spec, spec],
        out_specs=spec,
        out_shape=jax.ShapeDtypeStruct((rows, d), BF16),
        scratch_shapes=[pltpu.VMEM((seq, dh), BF16), pltpu.VMEM((seq, 2 * dh), BF16), pltpu.VMEM((seq, LANES), F32)],
        compiler_params=_cparams(("arbitrary", "arbitrary")),
        name="moba_prompt",
    )(q, k, v)


def _attn_sample_body(pt_ref, q_ref, kn_ref, vn_ref, hm_ref, *rest):
    npg = PAGES_PER_STEP
    k_refs = rest[:npg]
    v_refs = rest[npg:2 * npg]
    o_ref = rest[2 * npg]
    s_scr, p_scr, ksum_scr, acc_scr, linv_scr = rest[2 * npg + 1:]

    t = pl.program_id(1)
    n_pages = s_scr.shape[0]
    k_steps = n_pages // npg
    nq, dh = q_ref.shape
    page_rows = s_scr.shape[2]
    page = page_rows // N_HEADS
    pages_per_blk = MOBA_BLOCK // page
    nblk = n_pages // pages_per_blk
    scale = dh ** -0.5
    qf = q_ref[...]
    qb = qf.astype(BF16)

    @pl.when(t < k_steps)
    def _():
        sums = []
        for i in range(npg):
            kp = k_refs[i][...]
            sums.append(jnp.sum(kp, axis=0))
            kb = kp.reshape(page_rows, dh).astype(BF16)
            s_scr[t * npg + i] = lax.dot_general(qb, kb, NT_DIMS, preferred_element_type=F32)
        for b in range(npg // pages_per_blk):
            tot = sums[b * pages_per_blk]
            for i in range(1, pages_per_blk):
                tot = tot + sums[b * pages_per_blk + i]
            ksum_scr[t * (npg // pages_per_blk) + b] = tot

    @pl.when(t == k_steps - 1)
    def _():
        km = ksum_scr[...].reshape(nblk * N_HEADS, dh) * (1.0 / MOBA_BLOCK)
        g = lax.dot_general(qf, km, NT_DIMS, precision=lax.Precision.HIGHEST, preferred_element_type=F32)
        width = nblk * N_HEADS
        rowh = lax.broadcasted_iota(I32, (nq, width), 0) % N_HEADS
        lane = lax.broadcasted_iota(I32, (nq, width), 1)
        own = (lane % N_HEADS) == rowh
        blk = lane // N_HEADS
        g = jnp.where(own, g, -jnp.inf)
        cnt = jnp.zeros((nq, width), F32)
        for dlt in range(1, nblk):
            lower = pltpu.roll(g, dlt * N_HEADS, 1)
            cnt = cnt + ((blk >= dlt) & (lower >= g)).astype(F32)
            upper = pltpu.roll(g, width - dlt * N_HEADS, 1)
            cnt = cnt + ((blk + dlt < nblk) & (upper > g)).astype(F32)
        sel = jnp.where(own & (cnt < MOBA_TOPK), 1.0, 0.0)
        hm = hm_ref[...]

        kn = kn_ref[...]
        sn = lax.dot_general(qb, kn.astype(BF16), NT_DIMS, preferred_element_type=F32) * scale
        r = lax.broadcasted_iota(I32, (nq, nq), 0)
        cidx = lax.broadcasted_iota(I32, (nq, nq), 1)
        ok = ((r % N_HEADS) == (cidx % N_HEADS)) & ((cidx // N_HEADS) <= (r // N_HEADS))
        sn = jnp.where(ok, sn, -jnp.inf)
        m = sn.max(axis=1, keepdims=True)

        biases = []
        for j in range(nblk):
            picked = sel[:, j * N_HEADS:(j + 1) * N_HEADS].max(axis=1, keepdims=True)
            biases.append(jnp.where(picked > 0.5, 0.0, -jnp.inf))
        for pg in range(n_pages):
            s = s_scr[pg] * scale + hm + biases[pg // pages_per_blk]
            s_scr[pg] = s
            m = jnp.maximum(m, s.max(axis=1, keepdims=True))
        en = jnp.exp(sn - m)
        l = en.sum(axis=1, keepdims=True)
        for pg in range(n_pages):
            e = jnp.exp(s_scr[pg] - m)
            l = l + e.sum(axis=1, keepdims=True)
            p_scr[pg] = e.astype(BF16)
        linv_scr[...] = jnp.broadcast_to(1.0 / l, linv_scr.shape)
        acc_scr[...] = jnp.dot(en.astype(BF16), vn_ref[...].astype(BF16), preferred_element_type=F32)

    @pl.when(t >= k_steps)
    def _():
        acc = acc_scr[...]
        for i in range(npg):
            vb = v_refs[i][...].reshape(page_rows, dh).astype(BF16)
            acc = acc + jnp.dot(p_scr[(t - k_steps) * npg + i], vb, preferred_element_type=F32)
        acc_scr[...] = acc

    @pl.when(t == 2 * k_steps - 1)
    def _():
        o_ref[...] = acc_scr[...] * linv_scr[...]


def _attn_sample(q_rows, k_new, v_new, cache_k, cache_v, layer, page_table):
    nseq, nq, dh = q_rows.shape
    _, n_phys, page, heads, _ = cache_k.shape
    n_pages = page_table.shape[1]
    npg = PAGES_PER_STEP
    k_steps = n_pages // npg
    page_rows = page * heads
    rowh = jnp.arange(nq, dtype=I32)[:, None] % heads
    colh = jnp.arange(page_rows, dtype=I32)[None, :] % heads
    head_mask = jnp.where(rowh == colh, 0.0, -jnp.inf).astype(F32)

    def kmap(i):
        return lambda b, t, pt: (layer, pt[b * n_pages + jnp.minimum(t, k_steps - 1) * npg + i], 0, 0, 0)

    def vmap_(i):
        def index(b, t, pt):
            in_v = t >= k_steps
            seq_i = jnp.where(in_v, b, jnp.maximum(b - 1, 0))
            step = jnp.where(in_v, t - k_steps, k_steps - 1)
            return (layer, pt[seq_i * n_pages + step * npg + i], 0, 0, 0)
        return index

    seq_spec = pl.BlockSpec((None, nq, dh), lambda b, t, pt: (b, 0, 0))
    page_block = (None, None, page, heads, dh)
    grid_spec = pltpu.PrefetchScalarGridSpec(
        num_scalar_prefetch=1,
        grid=(nseq, 2 * k_steps),
        in_specs=[seq_spec, seq_spec, seq_spec,
                  pl.BlockSpec((nq, page_rows), lambda b, t, pt: (0, 0))]
                 + [pl.BlockSpec(page_block, kmap(i)) for i in range(npg)]
                 + [pl.BlockSpec(page_block, vmap_(i)) for i in range(npg)],
        out_specs=seq_spec,
        scratch_shapes=[
            pltpu.VMEM((n_pages, nq, page_rows), F32),
            pltpu.VMEM((n_pages, nq, page_rows), BF16),
            pltpu.VMEM((n_pages * page // MOBA_BLOCK, heads, dh), F32),
            pltpu.VMEM((nq, dh), F32),
            pltpu.VMEM((nq, dh), F32),
        ],
    )
    return pl.pallas_call(
        _attn_sample_body,
        grid_spec=grid_spec,
        out_shape=jax.ShapeDtypeStruct((nseq, nq, dh), F32),
        compiler_params=_cparams(("arbitrary", "arbitrary")),
        name="moba_sample",
    )(page_table.reshape(-1), q_rows, k_new, v_new, head_mask,
      *([cache_k] * npg), *([cache_v] * npg))


def _post_body(alpha, prologue, n_pro, *refs):
    pro_refs = refs[:n_pro]
    (x_ref, w_ref, gt_ref, lng_ref, lnb_ref, sh_ref, sc_ref, wrh_ref, wrl_ref, br_ref,
     x1_ref, h2c_ref, lg_ref) = refs[n_pro:]
    a = prologue(*pro_refs)
    y = jnp.dot(a, w_ref[...], preferred_element_type=F32)
    z = alpha * x_ref[...] + (1.0 + gt_ref[...]) * y
    x1 = _layer_norm(z, lng_ref[...], lnb_ref[...])
    x1_ref[...] = x1
    h2 = x1 * (1.0 + sc_ref[...]) + sh_ref[...]
    tm, d = h2.shape
    nchunk = d // LANES
    for c in range(nchunk):
        h2c_ref[pl.ds(c, tm, stride=nchunk), :] = h2[:, c * LANES:(c + 1) * LANES]
    hi = h2.astype(BF16)
    lo = (h2 - hi.astype(F32)).astype(BF16)
    wh, wl = wrh_ref[...], wrl_ref[...]
    lg = (lax.dot_general(wh, hi, NT_DIMS, preferred_element_type=F32)
          + lax.dot_general(wl, hi, NT_DIMS, preferred_element_type=F32)
          + lax.dot_general(wh, lo, NT_DIMS, preferred_element_type=F32))
    lg_ref[...] = lg + br_ref[...]


def _pro_identity(a_ref):
    return a_ref[...]


def _pro_gmlp_prompt(u_ref, v_ref, ws_ref, bexp_ref):
    tm, d = u_ref.shape
    gw = d // GMLP_GROUPS
    row = lax.broadcasted_iota(I32, (GMLP_CHUNK, GMLP_CHUNK), 0)
    col = lax.broadcasted_iota(I32, (GMLP_CHUNK, GMLP_CHUNK), 1)
    tri = row >= col
    wms = [jnp.where(tri, ws_ref[g], 0.0).astype(BF16) for g in range(GMLP_GROUPS)]
    outs = []
    for ch in range(tm // GMLP_CHUNK):
        lo, hi = ch * GMLP_CHUNK, (ch + 1) * GMLP_CHUNK
        cols = [jnp.dot(wms[g], v_ref[lo:hi, g * gw:(g + 1) * gw], preferred_element_type=F32)
                for g in range(GMLP_GROUPS)]
        mixed = jnp.concatenate(cols, axis=1) + bexp_ref[...]
        outs.append((u_ref[lo:hi, :].astype(F32) * mixed).astype(BF16))
    return jnp.concatenate(outs, axis=0)


def _pro_gmlp_sample(u_ref, v_ref, wexp_ref, bexp_ref):
    s = pl.program_id(0)
    tm = u_ref.shape[0]
    n_pos = v_ref.shape[0] // tm
    mixed = bexp_ref[pl.ds(s, 1), :]
    for j in range(n_pos):
        w = wexp_ref[pl.ds(s * n_pos + j, 1), :] * (j <= s).astype(F32)
        mixed = mixed + w * v_ref[j * tm:(j + 1) * tm, :]
    return (u_ref[...].astype(F32) * mixed).astype(BF16)


def _post(alpha, prologue, pro_args, pro_specs, x, w_bf16, mod, ln_g, ln_b, wr_hi, wr_lo, br, tm):
    rows, d = x.shape
    ne = wr_hi.shape[0]
    nchunk = d // LANES
    vec = pl.BlockSpec((1, d), lambda i: (0, 0))
    return pl.pallas_call(
        functools.partial(_post_body, alpha, prologue, len(pro_args)),
        grid=(rows // tm,),
        in_specs=list(pro_specs) + [
            pl.BlockSpec((tm, d), lambda i: (i, 0)),
            pl.BlockSpec(w_bf16.shape, lambda i: (0, 0)),
            mod.spec(2, 1), vec, vec, mod.spec(3, 1), mod.spec(4, 1),
            pl.BlockSpec((ne, d), lambda i: (0, 0)),
            pl.BlockSpec((ne, d), lambda i: (0, 0)),
            pl.BlockSpec((ne, 1), lambda i: (0, 0)),
        ],
        out_specs=[
            pl.BlockSpec((tm, d), lambda i: (i, 0)),
            pl.BlockSpec((tm * nchunk, LANES), lambda i: (i, 0)),
            pl.BlockSpec((ne, tm), lambda i: (0, i)),
        ],
        out_shape=[
            jax.ShapeDtypeStruct((rows, d), F32),
            jax.ShapeDtypeStruct((rows * nchunk, LANES), F32),
            jax.ShapeDtypeStruct((ne, rows), F32),
        ],
        compiler_params=_cparams(("arbitrary",)),
        name="proj_postnorm_router",
    )(*pro_args, x, w_bf16, mod.arr, ln_g, ln_b, mod.arr, mod.arr, wr_hi, wr_lo, br)


def _route_body(lg_ref, eid_ref, gate_ref, dest_ref, meta_ref):
    ne = MOE_GROUPS * MOE_EXPERTS_PER_GROUP
    epg = MOE_EXPERTS_PER_GROUP
    lgt = lg_ref[...]
    t = lgt.shape[1]
    row8 = lax.broadcasted_iota(I32, (SUBLANES, t), 0)
    lgp = jnp.where(row8 < MOE_GROUPS, lgt[0:SUBLANES], -jnp.inf)
    mg = lgp.max(axis=0, keepdims=True)
    grp = jnp.where(lgp == mg, row8, SUBLANES).min(axis=0, keepdims=True)
    pg = 1.0 / jnp.exp(lgp - mg).sum(axis=0, keepdims=True)

    le = jnp.zeros((epg, t), F32)
    for g in range(MOE_GROUPS):
        le = jnp.where(grp == g, lgt[SUBLANES + g * epg:SUBLANES + (g + 1) * epg], le)
    m1 = le.max(axis=0, keepdims=True)
    i1 = jnp.where(le == m1, row8, epg).min(axis=0, keepdims=True)
    le2 = jnp.where(row8 == i1, -jnp.inf, le)
    m2 = le2.max(axis=0, keepdims=True)
    i2 = jnp.where(le2 == m2, row8, epg).min(axis=0, keepdims=True)
    e = jnp.exp(m2 - m1)
    g1 = pg / (1.0 + e)
    g2 = pg * e / (1.0 + e)
    e1 = grp * epg + i1
    e2 = grp * epg + i2
    eid_ref[0:1, :] = e1
    eid_ref[1:2, :] = e2
    gate_ref[0:1, :] = g1
    gate_ref[1:2, :] = g2

    rows = lax.broadcasted_iota(I32, (ne, t), 0)
    oh1 = (rows == e1).astype(F32)
    oh2 = (rows == e2).astype(F32)
    oh = (oh1 + oh2).astype(BF16)
    cw = 512
    ur = lax.broadcasted_iota(I32, (cw, cw), 0)
    uc = lax.broadcasted_iota(I32, (cw, cw), 1)
    upper = (ur < uc).astype(BF16)
    carry = jnp.zeros((ne, 1), F32)
    pref = []
    for ci in range(t // cw):
        ohc = oh[:, ci * cw:(ci + 1) * cw]
        pref.append(jnp.dot(ohc, upper, preferred_element_type=F32) + carry)
        carry = carry + ohc.astype(F32).sum(axis=1, keepdims=True)
    cnt_before = jnp.concatenate(pref, axis=1)

    counts = carry
    shift = MOE_ROWS.bit_length() - 1
    padded = (((counts.astype(I32) + (MOE_ROWS - 1)) >> shift) << shift).astype(F32)
    lr = lax.broadcasted_iota(I32, (ne, ne), 0)
    lc = lax.broadcasted_iota(I32, (ne, ne), 1)
    lower = (lc < lr).astype(F32)
    pstart = jnp.dot(lower, jnp.broadcast_to(padded, (ne, LANES)), precision=lax.Precision.HIGHEST,
                     preferred_element_type=F32)[:, 0:1]
    pend = pstart + padded
    base = pstart + cnt_before
    d1 = (oh1 * base).sum(axis=0, keepdims=True)
    d2 = (oh2 * base).sum(axis=0, keepdims=True)
    dest_ref[0:1, :] = d1.astype(I32)
    dest_ref[1:2, :] = d2.astype(I32)

    mw = meta_ref.shape[1]
    blk_start = (lax.broadcasted_iota(I32, (ne, mw), 1) * MOE_ROWS).astype(F32)
    blk_e = jnp.minimum((pend <= blk_start).astype(F32).sum(axis=0, keepdims=True), ne - 1.0)
    last = lax.broadcasted_iota(I32, (ne, mw), 0) == ne - 1
    nused = jnp.where(last, jnp.broadcast_to(pend, (ne, mw)), 0.0).sum(axis=0, keepdims=True) * (1.0 / MOE_ROWS)
    lane_e = lax.broadcasted_iota(I32, (ne, mw), 1)
    diag = lane_e == lax.broadcasted_iota(I32, (ne, mw), 0)
    fill_lo = jnp.where(diag, jnp.broadcast_to(pstart + counts, (ne, mw)), 0.0).sum(axis=0, keepdims=True)
    fill_hi = jnp.where(diag, jnp.broadcast_to(pend, (ne, mw)), 0.0).sum(axis=0, keepdims=True)
    mrow = lax.broadcasted_iota(I32, (SUBLANES, mw), 0)
    meta = jnp.where(mrow == 0, blk_e, jnp.where(mrow == 1, nused, jnp.where(mrow == 2, fill_lo, fill_hi)))
    meta_ref[...] = meta.astype(I32)


def _route(logits_t, n_blocks_max):
    ne_pad, t = logits_t.shape
    mw = 256
    assert n_blocks_max <= mw
    return pl.pallas_call(
        _route_body,
        out_shape=[
            jax.ShapeDtypeStruct((2, t), I32),
            jax.ShapeDtypeStruct((2, t), F32),
            jax.ShapeDtypeStruct((2, t), I32),
            jax.ShapeDtypeStruct((SUBLANES, mw), I32),
        ],
        compiler_params=pltpu.CompilerParams(vmem_limit_bytes=VMEM_LIMIT),
        name="moe_route",
    )(logits_t)


def _dispatch_body(nchunk, n_prompt_tiles, fill_ref, dest_ref, hp_ref, hs_ref, xb_ref, zero_scr, sem):
    i = pl.program_id(0)
    tm = dest_ref.shape[1]
    n_ranges = fill_ref.shape[1]

    def slab(ref, row):
        return ref.at[pl.ds(pl.multiple_of(row * nchunk, nchunk), nchunk)]

    def issue(src_ref):
        def body(tok, carry):
            for k in range(2):
                pltpu.make_async_copy(slab(src_ref, tok), slab(xb_ref, dest_ref[k, tok]), sem).start()
            return carry
        lax.fori_loop(0, tm, body, 0, unroll=8)

    @pl.when(i < n_prompt_tiles)
    def _():
        issue(hp_ref)

    @pl.when(i >= n_prompt_tiles)
    def _():
        issue(hs_ref)

    for _ in range(2):
        pltpu.make_async_copy(hp_ref, xb_ref.at[pl.ds(0, tm * nchunk)], sem).wait()

    @pl.when(i == 0)
    def _():
        zero_scr[...] = jnp.zeros_like(zero_scr)
        total = 0
        for e in range(n_ranges):
            lo, hi = fill_ref[0, e], fill_ref[1, e]

            def body(r, carry):
                pltpu.make_async_copy(zero_scr, slab(xb_ref, r), sem).start()
                return carry
            lax.fori_loop(lo, hi, body, 0)
            total = total + (hi - lo)

        def wbody(_, carry):
            pltpu.make_async_copy(zero_scr, slab(xb_ref, 0), sem).wait()
            return carry
        lax.fori_loop(0, total, wbody, 0)


def _dispatch(fill, dest_tiles, h2c_p, h2c_s, n_rows, nchunk):
    n_tiles, _, tm = dest_tiles.shape
    n_prompt_tiles = h2c_p.shape[0] // (nchunk * tm)
    grid_spec = pltpu.PrefetchScalarGridSpec(
        num_scalar_prefetch=1,
        grid=(n_tiles,),
        in_specs=[
            pl.BlockSpec((None, 2, tm), lambda i, f: (i, 0, 0), memory_space=pltpu.SMEM),
            pl.BlockSpec((tm * nchunk, LANES), lambda i, f: (jnp.minimum(i, n_prompt_tiles - 1), 0)),
            pl.BlockSpec((tm * nchunk, LANES), lambda i, f: (jnp.maximum(i - n_prompt_tiles, 0), 0)),
        ],
        out_specs=pl.BlockSpec(memory_space=pl.ANY),
        scratch_shapes=[pltpu.VMEM((nchunk, LANES), F32), pltpu.SemaphoreType.DMA(())],
    )
    return pl.pallas_call(
        functools.partial(_dispatch_body, nchunk, n_prompt_tiles),
        grid_spec=grid_spec,
        out_shape=jax.ShapeDtypeStruct((n_rows * nchunk, LANES), F32),
        compiler_params=_cparams(("arbitrary",)),
        name="moe_dispatch",
    )(fill, dest_tiles, h2c_p, h2c_s)


def _expert_body(blk_ref, nused_ref, x_ref, wg_ref, wu_ref, wd_ref, y_ref, wg_scr, wu_scr, wd_scr):
    i = pl.program_id(0)
    nchunk = x_ref.shape[0] // MOE_ROWS

    @pl.when(i < nused_ref[0])
    def _():
        prev = blk_ref[jnp.maximum(i - 1, 0)]

        @pl.when((i == 0) | (blk_ref[i] != prev))
        def _():
            wg_scr[...] = wg_ref[...].astype(BF16)
            wu_scr[...] = wu_ref[...].astype(BF16)
            wd_scr[...] = wd_ref[...].astype(BF16)

        x = jnp.concatenate([x_ref[pl.ds(c, MOE_ROWS, stride=nchunk), :] for c in range(nchunk)],
                            axis=1).astype(BF16)
        g = jnp.dot(x, wg_scr[...], preferred_element_type=F32)
        u = jnp.dot(x, wu_scr[...], preferred_element_type=F32)
        a = (jax.nn.silu(g) * u).astype(BF16)
        y = jnp.dot(a, wd_scr[...], preferred_element_type=F32)
        for c in range(nchunk):
            y_ref[pl.ds(c, MOE_ROWS, stride=nchunk), :] = y[:, c * LANES:(c + 1) * LANES]

    @pl.when(i >= nused_ref[0])
    def _():
        y_ref[...] = jnp.zeros_like(y_ref)


def _experts(blk_e, nused, xb, w_gate, w_up, w_down, layer, n_blocks):
    _, ne, d, de = w_gate.shape
    nchunk = d // LANES
    rows_c = MOE_ROWS * nchunk

    def blk(i, be, nu):
        return jnp.minimum(i, nu[0] - 1)

    grid_spec = pltpu.PrefetchScalarGridSpec(
        num_scalar_prefetch=2,
        grid=(n_blocks,),
        in_specs=[
            pl.BlockSpec((rows_c, LANES), lambda i, be, nu: (blk(i, be, nu), 0)),
            pl.BlockSpec((None, None, d, de), lambda i, be, nu: (layer, be[blk(i, be, nu)], 0, 0)),
            pl.BlockSpec((None, None, d, de), lambda i, be, nu: (layer, be[blk(i, be, nu)], 0, 0)),
            pl.BlockSpec((None, None, de, d), lambda i, be, nu: (layer, be[blk(i, be, nu)], 0, 0)),
        ],
        out_specs=pl.BlockSpec((rows_c, LANES), lambda i, be, nu: (i, 0)),
        scratch_shapes=[pltpu.VMEM((d, de), BF16), pltpu.VMEM((d, de), BF16), pltpu.VMEM((de, d), BF16)],
    )
    return pl.pallas_call(
        _expert_body,
        grid_spec=grid_spec,
        out_shape=jax.ShapeDtypeStruct(xb.shape, F32),
        compiler_params=_cparams(("arbitrary",)),
        name="moe_experts",
    )(blk_e, nused, xb, w_gate, w_up, w_down)


def _combine_body(alpha, dest_ref, x_ref, gts_ref, gt_ref, lng_ref, lnb_ref, yb_ref, o_ref, buf, sem):
    tm, d = x_ref.shape
    nchunk = d // LANES

    def slab(ref, row):
        return ref.at[pl.ds(pl.multiple_of(row * nchunk, nchunk), nchunk)]

    def body(tok, carry):
        for k in range(2):
            pltpu.make_async_copy(slab(yb_ref, dest_ref[k, tok]), slab(buf.at[k], tok), sem).start()
        return carry
    lax.fori_loop(0, tm, body, 0, unroll=8)

    for k in range(2):
        pltpu.make_async_copy(yb_ref.at[pl.ds(0, tm * nchunk)], buf.at[k], sem).wait()

    gts = gts_ref[...]
    f = jnp.zeros((tm, d), F32)
    for k in range(2):
        yk = jnp.concatenate([buf[k, pl.ds(c, tm, stride=nchunk), :] for c in range(nchunk)], axis=1)
        f = f + gts[:, k:k + 1] * yk
    z = alpha * x_ref[...] + (1.0 + gt_ref[...]) * f
    o_ref[...] = _layer_norm(z, lng_ref[...], lnb_ref[...])


def _combine(alpha, dest_tiles, x1, gates_t, mod, ln_g, ln_b, yb):
    rows, d = x1.shape
    n_tiles, _, tm = dest_tiles.shape
    nchunk = d // LANES
    vec = pl.BlockSpec((1, d), lambda i: (0, 0))
    if mod.tiles_per_seq is not None:
        mod = _Mod(mod.arr, d, mod.tiles_per_seq * (ROW_TILE // tm))
    return pl.pallas_call(
        functools.partial(_combine_body, alpha),
        grid=(n_tiles,),
        in_specs=[
            pl.BlockSpec((None, 2, tm), lambda i: (i, 0, 0), memory_space=pltpu.SMEM),
            pl.BlockSpec((tm, d), lambda i: (i, 0)),
            pl.BlockSpec((tm, 2), lambda i: (i, 0)),
            mod.spec(5, 1), vec, vec,
            pl.BlockSpec(memory_space=pl.ANY),
        ],
        out_specs=pl.BlockSpec((tm, d), lambda i: (i, 0)),
        out_shape=jax.ShapeDtypeStruct((rows, d), F32),
        scratch_shapes=[pltpu.VMEM((2, tm * nchunk, LANES), F32), pltpu.SemaphoreType.DMA(())],
        compiler_params=_cparams(("arbitrary",)),
        name="moe_combine_postnorm",
    )(dest_tiles, x1, gates_t, mod.arr, ln_g, ln_b, yb)


def _gmlp_in_body(x_ref, sh_ref, sc_ref, w_ref, vg_ref, vb_ref, u_ref, v_ref, h_scr, v_scr):
    j = pl.program_id(1)
    half = v_scr.shape[0]

    @pl.when(j == 0)
    def _():
        h_scr[...] = (x_ref[...] * (1.0 + sc_ref[...]) + sh_ref[...]).astype(BF16)

    y = jax.nn.gelu(jnp.dot(h_scr[...], w_ref[...], preferred_element_type=F32))

    @pl.when(j < half)
    def _():
        u_ref[...] = y.astype(u_ref.dtype)

    @pl.when(j >= half)
    def _():
        v_scr[j - half] = y

    @pl.when(j == 2 * half - 1)
    def _():
        v = jnp.concatenate([v_scr[c] for c in range(half)], axis=1)
        v_ref[...] = _layer_norm(v, vg_ref[...], vb_ref[...]).astype(v_ref.dtype)


def _gmlp_in(x, mod, w_in_bf16, vn_g, vn_b, tm, v_dtype):
    rows, d = x.shape
    tn = 512
    half = d // tn
    vec = pl.BlockSpec((1, d), lambda i, j: (0, 0))
    return pl.pallas_call(
        _gmlp_in_body,
        grid=(rows // tm, 2 * half),
        in_specs=[
            pl.BlockSpec((tm, d), lambda i, j: (i, 0)),
            mod.spec(0, 2), mod.spec(1, 2),
            pl.BlockSpec((d, tn), lambda i, j: (0, j)),
            vec, vec,
        ],
        out_specs=[
            pl.BlockSpec((tm, tn), lambda i, j: (i, jnp.minimum(j, half - 1))),
            pl.BlockSpec((tm, d), lambda i, j: (i, 0)),
        ],
        out_shape=[jax.ShapeDtypeStruct((rows, d), BF16), jax.ShapeDtypeStruct((rows, d), v_dtype)],
        scratch_shapes=[pltpu.VMEM((tm, d), BF16), pltpu.VMEM((half, tm, tn), F32)],
        compiler_params=_cparams(("arbitrary", "arbitrary")),
        name="gmlp_in",
    )(x, mod.arr, mod.arr, w_in_bf16, vn_g, vn_b)


def _moe_and_norm(alpha, x1_p, x1_s, h2c_p, h2c_s, lg_p, lg_s, mod_p, mod_s, ln_g, ln_b,
                  w_gate, w_up, w_down, layer):
    t_p, d = x1_p.shape
    t_s = x1_s.shape[0]
    t = t_p + t_s
    nchunk = d // LANES
    ne = w_gate.shape[1]
    n_blocks = -(-(2 * t) // MOE_ROWS) + ne
    n_rows = n_blocks * MOE_ROWS

    eid, gates, dest, meta = _route(jnp.concatenate([lg_p, lg_s], axis=1), n_blocks)
    blk_e = meta[0, :n_blocks]
    nused = meta[1, :1]
    tail = jnp.stack([nused * MOE_ROWS, jnp.full((1,), n_rows, I32)])
    fill = jnp.concatenate([meta[2:4, :ne], tail], axis=1)

    def tiles(a, tm):
        return a.reshape(2, -1, tm).transpose(1, 0, 2)

    xb = _dispatch(fill, tiles(dest, ROW_TILE), h2c_p, h2c_s, n_rows, nchunk)
    yb = _experts(blk_e, nused, xb, w_gate, w_up, w_down, layer, n_blocks)
    gates_t = gates.T
    tm = COMBINE_TILE
    x2_p = _combine(alpha, tiles(dest[:, :t_p], tm), x1_p, gates_t[:t_p], mod_p, ln_g, ln_b, yb)
    x2_s = _combine(alpha, tiles(dest[:, t_p:], min(tm, SAMPLE_TILE)), x1_s, gates_t[t_p:], mod_s, ln_g, ln_b, yb)
    return x2_p, x2_s


def _router_weights(w_rg, b_rg, w_re, b_re):
    d = w_rg.shape[0]
    pad = jnp.zeros((SUBLANES - MOE_GROUPS, d), F32)
    w = jnp.concatenate([w_rg.T, pad, w_re.T], axis=0)
    b = jnp.concatenate([b_rg, jnp.zeros((SUBLANES - MOE_GROUPS,), F32), b_re])[:, None]
    hi = w.astype(BF16)
    lo = (w - hi.astype(F32)).astype(BF16)
    return hi, lo, b


def kernel(x_prompt, x_sample, cache_k, cache_v, page_table, c_prompt, c_sample, w_ada, b_ada, ln_g, ln_b,
           attn_w_qkv, attn_w_o, gmlp_w_in, gmlp_vn_g, gmlp_vn_b, gmlp_w_s, gmlp_b_s, gmlp_w_out,
           moe_w_rg, moe_b_rg, moe_w_re, moe_b_re, moe_w_gate, moe_w_up, moe_w_down):
    batch, seq, d = x_prompt.shape
    nseq, dec_seq, _ = x_sample.shape
    depth = w_ada.shape[0]
    dh = d // N_HEADS
    past_len = page_table.shape[1] * cache_k.shape[2]
    alpha = (2.0 * depth) ** 0.25
    tiles_per_seq = seq // ROW_TILE

    xp = x_prompt.reshape(batch * seq, d)
    xs = x_sample.transpose(1, 0, 2).reshape(dec_seq * nseq, d)

    pad_rows = (-(batch + nseq)) % SUBLANES
    c_all = jnp.concatenate([c_prompt, c_sample, jnp.zeros((pad_rows, d), F32)], axis=0)
    m_all = _adaln(c_all, w_ada, b_ada)

    tabs_p = _rope_tables(jnp.arange(seq, dtype=I32), dh)
    tabs_s = _rope_tables(past_len + jnp.repeat(jnp.arange(dec_seq, dtype=I32), nseq), dh)

    new_kp, new_vp, new_ks, new_vs, new_gv = [], [], [], [], []
    for l in range(depth):
        mod_p = _Mod(m_all[l, :batch].reshape(batch, 1, 6 * d), d, tiles_per_seq)
        mod_in = _Mod(mod_p.arr, d, seq // IN_TILE)
        mod_s = _Mod(m_all[l, batch:batch + nseq], d, None)
        lng0, lnb0 = ln_g[l, 0][None, :], ln_b[l, 0][None, :]
        lng1, lnb1 = ln_g[l, 1][None, :], ln_b[l, 1][None, :]
        wr_hi, wr_lo, br = _router_weights(moe_w_rg[l], moe_b_rg[l], moe_w_re[l], moe_b_re[l])
        post = functools.partial(_post, alpha)

        if l % 2 == 0:
            la = l // 2
            w_qkv = attn_w_qkv[la].astype(BF16)
            w_o = attn_w_o[la].astype(BF16)
            q_p, k_p, v_p = _qkv(xp, mod_in, w_qkv, tabs_p, lambda i: i % (seq // IN_TILE), IN_TILE, BF16)
            q_s, k_s, v_s = _qkv(xs, mod_s, w_qkv, tabs_s, lambda i: i, SAMPLE_TILE, F32)
            o_p = _attn_prompt(q_p, k_p, v_p, batch, seq)

            def seq_rows(a):
                return a.reshape(dec_seq, nseq, N_HEADS, dh).transpose(1, 0, 2, 3).reshape(nseq, dec_seq * N_HEADS, dh)

            o_s = _attn_sample(seq_rows(q_s), seq_rows(k_s), seq_rows(v_s), cache_k, cache_v, la, page_table)
            o_s = o_s.reshape(nseq, dec_seq, d).transpose(1, 0, 2).reshape(dec_seq * nseq, d).astype(BF16)
            new_kp.append(k_p.reshape(batch, seq, N_HEADS, dh))
            new_vp.append(v_p.reshape(batch, seq, N_HEADS, dh))
            new_ks.append(k_s.reshape(dec_seq, nseq, N_HEADS, dh).transpose(1, 0, 2, 3))
            new_vs.append(v_s.reshape(dec_seq, nseq, N_HEADS, dh).transpose(1, 0, 2, 3))
            x1_p, h2c_p, lg_p = post(_pro_identity, (o_p,), [pl.BlockSpec((ROW_TILE, d), lambda i: (i, 0))],
                                     xp, w_o, mod_p, lng0, lnb0, wr_hi, wr_lo, br, ROW_TILE)
            x1_s, h2c_s, lg_s = post(_pro_identity, (o_s,), [pl.BlockSpec((SAMPLE_TILE, d), lambda i: (i, 0))],
                                     xs, w_o, mod_s, lng0, lnb0, wr_hi, wr_lo, br, SAMPLE_TILE)
        else:
            lb = l // 2
            w_in = gmlp_w_in[lb].astype(BF16)
            w_out = gmlp_w_out[lb].astype(BF16)
            vg, vb = gmlp_vn_g[lb][None, :], gmlp_vn_b[lb][None, :]
            u_p, vn_p = _gmlp_in(xp, mod_in, w_in, vg, vb, IN_TILE, BF16)
            u_s, vn_s = _gmlp_in(xs, mod_s, w_in, vg, vb, SAMPLE_TILE, F32)
            new_gv.append(vn_s.reshape(dec_seq, nseq, d).transpose(1, 0, 2))
            gw = d // GMLP_GROUPS
            bexp = jnp.repeat(gmlp_b_s[lb].T, gw, axis=1)
            wexp = jnp.repeat(gmlp_w_s[lb][:, :dec_seq, :dec_seq].transpose(1, 2, 0).reshape(dec_seq * dec_seq, -1),
                              gw, axis=1)
            tile = pl.BlockSpec((ROW_TILE, d), lambda i: (i, 0))
            x1_p, h2c_p, lg_p = post(
                _pro_gmlp_prompt, (u_p, vn_p, gmlp_w_s[lb], bexp),
                [tile, tile, pl.BlockSpec(gmlp_w_s[lb].shape, lambda i: (0, 0, 0)),
                 pl.BlockSpec(bexp.shape, lambda i: (0, 0))],
                xp, w_out, mod_p, lng0, lnb0, wr_hi, wr_lo, br, ROW_TILE)
            x1_s, h2c_s, lg_s = post(
                _pro_gmlp_sample, (u_s, vn_s, wexp, bexp[:SUBLANES]),
                [pl.BlockSpec((SAMPLE_TILE, d), lambda i: (i, 0)),
                 pl.BlockSpec(vn_s.shape, lambda i: (0, 0)),
                 pl.BlockSpec(wexp.shape, lambda i: (0, 0)),
                 pl.BlockSpec((SUBLANES, d), lambda i: (0, 0))],
                xs, w_out, mod_s, lng0, lnb0, wr_hi, wr_lo, br, SAMPLE_TILE)

        xp, xs = _moe_and_norm(alpha, x1_p, x1_s, h2c_p, h2c_s, lg_p, lg_s, mod_p, mod_s, lng1, lnb1,
                               moe_w_gate, moe_w_up, moe_w_down, l)

    y_p = xp.reshape(batch, seq, d)
    y_s = xs.reshape(dec_seq, nseq, d).transpose(1, 0, 2)
    return (y_p, y_s, jnp.stack(new_kp), jnp.stack(new_vp), jnp.stack(new_ks), jnp.stack(new_vs),
            jnp.stack(new_gv))
```

```python
import functools

import jax
import jax.numpy as jnp
from jax import lax
from jax.experimental import pallas as pl
from jax.experimental.pallas import tpu as pltpu

F32, BF16, I32 = jnp.float32, jnp.bfloat16, jnp.int32

N_HEADS = 16
MOBA_BLOCK = 256
MOBA_TOPK = 3
ROPE_THETA = 500000.0
GMLP_CHUNK = 128
GMLP_GROUPS = 8
MOE_GROUPS = 4
MOE_EXPERTS_PER_GROUP = 8
LN_EPS = 1e-5

LANES = 128
SUBLANES = 8
VMEM_LIMIT = 56 * 1024 * 1024

ROW_TILE = 512
IN_TILE = 1024
QKV_COLS = 512
SAMPLE_TILE = 128
MOE_ROWS = 128
COMBINE_TILE = 256
PAGES_PER_STEP = 8

NT_DIMS = (((1,), (1,)), ((), ()))


def _cparams(sem):
    return pltpu.CompilerParams(dimension_semantics=sem, vmem_limit_bytes=VMEM_LIMIT)


def _layer_norm(z, g, b):
    mu = jnp.mean(z, axis=-1, keepdims=True)
    zc = z - mu
    var = jnp.mean(zc * zc, axis=-1, keepdims=True)
    return zc * lax.rsqrt(var + LN_EPS) * g + b


def _adaln_body(c_ref, w_ref, b_ref, o_ref):
    a = jax.nn.silu(c_ref[...]).astype(BF16)
    o_ref[...] = jnp.dot(a, w_ref[...].astype(BF16), preferred_element_type=F32) + b_ref[...]


def _adaln(c_all, w_ada, b_ada):
    depth, d, n = w_ada.shape
    rows = c_all.shape[0]
    tn = 1024
    return pl.pallas_call(
        _adaln_body,
        grid=(depth, n // tn),
        in_specs=[
            pl.BlockSpec((rows, d), lambda l, j: (0, 0)),
            pl.BlockSpec((None, d, tn), lambda l, j: (l, 0, j)),
            pl.BlockSpec((None, 1, tn), lambda l, j: (l, 0, j)),
        ],
        out_specs=pl.BlockSpec((None, rows, tn), lambda l, j: (l, 0, j)),
        out_shape=jax.ShapeDtypeStruct((depth, rows, n), F32),
        compiler_params=_cparams(("arbitrary", "arbitrary")),
        name="adaln",
    )(c_all, w_ada, b_ada.reshape(depth, 1, n))


class _Mod:
    def __init__(self, arr, d, tiles_per_seq):
        self.arr = arr
        self.d = d
        self.tiles_per_seq = tiles_per_seq

    def spec(self, k, ngrid):
        d, tps = self.d, self.tiles_per_seq
        if tps is None:
            rows = self.arr.shape[0]
            if ngrid == 1:
                return pl.BlockSpec((rows, d), lambda i: (0, k))
            if ngrid == 2:
                return pl.BlockSpec((rows, d), lambda i, j: (0, k))
            return pl.BlockSpec((rows, d), lambda i, p, j: (0, k))
        if ngrid == 1:
            return pl.BlockSpec((None, 1, d), lambda i: (i // tps, 0, k))
        if ngrid == 2:
            return pl.BlockSpec((None, 1, d), lambda i, j: (i // tps, 0, k))
        return pl.BlockSpec((None, 1, d), lambda i, p, j: (i // tps, 0, k))


def _qkv_body(x_ref, sh_ref, sc_ref, wq_ref, wk_ref, wv_ref, c_ref, s1_ref, s2_ref, q_ref, k_ref, v_ref, h_scr):
    j = pl.program_id(1)

    @pl.when(j == 0)
    def _():
        h_scr[...] = (x_ref[...] * (1.0 + sc_ref[...]) + sh_ref[...]).astype(BF16)

    tn = q_ref.shape[1]
    reps = tn // LANES
    c = jnp.tile(c_ref[...], (1, reps))
    s1 = jnp.tile(s1_ref[...], (1, reps))
    s2 = jnp.tile(s2_ref[...], (1, reps))
    half = c_ref.shape[1] // 8

    def rope(y):
        return y * c + pltpu.roll(y, tn - half, 1) * s1 + pltpu.roll(y, half, 1) * s2

    h = h_scr[...]
    q_ref[...] = rope(jnp.dot(h, wq_ref[...], preferred_element_type=F32)).astype(q_ref.dtype)
    k_ref[...] = rope(jnp.dot(h, wk_ref[...], preferred_element_type=F32))
    v_ref[...] = jnp.dot(h, wv_ref[...], preferred_element_type=F32)


def _qkv(x, mod, w_qkv_bf16, tabs, tab_index, tm, tn, q_dtype):
    rows, d = x.shape
    ncol = d // tn
    c_tab, s1_tab, s2_tab = tabs
    tab_spec = pl.BlockSpec((tm, LANES), lambda i, j: (tab_index(i), 0))
    out = pl.BlockSpec((tm, tn), lambda i, j: (i, j))

    def w_spec(part):
        return pl.BlockSpec((d, tn), lambda i, j: (0, part * ncol + j))

    return pl.pallas_call(
        _qkv_body,
        grid=(rows // tm, ncol),
        in_specs=[
            pl.BlockSpec((tm, d), lambda i, j: (i, 0)),
            mod.spec(0, 2),
            mod.spec(1, 2),
            w_spec(0), w_spec(1), w_spec(2),
            tab_spec, tab_spec, tab_spec,
        ],
        out_specs=[out, out, out],
        out_shape=[
            jax.ShapeDtypeStruct((rows, d), q_dtype),
            jax.ShapeDtypeStruct((rows, d), F32),
            jax.ShapeDtypeStruct((rows, d), F32),
        ],
        scratch_shapes=[pltpu.VMEM((tm, d), BF16)],
        compiler_params=_cparams(("arbitrary", "arbitrary")),
        name="qkv_rope",
    )(x, mod.arr, mod.arr, w_qkv_bf16, w_qkv_bf16, w_qkv_bf16, c_tab, s1_tab, s2_tab)


def _rope_tables(pos, head_dim):
    rot = head_dim // 4
    half = rot // 2
    inv = ROPE_THETA ** (-jnp.arange(half, dtype=F32) * 2.0 / rot)
    ang = pos.astype(F32)[:, None] * inv[None, :]
    cos, sin = jnp.cos(ang), jnp.sin(ang)
    n = pos.shape[0]
    ones = jnp.ones((n, head_dim - rot), F32)
    zeros = jnp.zeros((n, head_dim - rot), F32)
    zh = jnp.zeros((n, half), F32)
    c = jnp.concatenate([cos, cos, ones], axis=1)
    s1 = jnp.concatenate([-sin, zh, zeros], axis=1)
    s2 = jnp.concatenate([zh, sin, zeros], axis=1)
    return c, s1, s2


def _attn_prompt_body(q_ref, k_ref, v_ref, o_ref, kb_scr, vb_scr, sel_scr):
    seq, dh = q_ref.shape
    nblk = seq // MOBA_BLOCK
    scale = dh ** -0.5
    kf = k_ref[...]
    kb_scr[...] = kf.astype(BF16)
    vb_scr[:, :dh] = v_ref[...].astype(BF16)
    vb_scr[:, dh:] = jnp.ones((seq, dh), BF16)

    km = kf.reshape(nblk, MOBA_BLOCK, dh).sum(axis=1) * (1.0 / MOBA_BLOCK)
    g = lax.dot_general(km, q_ref[...].astype(F32), NT_DIMS, precision=lax.Precision.HIGHEST,
                        preferred_element_type=F32)
    blk = lax.broadcasted_iota(I32, (nblk, seq), 0)
    cur = lax.broadcasted_iota(I32, (nblk, seq), 1) // MOBA_BLOCK
    past = blk < cur
    g = jnp.where(past, g, -jnp.inf)
    cnt = jnp.zeros((nblk, seq), F32)
    for k in range(nblk):
        gk = g[k:k + 1, :]
        beats = (gk > g) | ((gk == g) & (k < blk))
        cnt = cnt + beats.astype(F32)
    sel = ((cnt < MOBA_TOPK) & past).astype(F32)
    sel_scr[...] = jnp.concatenate([sel, jnp.zeros((LANES - nblk, seq), F32)], axis=0).T

    row = lax.broadcasted_iota(I32, (MOBA_BLOCK, MOBA_BLOCK), 0)
    col = lax.broadcasted_iota(I32, (MOBA_BLOCK, MOBA_BLOCK), 1)
    causal = row >= col

    for c in range(nblk):
        lo, hi = c * MOBA_BLOCK, (c + 1) * MOBA_BLOCK
        qc = q_ref[lo:hi, :]
        selc = sel_scr[lo:hi, :]
        slabs = []
        for j in range(c + 1):
            s = lax.dot_general(qc, kb_scr[j * MOBA_BLOCK:(j + 1) * MOBA_BLOCK, :], NT_DIMS,
                                preferred_element_type=F32) * scale
            if j < c:
                s = jnp.where(selc[:, j:j + 1] > 0.5, s, -jnp.inf)
            else:
                s = jnp.where(causal, s, -jnp.inf)
            slabs.append(s)
        mx = slabs[0]
        for s in slabs[1:]:
            mx = jnp.maximum(mx, s)
        m = mx.max(axis=1, keepdims=True)
        acc = jnp.zeros((MOBA_BLOCK, 2 * dh), F32)
        for j, s in enumerate(slabs):
            p = jnp.exp(s - m)
            acc = acc + jnp.dot(p.astype(BF16), vb_scr[j * MOBA_BLOCK:(j + 1) * MOBA_BLOCK, :],
                                preferred_element_type=F32)
        o_ref[lo:hi, :] = (acc[:, :dh] / acc[:, dh:]).astype(o_ref.dtype)


def _attn_prompt(q, k, v, batch, seq):
    rows, d = q.shape
    dh = d // N_HEADS
    spec = pl.BlockSpec((seq, dh), lambda b, h: (b, h))
    return pl.pallas_call(
        _attn_prompt_body,
        grid=(batch, N_HEADS),
        in_specs=[spec, spec, spec],
        out_specs=spec,
        out_shape=jax.ShapeDtypeStruct((rows, d), BF16),
        scratch_shapes=[pltpu.VMEM((seq, dh), BF16), pltpu.VMEM((seq, 2 * dh), BF16), pltpu.VMEM((seq, LANES), F32)],
        compiler_params=_cparams(("arbitrary", "arbitrary")),
        name="moba_prompt",
    )(q, k, v)


HEAD_GROUP = SUBLANES


def _attn_sample_body(pt_ref, q_ref, kn_ref, vn_ref, hm_ref, *rest):
    npg = PAGES_PER_STEP
    k_refs = rest[:npg]
    v_refs = rest[npg:2 * npg]
    o_ref = rest[2 * npg]
    s_scr, p_scr, ksum_scr, bmax_scr, acc_scr, linv_scr = rest[2 * npg + 1:]

    t = pl.program_id(1)
    n_pages, ngrp, nq, grp_rows = s_scr.shape
    hg = HEAD_GROUP
    k_steps = n_pages // npg
    dh = q_ref.shape[2]
    page = grp_rows // hg
    pages_per_blk = MOBA_BLOCK // page
    blk_per_step = npg // pages_per_blk
    nblk = n_pages // pages_per_blk
    scale = dh ** -0.5

    @pl.when(t < k_steps)
    def _():
        hm = hm_ref[...]
        sums = []
        pmax = [[] for _ in range(ngrp)]
        for i in range(npg):
            kp = k_refs[i][...]
            sums.append(jnp.sum(kp, axis=0))
            for g in range(ngrp):
                kb = kp[:, g * hg:(g + 1) * hg, :].reshape(grp_rows, dh).astype(BF16)
                s = lax.dot_general(q_ref[g].astype(BF16), kb, NT_DIMS, preferred_element_type=F32) * scale + hm
                s_scr[t * npg + i, g] = s
                pmax[g].append(s.max(axis=1, keepdims=True))
        for b in range(blk_per_step):
            j = t * blk_per_step + b
            tot = sums[b * pages_per_blk]
            for i in range(1, pages_per_blk):
                tot = tot + sums[b * pages_per_blk + i]
            ksum_scr[j] = tot
            for g in range(ngrp):
                bm = pmax[g][b * pages_per_blk]
                for i in range(1, pages_per_blk):
                    bm = jnp.maximum(bm, pmax[g][b * pages_per_blk + i])
                bmax_scr[j, g] = jnp.broadcast_to(bm, (nq, LANES))

    @pl.when(t == k_steps - 1)
    def _():
        gw = nblk * hg
        width = ngrp * gw
        km = jnp.concatenate(
            [ksum_scr[:, g * hg:(g + 1) * hg, :].reshape(gw, dh) for g in range(ngrp)], axis=0) * (1.0 / MOBA_BLOCK)
        qall = jnp.concatenate([q_ref[g] for g in range(ngrp)], axis=0)
        gate = lax.dot_general(qall, km, NT_DIMS, precision=lax.Precision.HIGHEST, preferred_element_type=F32)
        row = lax.broadcasted_iota(I32, (ngrp * nq, width), 0)
        lane = lax.broadcasted_iota(I32, (ngrp * nq, width), 1)
        own = ((row // nq) == (lane // gw)) & ((row % hg) == (lane % hg))
        blk = (lane % gw) // hg
        gate = jnp.where(own, gate, -jnp.inf)
        cnt = jnp.zeros((ngrp * nq, width), F32)
        for dlt in range(1, nblk):
            lower = pltpu.roll(gate, dlt * hg, 1)
            cnt = cnt + ((blk >= dlt) & (lower >= gate)).astype(F32)
            upper = pltpu.roll(gate, width - dlt * hg, 1)
            cnt = cnt + ((blk + dlt < nblk) & (upper > gate)).astype(F32)
        sel = jnp.where(own & (cnt < MOBA_TOPK), 1.0, 0.0)

        r = lax.broadcasted_iota(I32, (nq, nq), 0)
        c = lax.broadcasted_iota(I32, (nq, nq), 1)
        new_ok = ((r % hg) == (c % hg)) & ((c // hg) <= (r // hg))
        for g in range(ngrp):
            qb = q_ref[g].astype(BF16)
            sn = lax.dot_general(qb, kn_ref[g].astype(BF16), NT_DIMS, preferred_element_type=F32) * scale
            sn = jnp.where(new_ok, sn, -jnp.inf)
            m = sn.max(axis=1, keepdims=True)
            biases = []
            for j in range(nblk):
                lo = g * gw + j * hg
                picked = sel[g * nq:(g + 1) * nq, lo:lo + hg].max(axis=1, keepdims=True)
                bias = jnp.where(picked > 0.5, 0.0, -jnp.inf)
                biases.append(bias)
                m = jnp.maximum(m, bmax_scr[j, g][:, 0:1] + bias)
            en = jnp.exp(sn - m)
            l = en.sum(axis=1, keepdims=True)
            for pg in range(n_pages):
                e = jnp.exp(s_scr[pg, g] + (biases[pg // pages_per_blk] - m))
                l = l + e.sum(axis=1, keepdims=True)
                p_scr[pg, g] = e.astype(BF16)
            linv_scr[g] = jnp.broadcast_to(1.0 / l, (nq, dh))
            acc_scr[g] = jnp.dot(en.astype(BF16), vn_ref[g].astype(BF16), preferred_element_type=F32)

    @pl.when(t >= k_steps)
    def _():
        accs = [acc_scr[g] for g in range(ngrp)]
        for i in range(npg):
            vp = v_refs[i][...]
            for g in range(ngrp):
                vb = vp[:, g * hg:(g + 1) * hg, :].reshape(grp_rows, dh).astype(BF16)
                accs[g] = accs[g] + jnp.dot(p_scr[(t - k_steps) * npg + i, g], vb, preferred_element_type=F32)
        for g in range(ngrp):
            acc_scr[g] = accs[g]

    @pl.when(t == 2 * k_steps - 1)
    def _():
        o_ref[...] = acc_scr[...] * linv_scr[...]


def _attn_sample(q_rows, k_new, v_new, cache_k, cache_v, layer, page_table):
    nseq, ngrp, nq, dh = q_rows.shape
    _, n_phys, page, heads, _ = cache_k.shape
    n_pages = page_table.shape[1]
    npg = PAGES_PER_STEP
    k_steps = n_pages // npg
    grp_rows = page * HEAD_GROUP
    rowh = jnp.arange(nq, dtype=I32)[:, None] % HEAD_GROUP
    colh = jnp.arange(grp_rows, dtype=I32)[None, :] % HEAD_GROUP
    head_mask = jnp.where(rowh == colh, 0.0, -jnp.inf).astype(F32)

    def kmap(i):
        return lambda b, t, pt: (layer, pt[b * n_pages + jnp.minimum(t, k_steps - 1) * npg + i], 0, 0, 0)

    def vmap_(i):
        def index(b, t, pt):
            in_v = t >= k_steps
            seq_i = jnp.where(in_v, b, jnp.maximum(b - 1, 0))
            step = jnp.where(in_v, t - k_steps, k_steps - 1)
            return (layer, pt[seq_i * n_pages + step * npg + i], 0, 0, 0)
        return index

    seq_spec = pl.BlockSpec((None, ngrp, nq, dh), lambda b, t, pt: (b, 0, 0, 0))
    page_block = (None, None, page, heads, dh)
    nblk = n_pages * page // MOBA_BLOCK
    grid_spec = pltpu.PrefetchScalarGridSpec(
        num_scalar_prefetch=1,
        grid=(nseq, 2 * k_steps),
        in_specs=[seq_spec, seq_spec, seq_spec,
                  pl.BlockSpec((nq, grp_rows), lambda b, t, pt: (0, 0))]
                 + [pl.BlockSpec(page_block, kmap(i)) for i in range(npg)]
                 + [pl.BlockSpec(page_block, vmap_(i)) for i in range(npg)],
        out_specs=seq_spec,
        scratch_shapes=[
            pltpu.VMEM((n_pages, ngrp, nq, grp_rows), F32),
            pltpu.VMEM((n_pages, ngrp, nq, grp_rows), BF16),
            pltpu.VMEM((nblk, heads, dh), F32),
            pltpu.VMEM((nblk, ngrp, nq, LANES), F32),
            pltpu.VMEM((ngrp, nq, dh), F32),
            pltpu.VMEM((ngrp, nq, dh), F32),
        ],
    )
    return pl.pallas_call(
        _attn_sample_body,
        grid_spec=grid_spec,
        out_shape=jax.ShapeDtypeStruct((nseq, ngrp, nq, dh), F32),
        compiler_params=_cparams(("arbitrary", "arbitrary")),
        name="moba_sample",
    )(page_table.reshape(-1), q_rows, k_new, v_new, head_mask,
      *([cache_k] * npg), *([cache_v] * npg))


def _post_body(alpha, prologue, n_pro, *refs):
    pro_refs = refs[:n_pro]
    (x_ref, w_ref, gt_ref, lng_ref, lnb_ref, sh_ref, sc_ref, wrh_ref, wrl_ref, br_ref,
     x1_ref, h2c_ref, lg_ref) = refs[n_pro:]
    a = prologue(*pro_refs)
    y = jnp.dot(a, w_ref[...], preferred_element_type=F32)
    z = alpha * x_ref[...] + (1.0 + gt_ref[...]) * y
    x1 = _layer_norm(z, lng_ref[...], lnb_ref[...])
    x1_ref[...] = x1
    h2 = x1 * (1.0 + sc_ref[...]) + sh_ref[...]
    tm, d = h2.shape
    nchunk = d // LANES
    for c in range(nchunk):
        h2c_ref[pl.ds(c, tm, stride=nchunk), :] = h2[:, c * LANES:(c + 1) * LANES]
    hi = h2.astype(BF16)
    lo = (h2 - hi.astype(F32)).astype(BF16)
    wh, wl = wrh_ref[...], wrl_ref[...]
    lg = (lax.dot_general(wh, hi, NT_DIMS, preferred_element_type=F32)
          + lax.dot_general(wl, hi, NT_DIMS, preferred_element_type=F32)
          + lax.dot_general(wh, lo, NT_DIMS, preferred_element_type=F32))
    lg_ref[...] = lg + br_ref[...]


def _pro_identity(a_ref):
    return a_ref[...]


def _pro_gmlp_prompt(u_ref, v_ref, ws_ref, bexp_ref):
    tm, d = u_ref.shape
    gw = d // GMLP_GROUPS
    row = lax.broadcasted_iota(I32, (GMLP_CHUNK, GMLP_CHUNK), 0)
    col = lax.broadcasted_iota(I32, (GMLP_CHUNK, GMLP_CHUNK), 1)
    tri = row >= col
    wms = [jnp.where(tri, ws_ref[g], 0.0).astype(BF16) for g in range(GMLP_GROUPS)]
    outs = []
    for ch in range(tm // GMLP_CHUNK):
        lo, hi = ch * GMLP_CHUNK, (ch + 1) * GMLP_CHUNK
        cols = [jnp.dot(wms[g], v_ref[lo:hi, g * gw:(g + 1) * gw], preferred_element_type=F32)
                for g in range(GMLP_GROUPS)]
        mixed = jnp.concatenate(cols, axis=1) + bexp_ref[...]
        outs.append((u_ref[lo:hi, :].astype(F32) * mixed).astype(BF16))
    return jnp.concatenate(outs, axis=0)


def _pro_gmlp_sample(u_ref, v_ref, wexp_ref, bexp_ref):
    s = pl.program_id(0)
    tm = u_ref.shape[0]
    n_pos = v_ref.shape[0] // tm
    mixed = bexp_ref[pl.ds(s, 1), :]
    for j in range(n_pos):
        w = wexp_ref[pl.ds(s * n_pos + j, 1), :] * (j <= s).astype(F32)
        mixed = mixed + w * v_ref[j * tm:(j + 1) * tm, :]
    return (u_ref[...].astype(F32) * mixed).astype(BF16)


def _post(alpha, prologue, pro_args, pro_specs, x, w_bf16, mod, ln_g, ln_b, wr_hi, wr_lo, br, tm):
    rows, d = x.shape
    ne = wr_hi.shape[0]
    nchunk = d // LANES
    vec = pl.BlockSpec((1, d), lambda i: (0, 0))
    return pl.pallas_call(
        functools.partial(_post_body, alpha, prologue, len(pro_args)),
        grid=(rows // tm,),
        in_specs=list(pro_specs) + [
            pl.BlockSpec((tm, d), lambda i: (i, 0)),
            pl.BlockSpec(w_bf16.shape, lambda i: (0, 0)),
            mod.spec(2, 1), vec, vec, mod.spec(3, 1), mod.spec(4, 1),
            pl.BlockSpec((ne, d), lambda i: (0, 0)),
            pl.BlockSpec((ne, d), lambda i: (0, 0)),
            pl.BlockSpec((ne, 1), lambda i: (0, 0)),
        ],
        out_specs=[
            pl.BlockSpec((tm, d), lambda i: (i, 0)),
            pl.BlockSpec((tm * nchunk, LANES), lambda i: (i, 0)),
            pl.BlockSpec((ne, tm), lambda i: (0, i)),
        ],
        out_shape=[
            jax.ShapeDtypeStruct((rows, d), F32),
            jax.ShapeDtypeStruct((rows * nchunk, LANES), F32),
            jax.ShapeDtypeStruct((ne, rows), F32),
        ],
        compiler_params=_cparams(("arbitrary",)),
        name="proj_postnorm_router",
    )(*pro_args, x, w_bf16, mod.arr, ln_g, ln_b, mod.arr, mod.arr, wr_hi, wr_lo, br)


def _route_body(lg_ref, eid_ref, gate_ref, dest_ref, meta_ref):
    ne = MOE_GROUPS * MOE_EXPERTS_PER_GROUP
    epg = MOE_EXPERTS_PER_GROUP
    lgt = lg_ref[...]
    t = lgt.shape[1]
    row8 = lax.broadcasted_iota(I32, (SUBLANES, t), 0)
    lgp = jnp.where(row8 < MOE_GROUPS, lgt[0:SUBLANES], -jnp.inf)
    mg = lgp.max(axis=0, keepdims=True)
    grp = jnp.where(lgp == mg, row8, SUBLANES).min(axis=0, keepdims=True)
    pg = 1.0 / jnp.exp(lgp - mg).sum(axis=0, keepdims=True)

    le = jnp.zeros((epg, t), F32)
    for g in range(MOE_GROUPS):
        le = jnp.where(grp == g, lgt[SUBLANES + g * epg:SUBLANES + (g + 1) * epg], le)
    m1 = le.max(axis=0, keepdims=True)
    i1 = jnp.where(le == m1, row8, epg).min(axis=0, keepdims=True)
    le2 = jnp.where(row8 == i1, -jnp.inf, le)
    m2 = le2.max(axis=0, keepdims=True)
    i2 = jnp.where(le2 == m2, row8, epg).min(axis=0, keepdims=True)
    e = jnp.exp(m2 - m1)
    g1 = pg / (1.0 + e)
    g2 = pg * e / (1.0 + e)
    e1 = grp * epg + i1
    e2 = grp * epg + i2
    eid_ref[0:1, :] = e1
    eid_ref[1:2, :] = e2
    gate_ref[0:1, :] = g1
    gate_ref[1:2, :] = g2

    rows = lax.broadcasted_iota(I32, (ne, t), 0)
    oh1 = (rows == e1).astype(F32)
    oh2 = (rows == e2).astype(F32)
    oh = (oh1 + oh2).astype(BF16)
    cw = 512
    ur = lax.broadcasted_iota(I32, (cw, cw), 0)
    uc = lax.broadcasted_iota(I32, (cw, cw), 1)
    upper = (ur < uc).astype(BF16)
    carry = jnp.zeros((ne, 1), F32)
    pref = []
    for ci in range(t // cw):
        ohc = oh[:, ci * cw:(ci + 1) * cw]
        pref.append(jnp.dot(ohc, upper, preferred_element_type=F32) + carry)
        carry = carry + ohc.astype(F32).sum(axis=1, keepdims=True)
    cnt_before = jnp.concatenate(pref, axis=1)

    counts = carry
    shift = MOE_ROWS.bit_length() - 1
    padded = (((counts.astype(I32) + (MOE_ROWS - 1)) >> shift) << shift).astype(F32)
    lr = lax.broadcasted_iota(I32, (ne, ne), 0)
    lc = lax.broadcasted_iota(I32, (ne, ne), 1)
    lower = (lc < lr).astype(F32)
    pstart = jnp.dot(lower, jnp.broadcast_to(padded, (ne, LANES)), precision=lax.Precision.HIGHEST,
                     preferred_element_type=F32)[:, 0:1]
    pend = pstart + padded
    base = pstart + cnt_before
    d1 = (oh1 * base).sum(axis=0, keepdims=True)
    d2 = (oh2 * base).sum(axis=0, keepdims=True)
    dest_ref[0:1, :] = d1.astype(I32)
    dest_ref[1:2, :] = d2.astype(I32)

    mw = meta_ref.shape[1]
    blk_start = (lax.broadcasted_iota(I32, (ne, mw), 1) * MOE_ROWS).astype(F32)
    blk_e = jnp.minimum((pend <= blk_start).astype(F32).sum(axis=0, keepdims=True), ne - 1.0)
    last = lax.broadcasted_iota(I32, (ne, mw), 0) == ne - 1
    nused = jnp.where(last, jnp.broadcast_to(pend, (ne, mw)), 0.0).sum(axis=0, keepdims=True) * (1.0 / MOE_ROWS)
    lane_e = lax.broadcasted_iota(I32, (ne, mw), 1)
    diag = lane_e == lax.broadcasted_iota(I32, (ne, mw), 0)
    fill_lo = jnp.where(diag, jnp.broadcast_to(pstart + counts, (ne, mw)), 0.0).sum(axis=0, keepdims=True)
    fill_hi = jnp.where(diag, jnp.broadcast_to(pend, (ne, mw)), 0.0).sum(axis=0, keepdims=True)
    mrow = lax.broadcasted_iota(I32, (SUBLANES, mw), 0)
    meta = jnp.where(mrow == 0, blk_e, jnp.where(mrow == 1, nused, jnp.where(mrow == 2, fill_lo, fill_hi)))
    meta_ref[...] = meta.astype(I32)


def _route(logits_t, n_blocks_max):
    ne_pad, t = logits_t.shape
    mw = 256
    assert n_blocks_max <= mw
    return pl.pallas_call(
        _route_body,
        out_shape=[
            jax.ShapeDtypeStruct((2, t), I32),
            jax.ShapeDtypeStruct((2, t), F32),
            jax.ShapeDtypeStruct((2, t), I32),
            jax.ShapeDtypeStruct((SUBLANES, mw), I32),
        ],
        compiler_params=pltpu.CompilerParams(vmem_limit_bytes=VMEM_LIMIT),
        name="moe_route",
    )(logits_t)


def _dispatch_body(nchunk, n_prompt_tiles, fill_ref, dest_ref, hp_ref, hs_ref, xb_ref, zero_scr, sem):
    i = pl.program_id(0)
    tm = dest_ref.shape[1]
    n_ranges = fill_ref.shape[1]

    def slab(ref, row):
        return ref.at[pl.ds(pl.multiple_of(row * nchunk, nchunk), nchunk)]

    def issue(src_ref):
        def body(tok, carry):
            for k in range(2):
                pltpu.make_async_copy(slab(src_ref, tok), slab(xb_ref, dest_ref[k, tok]), sem).start()
            return carry
        lax.fori_loop(0, tm, body, 0, unroll=8)

    @pl.when(i < n_prompt_tiles)
    def _():
        issue(hp_ref)

    @pl.when(i >= n_prompt_tiles)
    def _():
        issue(hs_ref)

    for _ in range(2):
        pltpu.make_async_copy(hp_ref, xb_ref.at[pl.ds(0, tm * nchunk)], sem).wait()

    @pl.when(i == 0)
    def _():
        zero_scr[...] = jnp.zeros_like(zero_scr)
        total = 0
        for e in range(n_ranges):
            lo, hi = fill_ref[0, e], fill_ref[1, e]

            def body(r, carry):
                pltpu.make_async_copy(zero_scr, slab(xb_ref, r), sem).start()
                return carry
            lax.fori_loop(lo, hi, body, 0)
            total = total + (hi - lo)

        def wbody(_, carry):
            pltpu.make_async_copy(zero_scr, slab(xb_ref, 0), sem).wait()
            return carry
        lax.fori_loop(0, total, wbody, 0)


def _dispatch(fill, dest_tiles, h2c_p, h2c_s, n_rows, nchunk):
    n_tiles, _, tm = dest_tiles.shape
    n_prompt_tiles = h2c_p.shape[0] // (nchunk * tm)
    grid_spec = pltpu.PrefetchScalarGridSpec(
        num_scalar_prefetch=1,
        grid=(n_tiles,),
        in_specs=[
            pl.BlockSpec((None, 2, tm), lambda i, f: (i, 0, 0), memory_space=pltpu.SMEM),
            pl.BlockSpec((tm * nchunk, LANES), lambda i, f: (jnp.minimum(i, n_prompt_tiles - 1), 0)),
            pl.BlockSpec((tm * nchunk, LANES), lambda i, f: (jnp.maximum(i - n_prompt_tiles, 0), 0)),
        ],
        out_specs=pl.BlockSpec(memory_space=pl.ANY),
        scratch_shapes=[pltpu.VMEM((nchunk, LANES), F32), pltpu.SemaphoreType.DMA(())],
    )
    return pl.pallas_call(
        functools.partial(_dispatch_body, nchunk, n_prompt_tiles),
        grid_spec=grid_spec,
        out_shape=jax.ShapeDtypeStruct((n_rows * nchunk, LANES), F32),
        compiler_params=_cparams(("arbitrary",)),
        name="moe_dispatch",
    )(fill, dest_tiles, h2c_p, h2c_s)


def _expert_body(layer, blk_ref, nused_ref, ord_ref, uniq_ref, ndist_ref, x_ref, wg_hbm, wu_hbm, wd_hbm, y_ref,
                 wg_buf, wu_buf, wd_buf, wg_scr, wu_scr, wd_scr, sem):
    i = pl.program_id(0)
    nchunk = x_ref.shape[0] // MOE_ROWS
    pairs = ((wg_hbm, wg_buf), (wu_hbm, wu_buf), (wd_hbm, wd_buf))

    def copies(n, slot):
        e = uniq_ref[n]
        return [pltpu.make_async_copy(src.at[layer, e], dst.at[slot], sem.at[slot]) for src, dst in pairs]

    @pl.when(i < nused_ref[0])
    def _():
        prev = blk_ref[jnp.maximum(i - 1, 0)]

        @pl.when(i == 0)
        def _():
            for cp in copies(0, 0):
                cp.start()

        @pl.when((i == 0) | (blk_ref[i] != prev))
        def _():
            n = ord_ref[i]
            slot = n % 2
            for cp in copies(n, slot):
                cp.wait()

            @pl.when(n + 1 < ndist_ref[0])
            def _():
                for cp in copies(n + 1, 1 - slot):
                    cp.start()

            wg_scr[...] = wg_buf[slot].astype(BF16)
            wu_scr[...] = wu_buf[slot].astype(BF16)
            wd_scr[...] = wd_buf[slot].astype(BF16)

        x = jnp.concatenate([x_ref[pl.ds(c, MOE_ROWS, stride=nchunk), :] for c in range(nchunk)],
                            axis=1).astype(BF16)
        g = jnp.dot(x, wg_scr[...], preferred_element_type=F32)
        u = jnp.dot(x, wu_scr[...], preferred_element_type=F32)
        a = (jax.nn.silu(g) * u).astype(BF16)
        y = jnp.dot(a, wd_scr[...], preferred_element_type=F32)
        for c in range(nchunk):
            y_ref[pl.ds(c, MOE_ROWS, stride=nchunk), :] = y[:, c * LANES:(c + 1) * LANES]

    @pl.when(i >= nused_ref[0])
    def _():
        y_ref[...] = jnp.zeros_like(y_ref)


def _experts(blk_e, nused, xb, w_gate, w_up, w_down, layer, n_blocks):
    _, ne, d, de = w_gate.shape
    nchunk = d // LANES
    rows_c = MOE_ROWS * nchunk

    ids = jnp.arange(n_blocks, dtype=I32)
    change = (ids < nused[0]) & ((ids == 0) | (blk_e != jnp.roll(blk_e, 1)))
    ordinal = jnp.cumsum(change.astype(I32)) - 1
    n_distinct = change.astype(I32).sum()[None]
    uniq = jnp.zeros((ne + 1,), I32).at[jnp.where(change, ordinal, ne)].set(blk_e)

    def blk(i, be, nu, *_):
        return jnp.minimum(i, nu[0] - 1)

    grid_spec = pltpu.PrefetchScalarGridSpec(
        num_scalar_prefetch=5,
        grid=(n_blocks,),
        in_specs=[
            pl.BlockSpec((rows_c, LANES), lambda i, *s: (blk(i, *s), 0)),
            pl.BlockSpec(memory_space=pl.ANY),
            pl.BlockSpec(memory_space=pl.ANY),
            pl.BlockSpec(memory_space=pl.ANY),
        ],
        out_specs=pl.BlockSpec((rows_c, LANES), lambda i, *s: (i, 0)),
        scratch_shapes=[
            pltpu.VMEM((2, d, de), F32), pltpu.VMEM((2, d, de), F32), pltpu.VMEM((2, de, d), F32),
            pltpu.VMEM((d, de), BF16), pltpu.VMEM((d, de), BF16), pltpu.VMEM((de, d), BF16),
            pltpu.SemaphoreType.DMA((2,)),
        ],
    )
    return pl.pallas_call(
        functools.partial(_expert_body, layer),
        grid_spec=grid_spec,
        out_shape=jax.ShapeDtypeStruct(xb.shape, F32),
        compiler_params=_cparams(("arbitrary",)),
        name="moe_experts",
    )(blk_e, nused, ordinal, uniq, n_distinct, xb, w_gate, w_up, w_down)


def _combine_body(alpha, dest_ref, x_ref, gts_ref, gt_ref, lng_ref, lnb_ref, yb_ref, o_ref, buf, sem):
    tm, d = x_ref.shape
    nchunk = d // LANES

    def slab(ref, row):
        return ref.at[pl.ds(pl.multiple_of(row * nchunk, nchunk), nchunk)]

    def body(tok, carry):
        for k in range(2):
            pltpu.make_async_copy(slab(yb_ref, dest_ref[k, tok]), slab(buf.at[k], tok), sem).start()
        return carry
    lax.fori_loop(0, tm, body, 0, unroll=8)

    for k in range(2):
        pltpu.make_async_copy(yb_ref.at[pl.ds(0, tm * nchunk)], buf.at[k], sem).wait()

    gts = gts_ref[...]
    f = jnp.zeros((tm, d), F32)
    for k in range(2):
        yk = jnp.concatenate([buf[k, pl.ds(c, tm, stride=nchunk), :] for c in range(nchunk)], axis=1)
        f = f + gts[:, k:k + 1] * yk
    z = alpha * x_ref[...] + (1.0 + gt_ref[...]) * f
    o_ref[...] = _layer_norm(z, lng_ref[...], lnb_ref[...])


def _combine(alpha, dest_tiles, x1, gates_t, mod, ln_g, ln_b, yb):
    rows, d = x1.shape
    n_tiles, _, tm = dest_tiles.shape
    nchunk = d // LANES
    vec = pl.BlockSpec((1, d), lambda i: (0, 0))
    if mod.tiles_per_seq is not None:
        mod = _Mod(mod.arr, d, mod.tiles_per_seq * (ROW_TILE // tm))
    return pl.pallas_call(
        functools.partial(_combine_body, alpha),
        grid=(n_tiles,),
        in_specs=[
            pl.BlockSpec((None, 2, tm), lambda i: (i, 0, 0), memory_space=pltpu.SMEM),
            pl.BlockSpec((tm, d), lambda i: (i, 0)),
            pl.BlockSpec((tm, 2), lambda i: (i, 0)),
            mod.spec(5, 1), vec, vec,
            pl.BlockSpec(memory_space=pl.ANY),
        ],
        out_specs=pl.BlockSpec((tm, d), lambda i: (i, 0)),
        out_shape=jax.ShapeDtypeStruct((rows, d), F32),
        scratch_shapes=[pltpu.VMEM((2, tm * nchunk, LANES), F32), pltpu.SemaphoreType.DMA(())],
        compiler_params=_cparams(("arbitrary",)),
        name="moe_combine_postnorm",
    )(dest_tiles, x1, gates_t, mod.arr, ln_g, ln_b, yb)


def _gmlp_in_body(x_ref, sh_ref, sc_ref, w_ref, vg_ref, vb_ref, u_ref, v_ref, h_scr, v_scr):
    j = pl.program_id(1)
    half = v_scr.shape[0]

    @pl.when(j == 0)
    def _():
        h_scr[...] = (x_ref[...] * (1.0 + sc_ref[...]) + sh_ref[...]).astype(BF16)

    y = jax.nn.gelu(jnp.dot(h_scr[...], w_ref[...], preferred_element_type=F32))

    @pl.when(j < half)
    def _():
        u_ref[...] = y.astype(u_ref.dtype)

    @pl.when(j >= half)
    def _():
        v_scr[j - half] = y

    @pl.when(j == 2 * half - 1)
    def _():
        v = jnp.concatenate([v_scr[c] for c in range(half)], axis=1)
        v_ref[...] = _layer_norm(v, vg_ref[...], vb_ref[...]).astype(v_ref.dtype)


def _gmlp_in(x, mod, w_in_bf16, vn_g, vn_b, tm, v_dtype):
    rows, d = x.shape
    tn = 512
    half = d // tn
    vec = pl.BlockSpec((1, d), lambda i, j: (0, 0))
    return pl.pallas_call(
        _gmlp_in_body,
        grid=(rows // tm, 2 * half),
        in_specs=[
            pl.BlockSpec((tm, d), lambda i, j: (i, 0)),
            mod.spec(0, 2), mod.spec(1, 2),
            pl.BlockSpec((d, tn), lambda i, j: (0, j)),
            vec, vec,
        ],
        out_specs=[
            pl.BlockSpec((tm, tn), lambda i, j: (i, jnp.minimum(j, half - 1))),
            pl.BlockSpec((tm, d), lambda i, j: (i, 0)),
        ],
        out_shape=[jax.ShapeDtypeStruct((rows, d), BF16), jax.ShapeDtypeStruct((rows, d), v_dtype)],
        scratch_shapes=[pltpu.VMEM((tm, d), BF16), pltpu.VMEM((half, tm, tn), F32)],
        compiler_params=_cparams(("arbitrary", "arbitrary")),
        name="gmlp_in",
    )(x, mod.arr, mod.arr, w_in_bf16, vn_g, vn_b)


def _moe_and_norm(alpha, x1_p, x1_s, h2c_p, h2c_s, lg_p, lg_s, mod_p, mod_s, ln_g, ln_b,
                  w_gate, w_up, w_down, layer):
    t_p, d = x1_p.shape
    t_s = x1_s.shape[0]
    t = t_p + t_s
    nchunk = d // LANES
    ne = w_gate.shape[1]
    n_blocks = -(-(2 * t) // MOE_ROWS) + ne
    n_rows = n_blocks * MOE_ROWS

    eid, gates, dest, meta = _route(jnp.concatenate([lg_p, lg_s], axis=1), n_blocks)
    blk_e = meta[0, :n_blocks]
    nused = meta[1, :1]
    tail = jnp.stack([nused * MOE_ROWS, jnp.full((1,), n_rows, I32)])
    fill = jnp.concatenate([meta[2:4, :ne], tail], axis=1)

    def tiles(a, tm):
        return a.reshape(2, -1, tm).transpose(1, 0, 2)

    xb = _dispatch(fill, tiles(dest, ROW_TILE), h2c_p, h2c_s, n_rows, nchunk)
    yb = _experts(blk_e, nused, xb, w_gate, w_up, w_down, layer, n_blocks)
    gates_t = gates.T
    tm = COMBINE_TILE
    x2_p = _combine(alpha, tiles(dest[:, :t_p], tm), x1_p, gates_t[:t_p], mod_p, ln_g, ln_b, yb)
    x2_s = _combine(alpha, tiles(dest[:, t_p:], min(tm, SAMPLE_TILE)), x1_s, gates_t[t_p:], mod_s, ln_g, ln_b, yb)
    return x2_p, x2_s


def _router_weights(w_rg, b_rg, w_re, b_re):
    d = w_rg.shape[0]
    pad = jnp.zeros((SUBLANES - MOE_GROUPS, d), F32)
    w = jnp.concatenate([w_rg.T, pad, w_re.T], axis=0)
    b = jnp.concatenate([b_rg, jnp.zeros((SUBLANES - MOE_GROUPS,), F32), b_re])[:, None]
    hi = w.astype(BF16)
    lo = (w - hi.astype(F32)).astype(BF16)
    return hi, lo, b


def kernel(x_prompt, x_sample, cache_k, cache_v, page_table, c_prompt, c_sample, w_ada, b_ada, ln_g, ln_b,
           attn_w_qkv, attn_w_o, gmlp_w_in, gmlp_vn_g, gmlp_vn_b, gmlp_w_s, gmlp_b_s, gmlp_w_out,
           moe_w_rg, moe_b_rg, moe_w_re, moe_b_re, moe_w_gate, moe_w_up, moe_w_down):
    batch, seq, d = x_prompt.shape
    nseq, dec_seq, _ = x_sample.shape
    depth = w_ada.shape[0]
    dh = d // N_HEADS
    past_len = page_table.shape[1] * cache_k.shape[2]
    alpha = (2.0 * depth) ** 0.25
    tiles_per_seq = seq // ROW_TILE

    xp = x_prompt.reshape(batch * seq, d)
    xs = x_sample.transpose(1, 0, 2).reshape(dec_seq * nseq, d)

    pad_rows = (-(batch + nseq)) % SUBLANES
    c_all = jnp.concatenate([c_prompt, c_sample, jnp.zeros((pad_rows, d), F32)], axis=0)
    m_all = _adaln(c_all, w_ada, b_ada)

    tabs_p = _rope_tables(jnp.arange(seq, dtype=I32), dh)
    tabs_s = _rope_tables(past_len + jnp.repeat(jnp.arange(dec_seq, dtype=I32), nseq), dh)

    new_kp, new_vp, new_ks, new_vs, new_gv = [], [], [], [], []
    for l in range(depth):
        mod_p = _Mod(m_all[l, :batch].reshape(batch, 1, 6 * d), d, tiles_per_seq)
        mod_in = _Mod(mod_p.arr, d, seq // IN_TILE)
        mod_s = _Mod(m_all[l, batch:batch + nseq], d, None)
        lng0, lnb0 = ln_g[l, 0][None, :], ln_b[l, 0][None, :]
        lng1, lnb1 = ln_g[l, 1][None, :], ln_b[l, 1][None, :]
        wr_hi, wr_lo, br = _router_weights(moe_w_rg[l], moe_b_rg[l], moe_w_re[l], moe_b_re[l])
        post = functools.partial(_post, alpha)

        if l % 2 == 0:
            la = l // 2
            w_qkv = attn_w_qkv[la].astype(BF16)
            w_o = attn_w_o[la].astype(BF16)
            q_p, k_p, v_p = _qkv(xp, mod_in, w_qkv, tabs_p, lambda i: i % (seq // IN_TILE), IN_TILE, QKV_COLS, BF16)
            q_s, k_s, v_s = _qkv(xs, mod_s, w_qkv, tabs_s, lambda i: i, SAMPLE_TILE, QKV_COLS, F32)
            o_p = _attn_prompt(q_p, k_p, v_p, batch, seq)

            ngrp = N_HEADS // HEAD_GROUP

            def seq_rows(a):
                a = a.reshape(dec_seq, nseq, ngrp, HEAD_GROUP, dh).transpose(1, 2, 0, 3, 4)
                return a.reshape(nseq, ngrp, dec_seq * HEAD_GROUP, dh)

            o_s = _attn_sample(seq_rows(q_s), seq_rows(k_s), seq_rows(v_s), cache_k, cache_v, la, page_table)
            o_s = o_s.reshape(nseq, ngrp, dec_seq, HEAD_GROUP, dh).transpose(2, 0, 1, 3, 4)
            o_s = o_s.reshape(dec_seq * nseq, d).astype(BF16)
            new_kp.append(k_p.reshape(batch, seq, N_HEADS, dh))
            new_vp.append(v_p.reshape(batch, seq, N_HEADS, dh))
            new_ks.append(k_s.reshape(dec_seq, nseq, N_HEADS, dh).transpose(1, 0, 2, 3))
            new_vs.append(v_s.reshape(dec_seq, nseq, N_HEADS, dh).transpose(1, 0, 2, 3))
            x1_p, h2c_p, lg_p = post(_pro_identity, (o_p,), [pl.BlockSpec((ROW_TILE, d), lambda i: (i, 0))],
                                     xp, w_o, mod_p, lng0, lnb0, wr_hi, wr_lo, br, ROW_TILE)
            x1_s, h2c_s, lg_s = post(_pro_identity, (o_s,), [pl.BlockSpec((SAMPLE_TILE, d), lambda i: (i, 0))],
                                     xs, w_o, mod_s, lng0, lnb0, wr_hi, wr_lo, br, SAMPLE_TILE)
        else:
            lb = l // 2
            w_in = gmlp_w_in[lb].astype(BF16)
            w_out = gmlp_w_out[lb].astype(BF16)
            vg, vb = gmlp_vn_g[lb][None, :], gmlp_vn_b[lb][None, :]
            u_p, vn_p = _gmlp_in(xp, mod_in, w_in, vg, vb, IN_TILE, BF16)
            u_s, vn_s = _gmlp_in(xs, mod_s, w_in, vg, vb, SAMPLE_TILE, F32)
            new_gv.append(vn_s.reshape(dec_seq, nseq, d).transpose(1, 0, 2))
            gw = d // GMLP_GROUPS
            bexp = jnp.repeat(gmlp_b_s[lb].T, gw, axis=1)
            wexp = jnp.repeat(gmlp_w_s[lb][:, :dec_seq, :dec_seq].transpose(1, 2, 0).reshape(dec_seq * dec_seq, -1),
                              gw, axis=1)
            tile = pl.BlockSpec((ROW_TILE, d), lambda i: (i, 0))
            x1_p, h2c_p, lg_p = post(
                _pro_gmlp_prompt, (u_p, vn_p, gmlp_w_s[lb], bexp),
                [tile, tile, pl.BlockSpec(gmlp_w_s[lb].shape, lambda i: (0, 0, 0)),
                 pl.BlockSpec(bexp.shape, lambda i: (0, 0))],
                xp, w_out, mod_p, lng0, lnb0, wr_hi, wr_lo, br, ROW_TILE)
            x1_s, h2c_s, lg_s = post(
                _pro_gmlp_sample, (u_s, vn_s, wexp, bexp[:SUBLANES]),
                [pl.BlockSpec((SAMPLE_TILE, d), lambda i: (i, 0)),
                 pl.BlockSpec(vn_s.shape, lambda i: (0, 0)),
                 pl.BlockSpec(wexp.shape, lambda i: (0, 0)),
                 pl.BlockSpec((SUBLANES, d), lambda i: (0, 0))],
                xs, w_out, mod_s, lng0, lnb0, wr_hi, wr_lo, br, SAMPLE_TILE)

        xp, xs = _moe_and_norm(alpha, x1_p, x1_s, h2c_p, h2c_s, lg_p, lg_s, mod_p, mod_s, lng1, lnb1,
                               moe_w_gate, moe_w_up, moe_w_down, l)

    y_p = xp.reshape(batch, seq, d)
    y_s = xs.reshape(dec_seq, nseq, d).transpose(1, 0, 2)
    return (y_p, y_s, jnp.stack(new_kp), jnp.stack(new_vp), jnp.stack(new_ks), jnp.stack(new_vs),
            jnp.stack(new_gv))
```

```python
import functools

import jax
import jax.numpy as jnp
from jax import lax
from jax.experimental import pallas as pl
from jax.experimental.pallas import tpu as pltpu

F32, BF16, I32 = jnp.float32, jnp.bfloat16, jnp.int32

N_HEADS = 16
MOBA_BLOCK = 256
MOBA_TOPK = 3
ROPE_THETA = 500000.0
GMLP_CHUNK = 128
GMLP_GROUPS = 8
MOE_GROUPS = 4
MOE_EXPERTS_PER_GROUP = 8
LN_EPS = 1e-5

LANES = 128
SUBLANES = 8
VMEM_LIMIT = 56 * 1024 * 1024

ROW_TILE = 512
IN_TILE = 1024
QKV_COLS = 512
GMLP_IN_COLS = 512
SAMPLE_TILE = 128
MOE_ROWS = 128
COMBINE_TILE = 256
PAGES_PER_STEP = 8

NT_DIMS = (((1,), (1,)), ((), ()))


def _cparams(sem):
    return pltpu.CompilerParams(dimension_semantics=sem, vmem_limit_bytes=VMEM_LIMIT)


def _layer_norm(z, g, b):
    mu = jnp.mean(z, axis=-1, keepdims=True)
    zc = z - mu
    var = jnp.mean(zc * zc, axis=-1, keepdims=True)
    return zc * lax.rsqrt(var + LN_EPS) * g + b


def _adaln_body(c_ref, w_ref, b_ref, o_ref):
    a = jax.nn.silu(c_ref[...]).astype(BF16)
    o_ref[...] = jnp.dot(a, w_ref[...].astype(BF16), preferred_element_type=F32) + b_ref[...]


def _adaln(c_all, w_ada, b_ada):
    depth, d, n = w_ada.shape
    rows = c_all.shape[0]
    tn = 1024
    return pl.pallas_call(
        _adaln_body,
        grid=(depth, n // tn),
        in_specs=[
            pl.BlockSpec((rows, d), lambda l, j: (0, 0)),
            pl.BlockSpec((None, d, tn), lambda l, j: (l, 0, j)),
            pl.BlockSpec((None, 1, tn), lambda l, j: (l, 0, j)),
        ],
        out_specs=pl.BlockSpec((None, rows, tn), lambda l, j: (l, 0, j)),
        out_shape=jax.ShapeDtypeStruct((depth, rows, n), F32),
        compiler_params=_cparams(("arbitrary", "arbitrary")),
        name="adaln",
    )(c_all, w_ada, b_ada.reshape(depth, 1, n))


class _Mod:
    def __init__(self, arr, d, tiles_per_seq):
        self.arr = arr
        self.d = d
        self.tiles_per_seq = tiles_per_seq

    def spec(self, k, ngrid):
        d, tps = self.d, self.tiles_per_seq
        if tps is None:
            rows = self.arr.shape[0]
            if ngrid == 1:
                return pl.BlockSpec((rows, d), lambda i: (0, k))
            if ngrid == 2:
                return pl.BlockSpec((rows, d), lambda i, j: (0, k))
            return pl.BlockSpec((rows, d), lambda i, p, j: (0, k))
        if ngrid == 1:
            return pl.BlockSpec((None, 1, d), lambda i: (i // tps, 0, k))
        if ngrid == 2:
            return pl.BlockSpec((None, 1, d), lambda i, j: (i // tps, 0, k))
        return pl.BlockSpec((None, 1, d), lambda i, p, j: (i // tps, 0, k))


def _qkv_body(x_ref, sh_ref, sc_ref, wq_ref, wk_ref, wv_ref, c_ref, s1_ref, s2_ref, q_ref, k_ref, v_ref, h_scr):
    j = pl.program_id(1)

    @pl.when(j == 0)
    def _():
        h_scr[...] = (x_ref[...] * (1.0 + sc_ref[...]) + sh_ref[...]).astype(BF16)

    tn = q_ref.shape[1]
    reps = tn // LANES
    c = jnp.tile(c_ref[...], (1, reps))
    s1 = jnp.tile(s1_ref[...], (1, reps))
    s2 = jnp.tile(s2_ref[...], (1, reps))
    half = c_ref.shape[1] // 8

    def rope(y):
        return y * c + pltpu.roll(y, tn - half, 1) * s1 + pltpu.roll(y, half, 1) * s2

    h = h_scr[...]
    q_ref[...] = rope(jnp.dot(h, wq_ref[...], preferred_element_type=F32)).astype(q_ref.dtype)
    k_ref[...] = rope(jnp.dot(h, wk_ref[...], preferred_element_type=F32))
    v_ref[...] = jnp.dot(h, wv_ref[...], preferred_element_type=F32)


def _qkv(x, mod, w_qkv_bf16, tabs, tab_index, tm, tn, q_dtype):
    rows, d = x.shape
    ncol = d // tn
    c_tab, s1_tab, s2_tab = tabs
    tab_spec = pl.BlockSpec((tm, LANES), lambda i, j: (tab_index(i), 0))
    out = pl.BlockSpec((tm, tn), lambda i, j: (i, j))

    def w_spec(part):
        return pl.BlockSpec((None, d, tn), lambda i, j: (part * ncol + j, 0, 0))

    return pl.pallas_call(
        _qkv_body,
        grid=(rows // tm, ncol),
        in_specs=[
            pl.BlockSpec((tm, d), lambda i, j: (i, 0)),
            mod.spec(0, 2),
            mod.spec(1, 2),
            w_spec(0), w_spec(1), w_spec(2),
            tab_spec, tab_spec, tab_spec,
        ],
        out_specs=[out, out, out],
        out_shape=[
            jax.ShapeDtypeStruct((rows, d), q_dtype),
            jax.ShapeDtypeStruct((rows, d), F32),
            jax.ShapeDtypeStruct((rows, d), F32),
        ],
        scratch_shapes=[pltpu.VMEM((tm, d), BF16)],
        compiler_params=_cparams(("arbitrary", "arbitrary")),
        name="qkv_rope",
    )(x, mod.arr, mod.arr, w_qkv_bf16, w_qkv_bf16, w_qkv_bf16, c_tab, s1_tab, s2_tab)


def _rope_tables(pos, head_dim):
    rot = head_dim // 4
    half = rot // 2
    inv = ROPE_THETA ** (-jnp.arange(half, dtype=F32) * 2.0 / rot)
    ang = pos.astype(F32)[:, None] * inv[None, :]
    cos, sin = jnp.cos(ang), jnp.sin(ang)
    n = pos.shape[0]
    ones = jnp.ones((n, head_dim - rot), F32)
    zeros = jnp.zeros((n, head_dim - rot), F32)
    zh = jnp.zeros((n, half), F32)
    c = jnp.concatenate([cos, cos, ones], axis=1)
    s1 = jnp.concatenate([-sin, zh, zeros], axis=1)
    s2 = jnp.concatenate([zh, sin, zeros], axis=1)
    return c, s1, s2


def _attn_prompt_body(q_ref, k_ref, v_ref, o_ref, kb_scr, vb_scr, sel_scr):
    seq, dh = q_ref.shape
    nblk = seq // MOBA_BLOCK
    scale = dh ** -0.5
    kf = k_ref[...]
    kb_scr[...] = kf.astype(BF16)
    vb_scr[:, :dh] = v_ref[...].astype(BF16)
    vb_scr[:, dh:] = jnp.ones((seq, dh), BF16)

    km = kf.reshape(nblk, MOBA_BLOCK, dh).sum(axis=1) * (1.0 / MOBA_BLOCK)
    g = lax.dot_general(km, q_ref[...].astype(F32), NT_DIMS, precision=lax.Precision.HIGHEST,
                        preferred_element_type=F32)
    blk = lax.broadcasted_iota(I32, (nblk, seq), 0)
    cur = lax.broadcasted_iota(I32, (nblk, seq), 1) // MOBA_BLOCK
    past = blk < cur
    g = jnp.where(past, g, -jnp.inf)
    cnt = jnp.zeros((nblk, seq), F32)
    for k in range(nblk):
        gk = g[k:k + 1, :]
        beats = (gk > g) | ((gk == g) & (k < blk))
        cnt = cnt + beats.astype(F32)
    sel = ((cnt < MOBA_TOPK) & past).astype(F32)
    sel_scr[...] = jnp.concatenate([sel, jnp.zeros((LANES - nblk, seq), F32)], axis=0).T

    row = lax.broadcasted_iota(I32, (MOBA_BLOCK, MOBA_BLOCK), 0)
    col = lax.broadcasted_iota(I32, (MOBA_BLOCK, MOBA_BLOCK), 1)
    causal = row >= col

    for c in range(nblk):
        lo, hi = c * MOBA_BLOCK, (c + 1) * MOBA_BLOCK
        qc = q_ref[lo:hi, :]
        selc = sel_scr[lo:hi, :]
        slabs = []
        for j in range(c + 1):
            s = lax.dot_general(qc, kb_scr[j * MOBA_BLOCK:(j + 1) * MOBA_BLOCK, :], NT_DIMS,
                                preferred_element_type=F32) * scale
            if j < c:
                s = jnp.where(selc[:, j:j + 1] > 0.5, s, -jnp.inf)
            else:
                s = jnp.where(causal, s, -jnp.inf)
            slabs.append(s)
        mx = slabs[0]
        for s in slabs[1:]:
            mx = jnp.maximum(mx, s)
        m = mx.max(axis=1, keepdims=True)
        acc = jnp.zeros((MOBA_BLOCK, 2 * dh), F32)
        for j, s in enumerate(slabs):
            p = jnp.exp(s - m)
            acc = acc + jnp.dot(p.astype(BF16), vb_scr[j * MOBA_BLOCK:(j + 1) * MOBA_BLOCK, :],
                                preferred_element_type=F32)
        o_ref[lo:hi, :] = (acc[:, :dh] / acc[:, dh:]).astype(o_ref.dtype)


def _attn_prompt(q, k, v, batch, seq):
    rows, d = q.shape
    dh = d // N_HEADS
    spec = pl.BlockSpec((seq, dh), lambda b, h: (b, h))
    return pl.pallas_call(
        _attn_prompt_body,
        grid=(batch, N_HEADS),
        in_specs=[spec, spec, spec],
        out_specs=spec,
        out_shape=jax.ShapeDtypeStruct((rows, d), BF16),
        scratch_shapes=[pltpu.VMEM((seq, dh), BF16), pltpu.VMEM((seq, 2 * dh), BF16), pltpu.VMEM((seq, LANES), F32)],
        compiler_params=_cparams(("arbitrary", "arbitrary")),
        name="moba_prompt",
    )(q, k, v)


HEAD_GROUP = SUBLANES


def _attn_sample_body(pt_ref, q_ref, kn_ref, vn_ref, hm_ref, *rest):
    npg = PAGES_PER_STEP
    k_refs = rest[:npg]
    v_refs = rest[npg:2 * npg]
    o_ref = rest[2 * npg]
    s_scr, p_scr, ksum_scr, bmax_scr, acc_scr, linv_scr = rest[2 * npg + 1:]

    t = pl.program_id(1)
    n_pages, ngrp, nq, grp_rows = s_scr.shape
    hg = HEAD_GROUP
    k_steps = n_pages // npg
    dh = q_ref.shape[2]
    page = grp_rows // hg
    pages_per_blk = MOBA_BLOCK // page
    blk_per_step = npg // pages_per_blk
    nblk = n_pages // pages_per_blk
    scale = dh ** -0.5

    @pl.when(t < k_steps)
    def _():
        hm = hm_ref[...]
        sums = []
        pmax = [[] for _ in range(ngrp)]
        for i in range(npg):
            kp = k_refs[i][...]
            sums.append(jnp.sum(kp, axis=0))
            for g in range(ngrp):
                kb = kp[:, g * hg:(g + 1) * hg, :].reshape(grp_rows, dh).astype(BF16)
                s = lax.dot_general(q_ref[g].astype(BF16), kb, NT_DIMS, preferred_element_type=F32) * scale + hm
                s_scr[t * npg + i, g] = s
                pmax[g].append(s.max(axis=1, keepdims=True))
        for b in range(blk_per_step):
            j = t * blk_per_step + b
            tot = sums[b * pages_per_blk]
            for i in range(1, pages_per_blk):
                tot = tot + sums[b * pages_per_blk + i]
            ksum_scr[j] = tot
            for g in range(ngrp):
                bm = pmax[g][b * pages_per_blk]
                for i in range(1, pages_per_blk):
                    bm = jnp.maximum(bm, pmax[g][b * pages_per_blk + i])
                bmax_scr[j, g] = jnp.broadcast_to(bm, (nq, LANES))

    @pl.when(t == k_steps - 1)
    def _():
        gw = nblk * hg
        width = ngrp * gw
        km = jnp.concatenate(
            [ksum_scr[:, g * hg:(g + 1) * hg, :].reshape(gw, dh) for g in range(ngrp)], axis=0) * (1.0 / MOBA_BLOCK)
        qall = jnp.concatenate([q_ref[g] for g in range(ngrp)], axis=0)
        gate = lax.dot_general(qall, km, NT_DIMS, precision=lax.Precision.HIGHEST, preferred_element_type=F32)
        row = lax.broadcasted_iota(I32, (ngrp * nq, width), 0)
        lane = lax.broadcasted_iota(I32, (ngrp * nq, width), 1)
        own = ((row // nq) == (lane // gw)) & ((row % hg) == (lane % hg))
        blk = (lane % gw) // hg
        gate = jnp.where(own, gate, -jnp.inf)
        cnt = jnp.zeros((ngrp * nq, width), F32)
        for dlt in range(1, nblk):
            lower = pltpu.roll(gate, dlt * hg, 1)
            cnt = cnt + ((blk >= dlt) & (lower >= gate)).astype(F32)
            upper = pltpu.roll(gate, width - dlt * hg, 1)
            cnt = cnt + ((blk + dlt < nblk) & (upper > gate)).astype(F32)
        sel = jnp.where(own & (cnt < MOBA_TOPK), 1.0, 0.0)

        r = lax.broadcasted_iota(I32, (nq, nq), 0)
        c = lax.broadcasted_iota(I32, (nq, nq), 1)
        new_ok = ((r % hg) == (c % hg)) & ((c // hg) <= (r // hg))
        for g in range(ngrp):
            qb = q_ref[g].astype(BF16)
            sn = lax.dot_general(qb, kn_ref[g].astype(BF16), NT_DIMS, preferred_element_type=F32) * scale
            sn = jnp.where(new_ok, sn, -jnp.inf)
            m = sn.max(axis=1, keepdims=True)
            biases = []
            for j in range(nblk):
                lo = g * gw + j * hg
                picked = sel[g * nq:(g + 1) * nq, lo:lo + hg].max(axis=1, keepdims=True)
                bias = jnp.where(picked > 0.5, 0.0, -jnp.inf)
                biases.append(bias)
                m = jnp.maximum(m, bmax_scr[j, g][:, 0:1] + bias)
            en = jnp.exp(sn - m)
            l = en.sum(axis=1, keepdims=True)
            for pg in range(n_pages):
                e = jnp.exp(s_scr[pg, g] + (biases[pg // pages_per_blk] - m))
                l = l + e.sum(axis=1, keepdims=True)
                p_scr[pg, g] = e.astype(BF16)
            linv_scr[g] = jnp.broadcast_to(1.0 / l, (nq, dh))
            acc_scr[g] = jnp.dot(en.astype(BF16), vn_ref[g].astype(BF16), preferred_element_type=F32)

    @pl.when(t >= k_steps)
    def _():
        accs = [acc_scr[g] for g in range(ngrp)]
        for i in range(npg):
            vp = v_refs[i][...]
            for g in range(ngrp):
                vb = vp[:, g * hg:(g + 1) * hg, :].reshape(grp_rows, dh).astype(BF16)
                accs[g] = accs[g] + jnp.dot(p_scr[(t - k_steps) * npg + i, g], vb, preferred_element_type=F32)
        for g in range(ngrp):
            acc_scr[g] = accs[g]

    @pl.when(t == 2 * k_steps - 1)
    def _():
        o_ref[...] = acc_scr[...] * linv_scr[...]


def _attn_sample(q_rows, k_new, v_new, cache_k, cache_v, layer, page_table):
    nseq, ngrp, nq, dh = q_rows.shape
    _, n_phys, page, heads, _ = cache_k.shape
    n_pages = page_table.shape[1]
    npg = PAGES_PER_STEP
    k_steps = n_pages // npg
    grp_rows = page * HEAD_GROUP
    rowh = jnp.arange(nq, dtype=I32)[:, None] % HEAD_GROUP
    colh = jnp.arange(grp_rows, dtype=I32)[None, :] % HEAD_GROUP
    head_mask = jnp.where(rowh == colh, 0.0, -jnp.inf).astype(F32)

    def kmap(i):
        return lambda b, t, pt: (layer, pt[b * n_pages + jnp.minimum(t, k_steps - 1) * npg + i], 0, 0, 0)

    def vmap_(i):
        def index(b, t, pt):
            in_v = t >= k_steps
            seq_i = jnp.where(in_v, b, jnp.maximum(b - 1, 0))
            step = jnp.where(in_v, t - k_steps, k_steps - 1)
            return (layer, pt[seq_i * n_pages + step * npg + i], 0, 0, 0)
        return index

    seq_spec = pl.BlockSpec((None, ngrp, nq, dh), lambda b, t, pt: (b, 0, 0, 0))
    page_block = (None, None, page, heads, dh)
    nblk = n_pages * page // MOBA_BLOCK
    grid_spec = pltpu.PrefetchScalarGridSpec(
        num_scalar_prefetch=1,
        grid=(nseq, 2 * k_steps),
        in_specs=[seq_spec, seq_spec, seq_spec,
                  pl.BlockSpec((nq, grp_rows), lambda b, t, pt: (0, 0))]
                 + [pl.BlockSpec(page_block, kmap(i)) for i in range(npg)]
                 + [pl.BlockSpec(page_block, vmap_(i)) for i in range(npg)],
        out_specs=seq_spec,
        scratch_shapes=[
            pltpu.VMEM((n_pages, ngrp, nq, grp_rows), F32),
            pltpu.VMEM((n_pages, ngrp, nq, grp_rows), BF16),
            pltpu.VMEM((nblk, heads, dh), F32),
            pltpu.VMEM((nblk, ngrp, nq, LANES), F32),
            pltpu.VMEM((ngrp, nq, dh), F32),
            pltpu.VMEM((ngrp, nq, dh), F32),
        ],
    )
    return pl.pallas_call(
        _attn_sample_body,
        grid_spec=grid_spec,
        out_shape=jax.ShapeDtypeStruct((nseq, ngrp, nq, dh), F32),
        compiler_params=_cparams(("arbitrary", "arbitrary")),
        name="moba_sample",
    )(page_table.reshape(-1), q_rows, k_new, v_new, head_mask,
      *([cache_k] * npg), *([cache_v] * npg))


def _post_body(alpha, prologue, n_pro, *refs):
    pro_refs = refs[:n_pro]
    (x_ref, w_ref, gt_ref, lng_ref, lnb_ref, sh_ref, sc_ref, wrh_ref, wrl_ref, br_ref,
     x1_ref, h2c_ref, lg_ref) = refs[n_pro:]
    a = prologue(*pro_refs)
    y = jnp.dot(a, w_ref[...], preferred_element_type=F32)
    z = alpha * x_ref[...] + (1.0 + gt_ref[...]) * y
    x1 = _layer_norm(z, lng_ref[...], lnb_ref[...])
    x1_ref[...] = x1
    h2 = x1 * (1.0 + sc_ref[...]) + sh_ref[...]
    tm, d = h2.shape
    hi = h2.astype(BF16)
    hi32 = hi.astype(F32)
    bits = lax.bitcast_convert_type(hi32, I32)
    packed = bits[:, :d // 2] | lax.shift_right_logical(bits[:, d // 2:], 16)
    nchunk = d // (2 * LANES)
    for c in range(nchunk):
        h2c_ref[pl.ds(c, tm, stride=nchunk), :] = packed[:, c * LANES:(c + 1) * LANES]
    lo = (h2 - hi32).astype(BF16)
    wh, wl = wrh_ref[...], wrl_ref[...]
    lg = (lax.dot_general(wh, hi, NT_DIMS, preferred_element_type=F32)
          + lax.dot_general(wl, hi, NT_DIMS, preferred_element_type=F32)
          + lax.dot_general(wh, lo, NT_DIMS, preferred_element_type=F32))
    lg_ref[...] = lg + br_ref[...]


def _pro_identity(a_ref):
    return a_ref[...]


def _pro_gmlp_prompt(u_ref, v_ref, ws_ref, bexp_ref):
    tm, d = u_ref.shape
    gw = d // GMLP_GROUPS
    row = lax.broadcasted_iota(I32, (GMLP_CHUNK, GMLP_CHUNK), 0)
    col = lax.broadcasted_iota(I32, (GMLP_CHUNK, GMLP_CHUNK), 1)
    tri = row >= col
    wms = [jnp.where(tri, ws_ref[g], 0.0).astype(BF16) for g in range(GMLP_GROUPS)]
    outs = []
    for ch in range(tm // GMLP_CHUNK):
        lo, hi = ch * GMLP_CHUNK, (ch + 1) * GMLP_CHUNK
        cols = [jnp.dot(wms[g], v_ref[lo:hi, g * gw:(g + 1) * gw], preferred_element_type=F32)
                for g in range(GMLP_GROUPS)]
        mixed = jnp.concatenate(cols, axis=1) + bexp_ref[...]
        outs.append((u_ref[lo:hi, :].astype(F32) * mixed).astype(BF16))
    return jnp.concatenate(outs, axis=0)


def _pro_gmlp_sample(u_ref, v_ref, wexp_ref, bexp_ref):
    s = pl.program_id(0)
    tm = u_ref.shape[0]
    n_pos = v_ref.shape[0] // tm
    mixed = bexp_ref[pl.ds(s, 1), :]
    for j in range(n_pos):
        w = wexp_ref[pl.ds(s * n_pos + j, 1), :] * (j <= s).astype(F32)
        mixed = mixed + w * v_ref[j * tm:(j + 1) * tm, :]
    return (u_ref[...].astype(F32) * mixed).astype(BF16)


def _post(alpha, prologue, pro_args, pro_specs, x, w_bf16, mod, ln_g, ln_b, wr_hi, wr_lo, br, tm):
    rows, d = x.shape
    ne = wr_hi.shape[0]
    nchunk = d // (2 * LANES)
    vec = pl.BlockSpec((1, d), lambda i: (0, 0))
    return pl.pallas_call(
        functools.partial(_post_body, alpha, prologue, len(pro_args)),
        grid=(rows // tm,),
        in_specs=list(pro_specs) + [
            pl.BlockSpec((tm, d), lambda i: (i, 0)),
            pl.BlockSpec(w_bf16.shape, lambda i: (0, 0)),
            mod.spec(2, 1), vec, vec, mod.spec(3, 1), mod.spec(4, 1),
            pl.BlockSpec((ne, d), lambda i: (0, 0)),
            pl.BlockSpec((ne, d), lambda i: (0, 0)),
            pl.BlockSpec((ne, 1), lambda i: (0, 0)),
        ],
        out_specs=[
            pl.BlockSpec((tm, d), lambda i: (i, 0)),
            pl.BlockSpec((tm * nchunk, LANES), lambda i: (i, 0)),
            pl.BlockSpec((ne, tm), lambda i: (0, i)),
        ],
        out_shape=[
            jax.ShapeDtypeStruct((rows, d), F32),
            jax.ShapeDtypeStruct((rows * nchunk, LANES), I32),
            jax.ShapeDtypeStruct((ne, rows), F32),
        ],
        compiler_params=_cparams(("arbitrary",)),
        name="proj_postnorm_router",
    )(*pro_args, x, w_bf16, mod.arr, ln_g, ln_b, mod.arr, mod.arr, wr_hi, wr_lo, br)


def _route_body(lg_ref, eid_ref, gate_ref, dest_ref, meta_ref):
    ne = MOE_GROUPS * MOE_EXPERTS_PER_GROUP
    epg = MOE_EXPERTS_PER_GROUP
    lgt = lg_ref[...]
    t = lgt.shape[1]
    row8 = lax.broadcasted_iota(I32, (SUBLANES, t), 0)
    lgp = jnp.where(row8 < MOE_GROUPS, lgt[0:SUBLANES], -jnp.inf)
    mg = lgp.max(axis=0, keepdims=True)
    grp = jnp.where(lgp == mg, row8, SUBLANES).min(axis=0, keepdims=True)
    pg = 1.0 / jnp.exp(lgp - mg).sum(axis=0, keepdims=True)

    le = jnp.zeros((epg, t), F32)
    for g in range(MOE_GROUPS):
        le = jnp.where(grp == g, lgt[SUBLANES + g * epg:SUBLANES + (g + 1) * epg], le)
    m1 = le.max(axis=0, keepdims=True)
    i1 = jnp.where(le == m1, row8, epg).min(axis=0, keepdims=True)
    le2 = jnp.where(row8 == i1, -jnp.inf, le)
    m2 = le2.max(axis=0, keepdims=True)
    i2 = jnp.where(le2 == m2, row8, epg).min(axis=0, keepdims=True)
    e = jnp.exp(m2 - m1)
    g1 = pg / (1.0 + e)
    g2 = pg * e / (1.0 + e)
    e1 = grp * epg + i1
    e2 = grp * epg + i2
    eid_ref[0:1, :] = e1
    eid_ref[1:2, :] = e2
    gate_ref[0:1, :] = g1
    gate_ref[1:2, :] = g2

    rows = lax.broadcasted_iota(I32, (ne, t), 0)
    oh1 = (rows == e1).astype(F32)
    oh2 = (rows == e2).astype(F32)
    oh = (oh1 + oh2).astype(BF16)
    cw = 512
    ur = lax.broadcasted_iota(I32, (cw, cw), 0)
    uc = lax.broadcasted_iota(I32, (cw, cw), 1)
    upper = (ur < uc).astype(BF16)
    carry = jnp.zeros((ne, 1), F32)
    pref = []
    for ci in range(t // cw):
        ohc = oh[:, ci * cw:(ci + 1) * cw]
        pref.append(jnp.dot(ohc, upper, preferred_element_type=F32) + carry)
        carry = carry + ohc.astype(F32).sum(axis=1, keepdims=True)
    cnt_before = jnp.concatenate(pref, axis=1)

    counts = carry
    shift = MOE_ROWS.bit_length() - 1
    padded = (((counts.astype(I32) + (MOE_ROWS - 1)) >> shift) << shift).astype(F32)
    lr = lax.broadcasted_iota(I32, (ne, ne), 0)
    lc = lax.broadcasted_iota(I32, (ne, ne), 1)
    lower = (lc < lr).astype(F32)
    pstart = jnp.dot(lower, jnp.broadcast_to(padded, (ne, LANES)), precision=lax.Precision.HIGHEST,
                     preferred_element_type=F32)[:, 0:1]
    pend = pstart + padded
    base = pstart + cnt_before
    d1 = (oh1 * base).sum(axis=0, keepdims=True)
    d2 = (oh2 * base).sum(axis=0, keepdims=True)
    dest_ref[0:1, :] = d1.astype(I32)
    dest_ref[1:2, :] = d2.astype(I32)

    mw = meta_ref.shape[1]
    blk_start = (lax.broadcasted_iota(I32, (ne, mw), 1) * MOE_ROWS).astype(F32)
    blk_e = jnp.minimum((pend <= blk_start).astype(F32).sum(axis=0, keepdims=True), ne - 1.0)
    last = lax.broadcasted_iota(I32, (ne, mw), 0) == ne - 1
    nused = jnp.where(last, jnp.broadcast_to(pend, (ne, mw)), 0.0).sum(axis=0, keepdims=True) * (1.0 / MOE_ROWS)
    lane_e = lax.broadcasted_iota(I32, (ne, mw), 1)
    diag = lane_e == lax.broadcasted_iota(I32, (ne, mw), 0)
    fill_lo = jnp.where(diag, jnp.broadcast_to(pstart + counts, (ne, mw)), 0.0).sum(axis=0, keepdims=True)
    fill_hi = jnp.where(diag, jnp.broadcast_to(pend, (ne, mw)), 0.0).sum(axis=0, keepdims=True)
    mrow = lax.broadcasted_iota(I32, (SUBLANES, mw), 0)
    meta = jnp.where(mrow == 0, blk_e, jnp.where(mrow == 1, nused, jnp.where(mrow == 2, fill_lo, fill_hi)))
    meta_ref[...] = meta.astype(I32)


def _route(logits_t, n_blocks_max):
    ne_pad, t = logits_t.shape
    mw = 256
    assert n_blocks_max <= mw
    return pl.pallas_call(
        _route_body,
        out_shape=[
            jax.ShapeDtypeStruct((2, t), I32),
            jax.ShapeDtypeStruct((2, t), F32),
            jax.ShapeDtypeStruct((2, t), I32),
            jax.ShapeDtypeStruct((SUBLANES, mw), I32),
        ],
        compiler_params=pltpu.CompilerParams(vmem_limit_bytes=VMEM_LIMIT),
        name="moe_route",
    )(logits_t)


def _dispatch_body(nchunk, n_prompt_tiles, fill_ref, dest_ref, hp_ref, hs_ref, xb_ref, zero_scr, sem):
    i = pl.program_id(0)
    tm = dest_ref.shape[1]
    n_ranges = fill_ref.shape[1]

    def slab(ref, row):
        return ref.at[pl.ds(pl.multiple_of(row * nchunk, nchunk), nchunk)]

    def issue(src_ref):
        def body(tok, carry):
            for k in range(2):
                pltpu.make_async_copy(slab(src_ref, tok), slab(xb_ref, dest_ref[k, tok]), sem).start()
            return carry
        lax.fori_loop(0, tm, body, 0, unroll=8)

    @pl.when(i < n_prompt_tiles)
    def _():
        issue(hp_ref)

    @pl.when(i >= n_prompt_tiles)
    def _():
        issue(hs_ref)

    for _ in range(2):
        pltpu.make_async_copy(hp_ref, xb_ref.at[pl.ds(0, tm * nchunk)], sem).wait()

    @pl.when(i == 0)
    def _():
        zero_scr[...] = jnp.zeros_like(zero_scr)
        total = 0
        for e in range(n_ranges):
            lo, hi = fill_ref[0, e], fill_ref[1, e]

            def body(r, carry):
                pltpu.make_async_copy(zero_scr, slab(xb_ref, r), sem).start()
                return carry
            lax.fori_loop(lo, hi, body, 0)
            total = total + (hi - lo)

        def wbody(_, carry):
            pltpu.make_async_copy(zero_scr, slab(xb_ref, 0), sem).wait()
            return carry
        lax.fori_loop(0, total, wbody, 0)


def _dispatch(fill, dest_tiles, h2c_p, h2c_s, n_rows, nchunk):
    n_tiles, _, tm = dest_tiles.shape
    n_prompt_tiles = h2c_p.shape[0] // (nchunk * tm)
    grid_spec = pltpu.PrefetchScalarGridSpec(
        num_scalar_prefetch=1,
        grid=(n_tiles,),
        in_specs=[
            pl.BlockSpec((None, 2, tm), lambda i, f: (i, 0, 0), memory_space=pltpu.SMEM),
            pl.BlockSpec((tm * nchunk, LANES), lambda i, f: (jnp.minimum(i, n_prompt_tiles - 1), 0)),
            pl.BlockSpec((tm * nchunk, LANES), lambda i, f: (jnp.maximum(i - n_prompt_tiles, 0), 0)),
        ],
        out_specs=pl.BlockSpec(memory_space=pl.ANY),
        scratch_shapes=[pltpu.VMEM((nchunk, LANES), h2c_p.dtype), pltpu.SemaphoreType.DMA(())],
    )
    return pl.pallas_call(
        functools.partial(_dispatch_body, nchunk, n_prompt_tiles),
        grid_spec=grid_spec,
        out_shape=jax.ShapeDtypeStruct((n_rows * nchunk, LANES), h2c_p.dtype),
        compiler_params=_cparams(("arbitrary",)),
        name="moe_dispatch",
    )(fill, dest_tiles, h2c_p, h2c_s)


def _expert_body(layer, blk_ref, nused_ref, ord_ref, uniq_ref, ndist_ref, x_ref, wg_hbm, wu_hbm, wd_hbm, y_ref,
                 wg_buf, wu_buf, wd_buf, wg_scr, wu_scr, wd_scr, sem):
    i = pl.program_id(0)
    npack = x_ref.shape[0] // MOE_ROWS
    nchunk = y_ref.shape[0] // MOE_ROWS
    pairs = ((wg_hbm, wg_buf), (wu_hbm, wu_buf), (wd_hbm, wd_buf))

    def copies(n, slot):
        e = uniq_ref[n]
        return [pltpu.make_async_copy(src.at[layer, e], dst.at[slot], sem.at[slot]) for src, dst in pairs]

    @pl.when(i < nused_ref[0])
    def _():
        prev = blk_ref[jnp.maximum(i - 1, 0)]

        @pl.when(i == 0)
        def _():
            for cp in copies(0, 0):
                cp.start()

        @pl.when((i == 0) | (blk_ref[i] != prev))
        def _():
            n = ord_ref[i]
            slot = n % 2
            for cp in copies(n, slot):
                cp.wait()

            @pl.when(n + 1 < ndist_ref[0])
            def _():
                for cp in copies(n + 1, 1 - slot):
                    cp.start()

            wg_scr[...] = wg_buf[slot].astype(BF16)
            wu_scr[...] = wu_buf[slot].astype(BF16)
            wd_scr[...] = wd_buf[slot].astype(BF16)

        words = jnp.concatenate([x_ref[pl.ds(c, MOE_ROWS, stride=npack), :] for c in range(npack)], axis=1)
        x_hi = lax.bitcast_convert_type(words & jnp.int32(-65536), F32)
        x_lo = lax.bitcast_convert_type(lax.shift_left(words, 16), F32)
        x = jnp.concatenate([x_hi, x_lo], axis=1).astype(BF16)
        g = jnp.dot(x, wg_scr[...], preferred_element_type=F32)
        u = jnp.dot(x, wu_scr[...], preferred_element_type=F32)
        a = (jax.nn.silu(g) * u).astype(BF16)
        y = jnp.dot(a, wd_scr[...], preferred_element_type=F32)
        for c in range(nchunk):
            y_ref[pl.ds(c, MOE_ROWS, stride=nchunk), :] = y[:, c * LANES:(c + 1) * LANES]

    @pl.when(i >= nused_ref[0])
    def _():
        y_ref[...] = jnp.zeros_like(y_ref)


def _experts(blk_e, nused, xb, w_gate, w_up, w_down, layer, n_blocks):
    _, ne, d, de = w_gate.shape
    nchunk = d // LANES
    rows_c = MOE_ROWS * nchunk
    rows_in = MOE_ROWS * (nchunk // 2)
    n_rows = xb.shape[0] // (nchunk // 2)

    ids = jnp.arange(n_blocks, dtype=I32)
    change = (ids < nused[0]) & ((ids == 0) | (blk_e != jnp.roll(blk_e, 1)))
    ordinal = jnp.cumsum(change.astype(I32)) - 1
    n_distinct = change.astype(I32).sum()[None]
    uniq = jnp.zeros((ne + 1,), I32).at[jnp.where(change, ordinal, ne)].set(blk_e)

    def blk(i, be, nu, *_):
        return jnp.minimum(i, nu[0] - 1)

    grid_spec = pltpu.PrefetchScalarGridSpec(
        num_scalar_prefetch=5,
        grid=(n_blocks,),
        in_specs=[
            pl.BlockSpec((rows_in, LANES), lambda i, *s: (blk(i, *s), 0)),
            pl.BlockSpec(memory_space=pl.ANY),
            pl.BlockSpec(memory_space=pl.ANY),
            pl.BlockSpec(memory_space=pl.ANY),
        ],
        out_specs=pl.BlockSpec((rows_c, LANES), lambda i, *s: (i, 0)),
        scratch_shapes=[
            pltpu.VMEM((2, d, de), F32), pltpu.VMEM((2, d, de), F32), pltpu.VMEM((2, de, d), F32),
            pltpu.VMEM((d, de), BF16), pltpu.VMEM((d, de), BF16), pltpu.VMEM((de, d), BF16),
            pltpu.SemaphoreType.DMA((2,)),
        ],
    )
    return pl.pallas_call(
        functools.partial(_expert_body, layer),
        grid_spec=grid_spec,
        out_shape=jax.ShapeDtypeStruct((n_rows * nchunk, LANES), F32),
        compiler_params=_cparams(("arbitrary",)),
        name="moe_experts",
    )(blk_e, nused, ordinal, uniq, n_distinct, xb, w_gate, w_up, w_down)


def _combine_body(alpha, dest_ref, x_ref, gts_ref, gt_ref, lng_ref, lnb_ref, yb_ref, o_ref, buf, sem):
    tm, d = x_ref.shape
    nchunk = d // LANES

    def slab(ref, row):
        return ref.at[pl.ds(pl.multiple_of(row * nchunk, nchunk), nchunk)]

    def body(tok, carry):
        for k in range(2):
            pltpu.make_async_copy(slab(yb_ref, dest_ref[k, tok]), slab(buf.at[k], tok), sem).start()
        return carry
    lax.fori_loop(0, tm, body, 0, unroll=8)

    for k in range(2):
        pltpu.make_async_copy(yb_ref.at[pl.ds(0, tm * nchunk)], buf.at[k], sem).wait()

    gts = gts_ref[...]
    f = jnp.zeros((tm, d), F32)
    for k in range(2):
        yk = jnp.concatenate([buf[k, pl.ds(c, tm, stride=nchunk), :] for c in range(nchunk)], axis=1)
        f = f + gts[:, k:k + 1] * yk
    z = alpha * x_ref[...] + (1.0 + gt_ref[...]) * f
    o_ref[...] = _layer_norm(z, lng_ref[...], lnb_ref[...])


def _combine(alpha, dest_tiles, x1, gates_t, mod, ln_g, ln_b, yb):
    rows, d = x1.shape
    n_tiles, _, tm = dest_tiles.shape
    nchunk = d // LANES
    vec = pl.BlockSpec((1, d), lambda i: (0, 0))
    if mod.tiles_per_seq is not None:
        mod = _Mod(mod.arr, d, mod.tiles_per_seq * (ROW_TILE // tm))
    return pl.pallas_call(
        functools.partial(_combine_body, alpha),
        grid=(n_tiles,),
        in_specs=[
            pl.BlockSpec((None, 2, tm), lambda i: (i, 0, 0), memory_space=pltpu.SMEM),
            pl.BlockSpec((tm, d), lambda i: (i, 0)),
            pl.BlockSpec((tm, 2), lambda i: (i, 0)),
            mod.spec(5, 1), vec, vec,
            pl.BlockSpec(memory_space=pl.ANY),
        ],
        out_specs=pl.BlockSpec((tm, d), lambda i: (i, 0)),
        out_shape=jax.ShapeDtypeStruct((rows, d), F32),
        scratch_shapes=[pltpu.VMEM((2, tm * nchunk, LANES), F32), pltpu.SemaphoreType.DMA(())],
        compiler_params=_cparams(("arbitrary",)),
        name="moe_combine_postnorm",
    )(dest_tiles, x1, gates_t, mod.arr, ln_g, ln_b, yb)


def _gmlp_in_body(x_ref, sh_ref, sc_ref, w_ref, vg_ref, vb_ref, u_ref, v_ref, h_scr, v_scr):
    j = pl.program_id(1)
    half = v_scr.shape[0]

    @pl.when(j == 0)
    def _():
        h_scr[...] = (x_ref[...] * (1.0 + sc_ref[...]) + sh_ref[...]).astype(BF16)

    y = jax.nn.gelu(jnp.dot(h_scr[...], w_ref[...], preferred_element_type=F32))

    @pl.when(j < half)
    def _():
        u_ref[...] = y.astype(u_ref.dtype)

    @pl.when(j >= half)
    def _():
        v_scr[j - half] = y

    @pl.when(j == 2 * half - 1)
    def _():
        v = jnp.concatenate([v_scr[c] for c in range(half)], axis=1)
        v_ref[...] = _layer_norm(v, vg_ref[...], vb_ref[...]).astype(v_ref.dtype)


def _gmlp_in(x, mod, w_in_bf16, vn_g, vn_b, tm, v_dtype):
    rows, d = x.shape
    tn = GMLP_IN_COLS
    half = d // tn
    vec = pl.BlockSpec((1, d), lambda i, j: (0, 0))
    return pl.pallas_call(
        _gmlp_in_body,
        grid=(rows // tm, 2 * half),
        in_specs=[
            pl.BlockSpec((tm, d), lambda i, j: (i, 0)),
            mod.spec(0, 2), mod.spec(1, 2),
            pl.BlockSpec((None, d, tn), lambda i, j: (j, 0, 0)),
            vec, vec,
        ],
        out_specs=[
            pl.BlockSpec((tm, tn), lambda i, j: (i, jnp.minimum(j, half - 1))),
            pl.BlockSpec((tm, d), lambda i, j: (i, 0)),
        ],
        out_shape=[jax.ShapeDtypeStruct((rows, d), BF16), jax.ShapeDtypeStruct((rows, d), v_dtype)],
        scratch_shapes=[pltpu.VMEM((tm, d), BF16), pltpu.VMEM((half, tm, tn), F32)],
        compiler_params=_cparams(("arbitrary", "arbitrary")),
        name="gmlp_in",
    )(x, mod.arr, mod.arr, w_in_bf16, vn_g, vn_b)


def _moe_and_norm(alpha, x1_p, x1_s, h2c_p, h2c_s, lg_p, lg_s, mod_p, mod_s, ln_g, ln_b,
                  w_gate, w_up, w_down, layer):
    t_p, d = x1_p.shape
    t_s = x1_s.shape[0]
    t = t_p + t_s
    nchunk = d // LANES
    ne = w_gate.shape[1]
    n_blocks = -(-(2 * t) // MOE_ROWS) + ne
    n_rows = n_blocks * MOE_ROWS

    eid, gates, dest, meta = _route(jnp.concatenate([lg_p, lg_s], axis=1), n_blocks)
    blk_e = meta[0, :n_blocks]
    nused = meta[1, :1]
    tail = jnp.stack([nused * MOE_ROWS, jnp.full((1,), n_rows, I32)])
    fill = jnp.concatenate([meta[2:4, :ne], tail], axis=1)

    def tiles(a, tm):
        return a.reshape(2, -1, tm).transpose(1, 0, 2)

    xb = _dispatch(fill, tiles(dest, ROW_TILE), h2c_p, h2c_s, n_rows, nchunk // 2)
    yb = _experts(blk_e, nused, xb, w_gate, w_up, w_down, layer, n_blocks)
    gates_t = gates.T
    tm = COMBINE_TILE
    x2_p = _combine(alpha, tiles(dest[:, :t_p], tm), x1_p, gates_t[:t_p], mod_p, ln_g, ln_b, yb)
    x2_s = _combine(alpha, tiles(dest[:, t_p:], min(tm, SAMPLE_TILE)), x1_s, gates_t[t_p:], mod_s, ln_g, ln_b, yb)
    return x2_p, x2_s


def _tile_major(w, tn):
    k, n = w.shape
    return w.reshape(k, n // tn, tn).transpose(1, 0, 2).astype(BF16)


def _router_weights(w_rg, b_rg, w_re, b_re):
    d = w_rg.shape[0]
    pad = jnp.zeros((SUBLANES - MOE_GROUPS, d), F32)
    w = jnp.concatenate([w_rg.T, pad, w_re.T], axis=0)
    b = jnp.concatenate([b_rg, jnp.zeros((SUBLANES - MOE_GROUPS,), F32), b_re])[:, None]
    hi = w.astype(BF16)
    lo = (w - hi.astype(F32)).astype(BF16)
    return hi, lo, b


def kernel(x_prompt, x_sample, cache_k, cache_v, page_table, c_prompt, c_sample, w_ada, b_ada, ln_g, ln_b,
           attn_w_qkv, attn_w_o, gmlp_w_in, gmlp_vn_g, gmlp_vn_b, gmlp_w_s, gmlp_b_s, gmlp_w_out,
           moe_w_rg, moe_b_rg, moe_w_re, moe_b_re, moe_w_gate, moe_w_up, moe_w_down):
    batch, seq, d = x_prompt.shape
    nseq, dec_seq, _ = x_sample.shape
    depth = w_ada.shape[0]
    dh = d // N_HEADS
    past_len = page_table.shape[1] * cache_k.shape[2]
    alpha = (2.0 * depth) ** 0.25
    tiles_per_seq = seq // ROW_TILE

    xp = x_prompt.reshape(batch * seq, d)
    xs = x_sample.transpose(1, 0, 2).reshape(dec_seq * nseq, d)

    pad_rows = (-(batch + nseq)) % SUBLANES
    c_all = jnp.concatenate([c_prompt, c_sample, jnp.zeros((pad_rows, d), F32)], axis=0)
    m_all = _adaln(c_all, w_ada, b_ada)

    tabs_p = _rope_tables(jnp.arange(seq, dtype=I32), dh)
    tabs_s = _rope_tables(past_len + jnp.repeat(jnp.arange(dec_seq, dtype=I32), nseq), dh)

    new_kp, new_vp, new_ks, new_vs, new_gv = [], [], [], [], []
    for l in range(depth):
        mod_p = _Mod(m_all[l, :batch].reshape(batch, 1, 6 * d), d, tiles_per_seq)
        mod_in = _Mod(mod_p.arr, d, seq // IN_TILE)
        mod_s = _Mod(m_all[l, batch:batch + nseq], d, None)
        lng0, lnb0 = ln_g[l, 0][None, :], ln_b[l, 0][None, :]
        lng1, lnb1 = ln_g[l, 1][None, :], ln_b[l, 1][None, :]
        wr_hi, wr_lo, br = _router_weights(moe_w_rg[l], moe_b_rg[l], moe_w_re[l], moe_b_re[l])
        post = functools.partial(_post, alpha)

        if l % 2 == 0:
            la = l // 2
            w_qkv = _tile_major(attn_w_qkv[la], QKV_COLS)
            w_o = attn_w_o[la].astype(BF16)
            q_p, k_p, v_p = _qkv(xp, mod_in, w_qkv, tabs_p, lambda i: i % (seq // IN_TILE), IN_TILE, QKV_COLS, BF16)
            q_s, k_s, v_s = _qkv(xs, mod_s, w_qkv, tabs_s, lambda i: i, SAMPLE_TILE, QKV_COLS, F32)
            o_p = _attn_prompt(q_p, k_p, v_p, batch, seq)

            ngrp = N_HEADS // HEAD_GROUP

            def seq_rows(a):
                a = a.reshape(dec_seq, nseq, ngrp, HEAD_GROUP, dh).transpose(1, 2, 0, 3, 4)
                return a.reshape(nseq, ngrp, dec_seq * HEAD_GROUP, dh)

            o_s = _attn_sample(seq_rows(q_s), seq_rows(k_s), seq_rows(v_s), cache_k, cache_v, la, page_table)
            o_s = o_s.reshape(nseq, ngrp, dec_seq, HEAD_GROUP, dh).transpose(2, 0, 1, 3, 4)
            o_s = o_s.reshape(dec_seq * nseq, d).astype(BF16)
            new_kp.append(k_p.reshape(batch, seq, N_HEADS, dh))
            new_vp.append(v_p.reshape(batch, seq, N_HEADS, dh))
            new_ks.append(k_s.reshape(dec_seq, nseq, N_HEADS, dh).transpose(1, 0, 2, 3))
            new_vs.append(v_s.reshape(dec_seq, nseq, N_HEADS, dh).transpose(1, 0, 2, 3))
            x1_p, h2c_p, lg_p = post(_pro_identity, (o_p,), [pl.BlockSpec((ROW_TILE, d), lambda i: (i, 0))],
                                     xp, w_o, mod_p, lng0, lnb0, wr_hi, wr_lo, br, ROW_TILE)
            x1_s, h2c_s, lg_s = post(_pro_identity, (o_s,), [pl.BlockSpec((SAMPLE_TILE, d), lambda i: (i, 0))],
                                     xs, w_o, mod_s, lng0, lnb0, wr_hi, wr_lo, br, SAMPLE_TILE)
        else:
            lb = l // 2
            w_in = _tile_major(gmlp_w_in[lb], GMLP_IN_COLS)
            w_out = gmlp_w_out[lb].astype(BF16)
            vg, vb = gmlp_vn_g[lb][None, :], gmlp_vn_b[lb][None, :]
            u_p, vn_p = _gmlp_in(xp, mod_in, w_in, vg, vb, IN_TILE, BF16)
            u_s, vn_s = _gmlp_in(xs, mod_s, w_in, vg, vb, SAMPLE_TILE, F32)
            new_gv.append(vn_s.reshape(dec_seq, nseq, d).transpose(1, 0, 2))
            gw = d // GMLP_GROUPS
            bexp = jnp.repeat(gmlp_b_s[lb].T, gw, axis=1)
            wexp = jnp.repeat(gmlp_w_s[lb][:, :dec_seq, :dec_seq].transpose(1, 2, 0).reshape(dec_seq * dec_seq, -1),
                              gw, axis=1)
            tile = pl.BlockSpec((ROW_TILE, d), lambda i: (i, 0))
            x1_p, h2c_p, lg_p = post(
                _pro_gmlp_prompt, (u_p, vn_p, gmlp_w_s[lb], bexp),
                [tile, tile, pl.BlockSpec(gmlp_w_s[lb].shape, lambda i: (0, 0, 0)),
                 pl.BlockSpec(bexp.shape, lambda i: (0, 0))],
                xp, w_out, mod_p, lng0, lnb0, wr_hi, wr_lo, br, ROW_TILE)
            x1_s, h2c_s, lg_s = post(
                _pro_gmlp_sample, (u_s, vn_s, wexp, bexp[:SUBLANES]),
                [pl.BlockSpec((SAMPLE_TILE, d), lambda i: (i, 0)),
                 pl.BlockSpec(vn_s.shape, lambda i: (0, 0)),
                 pl.BlockSpec(wexp.shape, lambda i: (0, 0)),
                 pl.BlockSpec((SUBLANES, d), lambda i: (0, 0))],
                xs, w_out, mod_s, lng0, lnb0, wr_hi, wr_lo, br, SAMPLE_TILE)

        xp, xs = _moe_and_norm(alpha, x1_p, x1_s, h2c_p, h2c_s, lg_p, lg_s, mod_p, mod_s, lng1, lnb1,
                               moe_w_gate, moe_w_up, moe_w_down, l)

    y_p = xp.reshape(batch, seq, d)
    y_s = xs.reshape(dec_seq, nseq, d).transpose(1, 0, 2)
    return (y_p, y_s, jnp.stack(new_kp), jnp.stack(new_vp), jnp.stack(new_ks), jnp.stack(new_vs),
            jnp.stack(new_gv))
```

```python
import functools

import jax
import jax.numpy as jnp
from jax import lax
from jax.experimental import pallas as pl
from jax.experimental.pallas import tpu as pltpu

F32, BF16, I32 = jnp.float32, jnp.bfloat16, jnp.int32

N_HEADS = 16
MOBA_BLOCK = 256
MOBA_TOPK = 3
ROPE_THETA = 500000.0
GMLP_CHUNK = 128
GMLP_GROUPS = 8
MOE_GROUPS = 4
MOE_EXPERTS_PER_GROUP = 8
LN_EPS = 1e-5

LANES = 128
SUBLANES = 8
VMEM_LIMIT = 56 * 1024 * 1024

ROW_TILE = 512
IN_TILE = 1024
QKV_COLS = 512
GMLP_IN_COLS = 512
SAMPLE_TILE = 128
MOE_ROWS = 128
COMBINE_TILE = 256
PAGES_PER_STEP = 8

NT_DIMS = (((1,), (1,)), ((), ()))


def _cparams(sem):
    return pltpu.CompilerParams(dimension_semantics=sem, vmem_limit_bytes=VMEM_LIMIT)


def _layer_norm(z, g, b):
    mu = jnp.mean(z, axis=-1, keepdims=True)
    zc = z - mu
    var = jnp.mean(zc * zc, axis=-1, keepdims=True)
    return zc * lax.rsqrt(var + LN_EPS) * g + b


def _adaln_body(c_ref, w_ref, b_ref, o_ref):
    a = jax.nn.silu(c_ref[...]).astype(BF16)
    o_ref[...] = jnp.dot(a, w_ref[...].astype(BF16), preferred_element_type=F32) + b_ref[...]


def _adaln(c_all, w_ada, b_ada):
    depth, d, n = w_ada.shape
    rows = c_all.shape[0]
    tn = 1024
    return pl.pallas_call(
        _adaln_body,
        grid=(depth, n // tn),
        in_specs=[
            pl.BlockSpec((rows, d), lambda l, j: (0, 0)),
            pl.BlockSpec((None, d, tn), lambda l, j: (l, 0, j)),
            pl.BlockSpec((None, 1, tn), lambda l, j: (l, 0, j)),
        ],
        out_specs=pl.BlockSpec((None, rows, tn), lambda l, j: (l, 0, j)),
        out_shape=jax.ShapeDtypeStruct((depth, rows, n), F32),
        compiler_params=_cparams(("arbitrary", "arbitrary")),
        name="adaln",
    )(c_all, w_ada, b_ada.reshape(depth, 1, n))


class _Mod:
    def __init__(self, arr, d, tiles_per_seq):
        self.arr = arr
        self.d = d
        self.tiles_per_seq = tiles_per_seq

    def spec(self, k, ngrid):
        d, tps = self.d, self.tiles_per_seq
        if tps is None:
            rows = self.arr.shape[0]
            if ngrid == 1:
                return pl.BlockSpec((rows, d), lambda i: (0, k))
            if ngrid == 2:
                return pl.BlockSpec((rows, d), lambda i, j: (0, k))
            return pl.BlockSpec((rows, d), lambda i, p, j: (0, k))
        if ngrid == 1:
            return pl.BlockSpec((None, 1, d), lambda i: (i // tps, 0, k))
        if ngrid == 2:
            return pl.BlockSpec((None, 1, d), lambda i, j: (i // tps, 0, k))
        return pl.BlockSpec((None, 1, d), lambda i, p, j: (i // tps, 0, k))


def _qkv_body(x_ref, sh_ref, sc_ref, wq_ref, wk_ref, wv_ref, c_ref, s1_ref, s2_ref, q_ref, k_ref, v_ref, h_scr):
    j = pl.program_id(1)

    @pl.when(j == 0)
    def _():
        h_scr[...] = (x_ref[...] * (1.0 + sc_ref[...]) + sh_ref[...]).astype(BF16)

    tn = q_ref.shape[1]
    reps = tn // LANES
    c = jnp.tile(c_ref[...], (1, reps))
    s1 = jnp.tile(s1_ref[...], (1, reps))
    s2 = jnp.tile(s2_ref[...], (1, reps))
    half = c_ref.shape[1] // 8

    def rope(y):
        return y * c + pltpu.roll(y, tn - half, 1) * s1 + pltpu.roll(y, half, 1) * s2

    h = h_scr[...]
    q_ref[...] = rope(jnp.dot(h, wq_ref[...], preferred_element_type=F32)).astype(q_ref.dtype)
    k_ref[...] = rope(jnp.dot(h, wk_ref[...], preferred_element_type=F32))
    v_ref[...] = jnp.dot(h, wv_ref[...], preferred_element_type=F32)


def _qkv(x, mod, w_qkv_bf16, tabs, tab_index, tm, tn, q_dtype):
    rows, d = x.shape
    ncol = d // tn
    c_tab, s1_tab, s2_tab = tabs
    tab_spec = pl.BlockSpec((tm, LANES), lambda i, j: (tab_index(i), 0))
    out = pl.BlockSpec((tm, tn), lambda i, j: (i, j))

    def w_spec(part):
        return pl.BlockSpec((d, tn), lambda i, j: (0, part * ncol + j))

    return pl.pallas_call(
        _qkv_body,
        grid=(rows // tm, ncol),
        in_specs=[
            pl.BlockSpec((tm, d), lambda i, j: (i, 0)),
            mod.spec(0, 2),
            mod.spec(1, 2),
            w_spec(0), w_spec(1), w_spec(2),
            tab_spec, tab_spec, tab_spec,
        ],
        out_specs=[out, out, out],
        out_shape=[
            jax.ShapeDtypeStruct((rows, d), q_dtype),
            jax.ShapeDtypeStruct((rows, d), F32),
            jax.ShapeDtypeStruct((rows, d), F32),
        ],
        scratch_shapes=[pltpu.VMEM((tm, d), BF16)],
        compiler_params=_cparams(("arbitrary", "arbitrary")),
        name="qkv_rope",
    )(x, mod.arr, mod.arr, w_qkv_bf16, w_qkv_bf16, w_qkv_bf16, c_tab, s1_tab, s2_tab)


def _rope_tables(pos, head_dim):
    rot = head_dim // 4
    half = rot // 2
    inv = ROPE_THETA ** (-jnp.arange(half, dtype=F32) * 2.0 / rot)
    ang = pos.astype(F32)[:, None] * inv[None, :]
    cos, sin = jnp.cos(ang), jnp.sin(ang)
    n = pos.shape[0]
    ones = jnp.ones((n, head_dim - rot), F32)
    zeros = jnp.zeros((n, head_dim - rot), F32)
    zh = jnp.zeros((n, half), F32)
    c = jnp.concatenate([cos, cos, ones], axis=1)
    s1 = jnp.concatenate([-sin, zh, zeros], axis=1)
    s2 = jnp.concatenate([zh, sin, zeros], axis=1)
    return c, s1, s2


def _attn_prompt_body(q_ref, k_ref, v_ref, o_ref, kb_scr, vb_scr, sel_scr, qs_scr):
    seq, dh = q_ref.shape
    nblk = seq // MOBA_BLOCK
    scale = dh ** -0.5
    kf = k_ref[...]
    kb_scr[...] = kf.astype(BF16)
    vb_scr[:, :dh] = v_ref[...].astype(BF16)
    vb_scr[:, dh:] = jnp.ones((seq, dh), BF16)

    km = kf.reshape(nblk, MOBA_BLOCK, dh).sum(axis=1) * (1.0 / MOBA_BLOCK)
    g = lax.dot_general(km, q_ref[...].astype(F32), NT_DIMS, precision=lax.Precision.HIGHEST,
                        preferred_element_type=F32)
    blk = lax.broadcasted_iota(I32, (nblk, seq), 0)
    cur = lax.broadcasted_iota(I32, (nblk, seq), 1) // MOBA_BLOCK
    past = blk < cur
    g = jnp.where(past, g, -jnp.inf)
    cnt = jnp.zeros((nblk, seq), F32)
    for k in range(nblk):
        gk = g[k:k + 1, :]
        beats = (gk > g) | ((gk == g) & (k < blk))
        cnt = cnt + beats.astype(F32)
    sel = ((cnt < MOBA_TOPK) & past).astype(F32)
    sel_scr[...] = jnp.concatenate([sel, jnp.zeros((LANES - nblk, seq), F32)], axis=0).T

    row = lax.broadcasted_iota(I32, (MOBA_BLOCK, MOBA_BLOCK), 0)
    col = lax.broadcasted_iota(I32, (MOBA_BLOCK, MOBA_BLOCK), 1)
    causal = row >= col

    qs_scr[...] = (q_ref[...].astype(F32) * scale).astype(BF16)

    for c in range(nblk):
        lo, hi = c * MOBA_BLOCK, (c + 1) * MOBA_BLOCK
        qc = qs_scr[lo:hi, :]
        selc = sel_scr[lo:hi, :]
        slabs = []
        for j in range(c + 1):
            s = lax.dot_general(qc, kb_scr[j * MOBA_BLOCK:(j + 1) * MOBA_BLOCK, :], NT_DIMS,
                                preferred_element_type=F32)
            if j < c:
                s = jnp.where(selc[:, j:j + 1] > 0.5, s, -jnp.inf)
            else:
                s = jnp.where(causal, s, -jnp.inf)
            slabs.append(s)
        mx = slabs[0]
        for s in slabs[1:]:
            mx = jnp.maximum(mx, s)
        m = mx.max(axis=1, keepdims=True)
        acc = jnp.zeros((MOBA_BLOCK, 2 * dh), F32)
        for j, s in enumerate(slabs):
            p = jnp.exp(s - m)
            acc = acc + jnp.dot(p.astype(BF16), vb_scr[j * MOBA_BLOCK:(j + 1) * MOBA_BLOCK, :],
                                preferred_element_type=F32)
        o_ref[lo:hi, :] = (acc[:, :dh] / acc[:, dh:]).astype(o_ref.dtype)


def _attn_prompt(q, k, v, batch, seq):
    rows, d = q.shape
    dh = d // N_HEADS
    spec = pl.BlockSpec((seq, dh), lambda b, h: (b, h))
    return pl.pallas_call(
        _attn_prompt_body,
        grid=(batch, N_HEADS),
        in_specs=[spec, spec, spec],
        out_specs=spec,
        out_shape=jax.ShapeDtypeStruct((rows, d), BF16),
        scratch_shapes=[pltpu.VMEM((seq, dh), BF16), pltpu.VMEM((seq, 2 * dh), BF16), pltpu.VMEM((seq, LANES), F32),
                        pltpu.VMEM((seq, dh), BF16)],
        compiler_params=_cparams(("arbitrary", "arbitrary")),
        name="moba_prompt",
    )(q, k, v)


HEAD_GROUP = SUBLANES


def _attn_sample_body(pt_ref, q_ref, kn_ref, vn_ref, hm_ref, *rest):
    npg = PAGES_PER_STEP
    k_refs = rest[:npg]
    v_refs = rest[npg:2 * npg]
    o_ref = rest[2 * npg]
    s_scr, p_scr, ksum_scr, bmax_scr, acc_scr, linv_scr = rest[2 * npg + 1:]

    t = pl.program_id(1)
    n_pages, ngrp, nq, grp_rows = s_scr.shape
    hg = HEAD_GROUP
    k_steps = n_pages // npg
    dh = q_ref.shape[2]
    page = grp_rows // hg
    pages_per_blk = MOBA_BLOCK // page
    blk_per_step = npg // pages_per_blk
    nblk = n_pages // pages_per_blk
    scale = dh ** -0.5

    @pl.when(t < k_steps)
    def _():
        hm = hm_ref[...]
        sums = []
        pmax = [[] for _ in range(ngrp)]
        for i in range(npg):
            kp = k_refs[i][...]
            sums.append(jnp.sum(kp, axis=0))
            for g in range(ngrp):
                kb = kp[:, g * hg:(g + 1) * hg, :].reshape(grp_rows, dh).astype(BF16)
                s = lax.dot_general(q_ref[g].astype(BF16), kb, NT_DIMS, preferred_element_type=F32) * scale + hm
                s_scr[t * npg + i, g] = s
                pmax[g].append(s.max(axis=1, keepdims=True))
        for b in range(blk_per_step):
            j = t * blk_per_step + b
            tot = sums[b * pages_per_blk]
            for i in range(1, pages_per_blk):
                tot = tot + sums[b * pages_per_blk + i]
            ksum_scr[j] = tot
            for g in range(ngrp):
                bm = pmax[g][b * pages_per_blk]
                for i in range(1, pages_per_blk):
                    bm = jnp.maximum(bm, pmax[g][b * pages_per_blk + i])
                bmax_scr[j, g] = jnp.broadcast_to(bm, (nq, LANES))

    @pl.when(t == k_steps - 1)
    def _():
        gw = nblk * hg
        width = ngrp * gw
        km = jnp.concatenate(
            [ksum_scr[:, g * hg:(g + 1) * hg, :].reshape(gw, dh) for g in range(ngrp)], axis=0) * (1.0 / MOBA_BLOCK)
        qall = jnp.concatenate([q_ref[g] for g in range(ngrp)], axis=0)
        gate = lax.dot_general(qall, km, NT_DIMS, precision=lax.Precision.HIGHEST, preferred_element_type=F32)
        row = lax.broadcasted_iota(I32, (ngrp * nq, width), 0)
        lane = lax.broadcasted_iota(I32, (ngrp * nq, width), 1)
        own = ((row // nq) == (lane // gw)) & ((row % hg) == (lane % hg))
        blk = (lane % gw) // hg
        gate = jnp.where(own, gate, -jnp.inf)
        cnt = jnp.zeros((ngrp * nq, width), F32)
        for dlt in range(1, nblk):
            lower = pltpu.roll(gate, dlt * hg, 1)
            cnt = cnt + ((blk >= dlt) & (lower >= gate)).astype(F32)
            upper = pltpu.roll(gate, width - dlt * hg, 1)
            cnt = cnt + ((blk + dlt < nblk) & (upper > gate)).astype(F32)
        sel = jnp.where(own & (cnt < MOBA_TOPK), 1.0, 0.0)

        r = lax.broadcasted_iota(I32, (nq, nq), 0)
        c = lax.broadcasted_iota(I32, (nq, nq), 1)
        new_ok = ((r % hg) == (c % hg)) & ((c // hg) <= (r // hg))
        for g in range(ngrp):
            qb = q_ref[g].astype(BF16)
            sn = lax.dot_general(qb, kn_ref[g].astype(BF16), NT_DIMS, preferred_element_type=F32) * scale
            sn = jnp.where(new_ok, sn, -jnp.inf)
            m = sn.max(axis=1, keepdims=True)
            biases = []
            for j in range(nblk):
                lo = g * gw + j * hg
                picked = sel[g * nq:(g + 1) * nq, lo:lo + hg].max(axis=1, keepdims=True)
                bias = jnp.where(picked > 0.5, 0.0, -jnp.inf)
                biases.append(bias)
                m = jnp.maximum(m, bmax_scr[j, g][:, 0:1] + bias)
            en = jnp.exp(sn - m)
            l = en.sum(axis=1, keepdims=True)
            for pg in range(n_pages):
                e = jnp.exp(s_scr[pg, g] + (biases[pg // pages_per_blk] - m))
                l = l + e.sum(axis=1, keepdims=True)
                p_scr[pg, g] = e.astype(BF16)
            linv_scr[g] = jnp.broadcast_to(1.0 / l, (nq, dh))
            acc_scr[g] = jnp.dot(en.astype(BF16), vn_ref[g].astype(BF16), preferred_element_type=F32)

    @pl.when(t >= k_steps)
    def _():
        accs = [acc_scr[g] for g in range(ngrp)]
        for i in range(npg):
            vp = v_refs[i][...]
            for g in range(ngrp):
                vb = vp[:, g * hg:(g + 1) * hg, :].reshape(grp_rows, dh).astype(BF16)
                accs[g] = accs[g] + jnp.dot(p_scr[(t - k_steps) * npg + i, g], vb, preferred_element_type=F32)
        for g in range(ngrp):
            acc_scr[g] = accs[g]

    @pl.when(t == 2 * k_steps - 1)
    def _():
        o_ref[...] = acc_scr[...] * linv_scr[...]


def _attn_sample(q_rows, k_new, v_new, cache_k, cache_v, layer, page_table):
    nseq, ngrp, nq, dh = q_rows.shape
    _, n_phys, page, heads, _ = cache_k.shape
    n_pages = page_table.shape[1]
    npg = PAGES_PER_STEP
    k_steps = n_pages // npg
    grp_rows = page * HEAD_GROUP
    rowh = jnp.arange(nq, dtype=I32)[:, None] % HEAD_GROUP
    colh = jnp.arange(grp_rows, dtype=I32)[None, :] % HEAD_GROUP
    head_mask = jnp.where(rowh == colh, 0.0, -jnp.inf).astype(F32)

    def kmap(i):
        return lambda b, t, pt: (layer, pt[b * n_pages + jnp.minimum(t, k_steps - 1) * npg + i], 0, 0, 0)

    def vmap_(i):
        def index(b, t, pt):
            in_v = t >= k_steps
            seq_i = jnp.where(in_v, b, jnp.maximum(b - 1, 0))
            step = jnp.where(in_v, t - k_steps, k_steps - 1)
            return (layer, pt[seq_i * n_pages + step * npg + i], 0, 0, 0)
        return index

    seq_spec = pl.BlockSpec((None, ngrp, nq, dh), lambda b, t, pt: (b, 0, 0, 0))
    page_block = (None, None, page, heads, dh)
    nblk = n_pages * page // MOBA_BLOCK
    grid_spec = pltpu.PrefetchScalarGridSpec(
        num_scalar_prefetch=1,
        grid=(nseq, 2 * k_steps),
        in_specs=[seq_spec, seq_spec, seq_spec,
                  pl.BlockSpec((nq, grp_rows), lambda b, t, pt: (0, 0))]
                 + [pl.BlockSpec(page_block, kmap(i)) for i in range(npg)]
                 + [pl.BlockSpec(page_block, vmap_(i)) for i in range(npg)],
        out_specs=seq_spec,
        scratch_shapes=[
            pltpu.VMEM((n_pages, ngrp, nq, grp_rows), F32),
            pltpu.VMEM((n_pages, ngrp, nq, grp_rows), BF16),
            pltpu.VMEM((nblk, heads, dh), F32),
            pltpu.VMEM((nblk, ngrp, nq, LANES), F32),
            pltpu.VMEM((ngrp, nq, dh), F32),
            pltpu.VMEM((ngrp, nq, dh), F32),
        ],
    )
    return pl.pallas_call(
        _attn_sample_body,
        grid_spec=grid_spec,
        out_shape=jax.ShapeDtypeStruct((nseq, ngrp, nq, dh), F32),
        compiler_params=_cparams(("arbitrary", "arbitrary")),
        name="moba_sample",
    )(page_table.reshape(-1), q_rows, k_new, v_new, head_mask,
      *([cache_k] * npg), *([cache_v] * npg))


def _post_body(alpha, prologue, n_pro, *refs):
    pro_refs = refs[:n_pro]
    (x_ref, w_ref, gt_ref, lng_ref, lnb_ref, sh_ref, sc_ref, wrh_ref, wrl_ref, br_ref,
     x1_ref, h2c_ref, lg_ref) = refs[n_pro:]
    a = prologue(*pro_refs)
    y = jnp.dot(a, w_ref[...], preferred_element_type=F32)
    z = alpha * x_ref[...] + (1.0 + gt_ref[...]) * y
    x1 = _layer_norm(z, lng_ref[...], lnb_ref[...])
    x1_ref[...] = x1
    h2 = x1 * (1.0 + sc_ref[...]) + sh_ref[...]
    tm, d = h2.shape
    hi = h2.astype(BF16)
    hi32 = hi.astype(F32)
    bits = lax.bitcast_convert_type(hi32, I32)
    packed = bits[:, :d // 2] | lax.shift_right_logical(bits[:, d // 2:], 16)
    nchunk = d // (2 * LANES)
    for c in range(nchunk):
        h2c_ref[pl.ds(c, tm, stride=nchunk), :] = packed[:, c * LANES:(c + 1) * LANES]
    lo = (h2 - hi32).astype(BF16)
    wh, wl = wrh_ref[...], wrl_ref[...]
    lg = (lax.dot_general(wh, hi, NT_DIMS, preferred_element_type=F32)
          + lax.dot_general(wl, hi, NT_DIMS, preferred_element_type=F32)
          + lax.dot_general(wh, lo, NT_DIMS, preferred_element_type=F32))
    lg_ref[...] = lg + br_ref[...]


def _pro_identity(a_ref):
    return a_ref[...]


def _pro_gmlp_prompt(u_ref, v_ref, ws_ref, bexp_ref):
    tm, d = u_ref.shape
    gw = d // GMLP_GROUPS
    row = lax.broadcasted_iota(I32, (GMLP_CHUNK, GMLP_CHUNK), 0)
    col = lax.broadcasted_iota(I32, (GMLP_CHUNK, GMLP_CHUNK), 1)
    tri = row >= col
    wms = [jnp.where(tri, ws_ref[g], 0.0).astype(BF16) for g in range(GMLP_GROUPS)]
    outs = []
    for ch in range(tm // GMLP_CHUNK):
        lo, hi = ch * GMLP_CHUNK, (ch + 1) * GMLP_CHUNK
        cols = [jnp.dot(wms[g], v_ref[lo:hi, g * gw:(g + 1) * gw], preferred_element_type=F32)
                for g in range(GMLP_GROUPS)]
        mixed = jnp.concatenate(cols, axis=1) + bexp_ref[...]
        outs.append((u_ref[lo:hi, :].astype(F32) * mixed).astype(BF16))
    return jnp.concatenate(outs, axis=0)


def _pro_gmlp_sample(u_ref, v_ref, wexp_ref, bexp_ref):
    s = pl.program_id(0)
    tm = u_ref.shape[0]
    n_pos = v_ref.shape[0] // tm
    mixed = bexp_ref[pl.ds(s, 1), :]
    for j in range(n_pos):
        w = wexp_ref[pl.ds(s * n_pos + j, 1), :] * (j <= s).astype(F32)
        mixed = mixed + w * v_ref[j * tm:(j + 1) * tm, :]
    return (u_ref[...].astype(F32) * mixed).astype(BF16)


def _post(alpha, prologue, pro_args, pro_specs, x, w_bf16, mod, ln_g, ln_b, wr_hi, wr_lo, br, tm):
    rows, d = x.shape
    ne = wr_hi.shape[0]
    nchunk = d // (2 * LANES)
    vec = pl.BlockSpec((1, d), lambda i: (0, 0))
    return pl.pallas_call(
        functools.partial(_post_body, alpha, prologue, len(pro_args)),
        grid=(rows // tm,),
        in_specs=list(pro_specs) + [
            pl.BlockSpec((tm, d), lambda i: (i, 0)),
            pl.BlockSpec(w_bf16.shape, lambda i: (0, 0)),
            mod.spec(2, 1), vec, vec, mod.spec(3, 1), mod.spec(4, 1),
            pl.BlockSpec((ne, d), lambda i: (0, 0)),
            pl.BlockSpec((ne, d), lambda i: (0, 0)),
            pl.BlockSpec((ne, 1), lambda i: (0, 0)),
        ],
        out_specs=[
            pl.BlockSpec((tm, d), lambda i: (i, 0)),
            pl.BlockSpec((tm * nchunk, LANES), lambda i: (i, 0)),
            pl.BlockSpec((ne, tm), lambda i: (0, i)),
        ],
        out_shape=[
            jax.ShapeDtypeStruct((rows, d), F32),
            jax.ShapeDtypeStruct((rows * nchunk, LANES), I32),
            jax.ShapeDtypeStruct((ne, rows), F32),
        ],
        compiler_params=_cparams(("arbitrary",)),
        name="proj_postnorm_router",
    )(*pro_args, x, w_bf16, mod.arr, ln_g, ln_b, mod.arr, mod.arr, wr_hi, wr_lo, br)


def _route_body(lg_ref, eid_ref, gate_ref, dest_ref, meta_ref):
    ne = MOE_GROUPS * MOE_EXPERTS_PER_GROUP
    epg = MOE_EXPERTS_PER_GROUP
    lgt = lg_ref[...]
    t = lgt.shape[1]
    row8 = lax.broadcasted_iota(I32, (SUBLANES, t), 0)
    lgp = jnp.where(row8 < MOE_GROUPS, lgt[0:SUBLANES], -jnp.inf)
    mg = lgp.max(axis=0, keepdims=True)
    grp = jnp.where(lgp == mg, row8, SUBLANES).min(axis=0, keepdims=True)
    pg = 1.0 / jnp.exp(lgp - mg).sum(axis=0, keepdims=True)

    le = jnp.zeros((epg, t), F32)
    for g in range(MOE_GROUPS):
        le = jnp.where(grp == g, lgt[SUBLANES + g * epg:SUBLANES + (g + 1) * epg], le)
    m1 = le.max(axis=0, keepdims=True)
    i1 = jnp.where(le == m1, row8, epg).min(axis=0, keepdims=True)
    le2 = jnp.where(row8 == i1, -jnp.inf, le)
    m2 = le2.max(axis=0, keepdims=True)
    i2 = jnp.where(le2 == m2, row8, epg).min(axis=0, keepdims=True)
    e = jnp.exp(m2 - m1)
    g1 = pg / (1.0 + e)
    g2 = pg * e / (1.0 + e)
    e1 = grp * epg + i1
    e2 = grp * epg + i2
    eid_ref[0:1, :] = e1
    eid_ref[1:2, :] = e2
    gate_ref[0:1, :] = g1
    gate_ref[1:2, :] = g2

    rows = lax.broadcasted_iota(I32, (ne, t), 0)
    oh1 = (rows == e1).astype(F32)
    oh2 = (rows == e2).astype(F32)
    oh = (oh1 + oh2).astype(BF16)
    cw = 512
    ur = lax.broadcasted_iota(I32, (cw, cw), 0)
    uc = lax.broadcasted_iota(I32, (cw, cw), 1)
    upper = (ur < uc).astype(BF16)
    carry = jnp.zeros((ne, 1), F32)
    pref = []
    for ci in range(t // cw):
        ohc = oh[:, ci * cw:(ci + 1) * cw]
        pref.append(jnp.dot(ohc, upper, preferred_element_type=F32) + carry)
        carry = carry + ohc.astype(F32).sum(axis=1, keepdims=True)
    cnt_before = jnp.concatenate(pref, axis=1)

    counts = carry
    shift = MOE_ROWS.bit_length() - 1
    padded = (((counts.astype(I32) + (MOE_ROWS - 1)) >> shift) << shift).astype(F32)
    lr = lax.broadcasted_iota(I32, (ne, ne), 0)
    lc = lax.broadcasted_iota(I32, (ne, ne), 1)
    lower = (lc < lr).astype(F32)
    pstart = jnp.dot(lower, jnp.broadcast_to(padded, (ne, LANES)), precision=lax.Precision.HIGHEST,
                     preferred_element_type=F32)[:, 0:1]
    pend = pstart + padded
    base = pstart + cnt_before
    d1 = (oh1 * base).sum(axis=0, keepdims=True)
    d2 = (oh2 * base).sum(axis=0, keepdims=True)
    dest_ref[0:1, :] = d1.astype(I32)
    dest_ref[1:2, :] = d2.astype(I32)

    mw = meta_ref.shape[1]
    blk_start = (lax.broadcasted_iota(I32, (ne, mw), 1) * MOE_ROWS).astype(F32)
    blk_e = jnp.minimum((pend <= blk_start).astype(F32).sum(axis=0, keepdims=True), ne - 1.0)
    last = lax.broadcasted_iota(I32, (ne, mw), 0) == ne - 1
    nused = jnp.where(last, jnp.broadcast_to(pend, (ne, mw)), 0.0).sum(axis=0, keepdims=True) * (1.0 / MOE_ROWS)
    lane_e = lax.broadcasted_iota(I32, (ne, mw), 1)
    diag = lane_e == lax.broadcasted_iota(I32, (ne, mw), 0)
    fill_lo = jnp.where(diag, jnp.broadcast_to(pstart + counts, (ne, mw)), 0.0).sum(axis=0, keepdims=True)
    fill_hi = jnp.where(diag, jnp.broadcast_to(pend, (ne, mw)), 0.0).sum(axis=0, keepdims=True)
    mrow = lax.broadcasted_iota(I32, (SUBLANES, mw), 0)
    meta = jnp.where(mrow == 0, blk_e, jnp.where(mrow == 1, nused, jnp.where(mrow == 2, fill_lo, fill_hi)))
    meta_ref[...] = meta.astype(I32)


def _route(logits_t, n_blocks_max):
    ne_pad, t = logits_t.shape
    mw = 256
    assert n_blocks_max <= mw
    return pl.pallas_call(
        _route_body,
        out_shape=[
            jax.ShapeDtypeStruct((2, t), I32),
            jax.ShapeDtypeStruct((2, t), F32),
            jax.ShapeDtypeStruct((2, t), I32),
            jax.ShapeDtypeStruct((SUBLANES, mw), I32),
        ],
        compiler_params=pltpu.CompilerParams(vmem_limit_bytes=VMEM_LIMIT),
        name="moe_route",
    )(logits_t)


def _dispatch_body(nchunk, n_prompt_tiles, fill_ref, dest_ref, hp_ref, hs_ref, xb_ref, zero_scr, sem):
    i = pl.program_id(0)
    tm = dest_ref.shape[1]
    n_ranges = fill_ref.shape[1]

    def slab(ref, row):
        return ref.at[pl.ds(pl.multiple_of(row * nchunk, nchunk), nchunk)]

    def issue(src_ref):
        def body(tok, carry):
            for k in range(2):
                pltpu.make_async_copy(slab(src_ref, tok), slab(xb_ref, dest_ref[k, tok]), sem).start()
            return carry
        lax.fori_loop(0, tm, body, 0, unroll=8)

    @pl.when(i < n_prompt_tiles)
    def _():
        issue(hp_ref)

    @pl.when(i >= n_prompt_tiles)
    def _():
        issue(hs_ref)

    for _ in range(2):
        pltpu.make_async_copy(hp_ref, xb_ref.at[pl.ds(0, tm * nchunk)], sem).wait()

    @pl.when(i == 0)
    def _():
        zero_scr[...] = jnp.zeros_like(zero_scr)
        total = 0
        for e in range(n_ranges):
            lo, hi = fill_ref[0, e], fill_ref[1, e]

            def body(r, carry):
                pltpu.make_async_copy(zero_scr, slab(xb_ref, r), sem).start()
                return carry
            lax.fori_loop(lo, hi, body, 0)
            total = total + (hi - lo)

        def wbody(_, carry):
            pltpu.make_async_copy(zero_scr, slab(xb_ref, 0), sem).wait()
            return carry
        lax.fori_loop(0, total, wbody, 0)


def _dispatch(fill, dest_tiles, h2c_p, h2c_s, n_rows, nchunk):
    n_tiles, _, tm = dest_tiles.shape
    n_prompt_tiles = h2c_p.shape[0] // (nchunk * tm)
    grid_spec = pltpu.PrefetchScalarGridSpec(
        num_scalar_prefetch=1,
        grid=(n_tiles,),
        in_specs=[
            pl.BlockSpec((None, 2, tm), lambda i, f: (i, 0, 0), memory_space=pltpu.SMEM),
            pl.BlockSpec((tm * nchunk, LANES), lambda i, f: (jnp.minimum(i, n_prompt_tiles - 1), 0)),
            pl.BlockSpec((tm * nchunk, LANES), lambda i, f: (jnp.maximum(i - n_prompt_tiles, 0), 0)),
        ],
        out_specs=pl.BlockSpec(memory_space=pl.ANY),
        scratch_shapes=[pltpu.VMEM((nchunk, LANES), h2c_p.dtype), pltpu.SemaphoreType.DMA(())],
    )
    return pl.pallas_call(
        functools.partial(_dispatch_body, nchunk, n_prompt_tiles),
        grid_spec=grid_spec,
        out_shape=jax.ShapeDtypeStruct((n_rows * nchunk, LANES), h2c_p.dtype),
        compiler_params=_cparams(("arbitrary",)),
        name="moe_dispatch",
    )(fill, dest_tiles, h2c_p, h2c_s)


def _expert_body(layer, blk_ref, nused_ref, ord_ref, uniq_ref, ndist_ref, x_ref, wg_hbm, wu_hbm, wd_hbm, y_ref,
                 wg_buf, wu_buf, wd_buf, wg_scr, wu_scr, wd_scr, sem):
    i = pl.program_id(0)
    npack = x_ref.shape[0] // MOE_ROWS
    nchunk = y_ref.shape[0] // MOE_ROWS
    pairs = ((wg_hbm, wg_buf), (wu_hbm, wu_buf), (wd_hbm, wd_buf))

    def copies(n, slot):
        e = uniq_ref[n]
        return [pltpu.make_async_copy(src.at[layer, e], dst.at[slot], sem.at[slot]) for src, dst in pairs]

    @pl.when(i < nused_ref[0])
    def _():
        prev = blk_ref[jnp.maximum(i - 1, 0)]

        @pl.when(i == 0)
        def _():
            for cp in copies(0, 0):
                cp.start()

        @pl.when((i == 0) | (blk_ref[i] != prev))
        def _():
            n = ord_ref[i]
            slot = n % 2
            for cp in copies(n, slot):
                cp.wait()

            @pl.when(n + 1 < ndist_ref[0])
            def _():
                for cp in copies(n + 1, 1 - slot):
                    cp.start()

            wg_scr[...] = wg_buf[slot].astype(BF16)
            wu_scr[...] = wu_buf[slot].astype(BF16)
            wd_scr[...] = wd_buf[slot].astype(BF16)

        words = jnp.concatenate([x_ref[pl.ds(c, MOE_ROWS, stride=npack), :] for c in range(npack)], axis=1)
        x_hi = lax.bitcast_convert_type(words & jnp.int32(-65536), F32)
        x_lo = lax.bitcast_convert_type(lax.shift_left(words, 16), F32)
        x = jnp.concatenate([x_hi, x_lo], axis=1).astype(BF16)
        g = jnp.dot(x, wg_scr[...], preferred_element_type=F32)
        u = jnp.dot(x, wu_scr[...], preferred_element_type=F32)
        a = (jax.nn.silu(g) * u).astype(BF16)
        y = jnp.dot(a, wd_scr[...], preferred_element_type=F32)
        for c in range(nchunk):
            y_ref[pl.ds(c, MOE_ROWS, stride=nchunk), :] = y[:, c * LANES:(c + 1) * LANES]

    @pl.when(i >= nused_ref[0])
    def _():
        y_ref[...] = jnp.zeros_like(y_ref)


def _experts(blk_e, nused, xb, w_gate, w_up, w_down, layer, n_blocks):
    _, ne, d, de = w_gate.shape
    nchunk = d // LANES
    rows_c = MOE_ROWS * nchunk
    rows_in = MOE_ROWS * (nchunk // 2)
    n_rows = xb.shape[0] // (nchunk // 2)

    ids = jnp.arange(n_blocks, dtype=I32)
    change = (ids < nused[0]) & ((ids == 0) | (blk_e != jnp.roll(blk_e, 1)))
    ordinal = jnp.cumsum(change.astype(I32)) - 1
    n_distinct = change.astype(I32).sum()[None]
    uniq = jnp.zeros((ne + 1,), I32).at[jnp.where(change, ordinal, ne)].set(blk_e)

    def blk(i, be, nu, *_):
        return jnp.minimum(i, nu[0] - 1)

    grid_spec = pltpu.PrefetchScalarGridSpec(
        num_scalar_prefetch=5,
        grid=(n_blocks,),
        in_specs=[
            pl.BlockSpec((rows_in, LANES), lambda i, *s: (blk(i, *s), 0)),
            pl.BlockSpec(memory_space=pl.ANY),
            pl.BlockSpec(memory_space=pl.ANY),
            pl.BlockSpec(memory_space=pl.ANY),
        ],
        out_specs=pl.BlockSpec((rows_c, LANES), lambda i, *s: (i, 0)),
        scratch_shapes=[
            pltpu.VMEM((2, d, de), F32), pltpu.VMEM((2, d, de), F32), pltpu.VMEM((2, de, d), F32),
            pltpu.VMEM((d, de), BF16), pltpu.VMEM((d, de), BF16), pltpu.VMEM((de, d), BF16),
            pltpu.SemaphoreType.DMA((2,)),
        ],
    )
    return pl.pallas_call(
        functools.partial(_expert_body, layer),
        grid_spec=grid_spec,
        out_shape=jax.ShapeDtypeStruct((n_rows * nchunk, LANES), F32),
        compiler_params=_cparams(("arbitrary",)),
        name="moe_experts",
    )(blk_e, nused, ordinal, uniq, n_distinct, xb, w_gate, w_up, w_down)


def _combine_body(alpha, dest_ref, next_ref, x_ref, gts_ref, gt_ref, lng_ref, lnb_ref, yb_ref, o_ref, buf, sem):
    i = pl.program_id(0)
    tm, d = x_ref.shape
    nchunk = d // LANES

    def slab(ref, row):
        return ref.at[pl.ds(pl.multiple_of(row * nchunk, nchunk), nchunk)]

    def issue(idx_ref, phase):
        def body(tok, carry):
            for k in range(2):
                pltpu.make_async_copy(slab(yb_ref, idx_ref[k, tok]), slab(buf.at[phase, k], tok),
                                      sem.at[phase]).start()
            return carry
        lax.fori_loop(0, tm, body, 0, unroll=8)

    @pl.when(i == 0)
    def _():
        issue(dest_ref, 0)

    @pl.when(i + 1 < pl.num_programs(0))
    def _():
        issue(next_ref, (i + 1) % 2)

    phase = i % 2
    for k in range(2):
        pltpu.make_async_copy(yb_ref.at[pl.ds(0, tm * nchunk)], buf.at[phase, k], sem.at[phase]).wait()

    gts = gts_ref[...]
    f = jnp.zeros((tm, d), F32)
    for k in range(2):
        rows_k = buf.at[phase, k]
        yk = jnp.concatenate([rows_k[pl.ds(c, tm, stride=nchunk), :] for c in range(nchunk)], axis=1)
        f = f + gts[:, k:k + 1] * yk
    z = alpha * x_ref[...] + (1.0 + gt_ref[...]) * f
    o_ref[...] = _layer_norm(z, lng_ref[...], lnb_ref[...])


def _combine(alpha, dest_tiles, x1, gates_t, mod, ln_g, ln_b, yb):
    rows, d = x1.shape
    n_tiles, _, tm = dest_tiles.shape
    nchunk = d // LANES
    vec = pl.BlockSpec((1, d), lambda i: (0, 0))
    if mod.tiles_per_seq is not None:
        mod = _Mod(mod.arr, d, mod.tiles_per_seq * (ROW_TILE // tm))
    return pl.pallas_call(
        functools.partial(_combine_body, alpha),
        grid=(n_tiles,),
        in_specs=[
            pl.BlockSpec((None, 2, tm), lambda i: (i, 0, 0), memory_space=pltpu.SMEM),
            pl.BlockSpec((None, 2, tm), lambda i: (jnp.minimum(i + 1, n_tiles - 1), 0, 0), memory_space=pltpu.SMEM),
            pl.BlockSpec((tm, d), lambda i: (i, 0)),
            pl.BlockSpec((tm, 2), lambda i: (i, 0)),
            mod.spec(5, 1), vec, vec,
            pl.BlockSpec(memory_space=pl.ANY),
        ],
        out_specs=pl.BlockSpec((tm, d), lambda i: (i, 0)),
        out_shape=jax.ShapeDtypeStruct((rows, d), F32),
        scratch_shapes=[pltpu.VMEM((2, 2, tm * nchunk, LANES), F32), pltpu.SemaphoreType.DMA((2,))],
        compiler_params=_cparams(("arbitrary",)),
        name="moe_combine_postnorm",
    )(dest_tiles, dest_tiles, x1, gates_t, mod.arr, ln_g, ln_b, yb)


def _gmlp_in_body(x_ref, sh_ref, sc_ref, wu_ref, wv_ref, vg_ref, vb_ref, u_ref, v_ref, h_scr, v_scr):
    j = pl.program_id(1)
    half = v_scr.shape[0]

    @pl.when(j == 0)
    def _():
        h_scr[...] = (x_ref[...] * (1.0 + sc_ref[...]) + sh_ref[...]).astype(BF16)

    h = h_scr[...]
    u_ref[...] = jax.nn.gelu(jnp.dot(h, wu_ref[...], preferred_element_type=F32)).astype(u_ref.dtype)
    v_scr[j] = jax.nn.gelu(jnp.dot(h, wv_ref[...], preferred_element_type=F32))

    @pl.when(j == half - 1)
    def _():
        v = jnp.concatenate([v_scr[c] for c in range(half)], axis=1)
        v_ref[...] = _layer_norm(v, vg_ref[...], vb_ref[...]).astype(v_ref.dtype)


def _gmlp_in(x, mod, w_in_bf16, vn_g, vn_b, tm, v_dtype):
    rows, d = x.shape
    tn = GMLP_IN_COLS
    half = d // tn
    vec = pl.BlockSpec((1, d), lambda i, j: (0, 0))
    return pl.pallas_call(
        _gmlp_in_body,
        grid=(rows // tm, half),
        in_specs=[
            pl.BlockSpec((tm, d), lambda i, j: (i, 0)),
            mod.spec(0, 2), mod.spec(1, 2),
            pl.BlockSpec((d, tn), lambda i, j: (0, j)),
            pl.BlockSpec((d, tn), lambda i, j: (0, half + j)),
            vec, vec,
        ],
        out_specs=[
            pl.BlockSpec((tm, tn), lambda i, j: (i, j)),
            pl.BlockSpec((tm, d), lambda i, j: (i, 0)),
        ],
        out_shape=[jax.ShapeDtypeStruct((rows, d), BF16), jax.ShapeDtypeStruct((rows, d), v_dtype)],
        scratch_shapes=[pltpu.VMEM((tm, d), BF16), pltpu.VMEM((half, tm, tn), F32)],
        compiler_params=_cparams(("arbitrary", "arbitrary")),
        name="gmlp_in",
    )(x, mod.arr, mod.arr, w_in_bf16, w_in_bf16, vn_g, vn_b)


def _moe_and_norm(alpha, x1_p, x1_s, h2c_p, h2c_s, lg_p, lg_s, mod_p, mod_s, ln_g, ln_b,
                  w_gate, w_up, w_down, layer):
    t_p, d = x1_p.shape
    t_s = x1_s.shape[0]
    t = t_p + t_s
    nchunk = d // LANES
    ne = w_gate.shape[1]
    n_blocks = -(-(2 * t) // MOE_ROWS) + ne
    n_rows = n_blocks * MOE_ROWS

    eid, gates, dest, meta = _route(jnp.concatenate([lg_p, lg_s], axis=1), n_blocks)
    blk_e = meta[0, :n_blocks]
    nused = meta[1, :1]
    tail = jnp.stack([nused * MOE_ROWS, jnp.full((1,), n_rows, I32)])
    fill = jnp.concatenate([meta[2:4, :ne], tail], axis=1)

    def tiles(a, tm):
        return a.reshape(2, -1, tm).transpose(1, 0, 2)

    xb = _dispatch(fill, tiles(dest, ROW_TILE), h2c_p, h2c_s, n_rows, nchunk // 2)
    yb = _experts(blk_e, nused, xb, w_gate, w_up, w_down, layer, n_blocks)
    gates_t = gates.T
    tm = COMBINE_TILE
    x2_p = _combine(alpha, tiles(dest[:, :t_p], tm), x1_p, gates_t[:t_p], mod_p, ln_g, ln_b, yb)
    x2_s = _combine(alpha, tiles(dest[:, t_p:], min(tm, SAMPLE_TILE)), x1_s, gates_t[t_p:], mod_s, ln_g, ln_b, yb)
    return x2_p, x2_s


def _router_weights(w_rg, b_rg, w_re, b_re):
    d = w_rg.shape[0]
    pad = jnp.zeros((SUBLANES - MOE_GROUPS, d), F32)
    w = jnp.concatenate([w_rg.T, pad, w_re.T], axis=0)
    b = jnp.concatenate([b_rg, jnp.zeros((SUBLANES - MOE_GROUPS,), F32), b_re])[:, None]
    hi = w.astype(BF16)
    lo = (w - hi.astype(F32)).astype(BF16)
    return hi, lo, b


def kernel(x_prompt, x_sample, cache_k, cache_v, page_table, c_prompt, c_sample, w_ada, b_ada, ln_g, ln_b,
           attn_w_qkv, attn_w_o, gmlp_w_in, gmlp_vn_g, gmlp_vn_b, gmlp_w_s, gmlp_b_s, gmlp_w_out,
           moe_w_rg, moe_b_rg, moe_w_re, moe_b_re, moe_w_gate, moe_w_up, moe_w_down):
    batch, seq, d = x_prompt.shape
    nseq, dec_seq, _ = x_sample.shape
    depth = w_ada.shape[0]
    dh = d // N_HEADS
    past_len = page_table.shape[1] * cache_k.shape[2]
    alpha = (2.0 * depth) ** 0.25
    tiles_per_seq = seq // ROW_TILE

    xp = x_prompt.reshape(batch * seq, d)
    xs = x_sample.transpose(1, 0, 2).reshape(dec_seq * nseq, d)

    pad_rows = (-(batch + nseq)) % SUBLANES
    c_all = jnp.concatenate([c_prompt, c_sample, jnp.zeros((pad_rows, d), F32)], axis=0)
    m_all = _adaln(c_all, w_ada, b_ada)

    tabs_p = _rope_tables(jnp.arange(seq, dtype=I32), dh)
    tabs_s = _rope_tables(past_len + jnp.repeat(jnp.arange(dec_seq, dtype=I32), nseq), dh)

    new_kp, new_vp, new_ks, new_vs, new_gv = [], [], [], [], []
    for l in range(depth):
        mod_p = _Mod(m_all[l, :batch].reshape(batch, 1, 6 * d), d, tiles_per_seq)
        mod_in = _Mod(mod_p.arr, d, seq // IN_TILE)
        mod_s = _Mod(m_all[l, batch:batch + nseq], d, None)
        lng0, lnb0 = ln_g[l, 0][None, :], ln_b[l, 0][None, :]
        lng1, lnb1 = ln_g[l, 1][None, :], ln_b[l, 1][None, :]
        wr_hi, wr_lo, br = _router_weights(moe_w_rg[l], moe_b_rg[l], moe_w_re[l], moe_b_re[l])
        post = functools.partial(_post, alpha)

        if l % 2 == 0:
            la = l // 2
            w_qkv = attn_w_qkv[la].astype(BF16)
            w_o = attn_w_o[la].astype(BF16)
            q_p, k_p, v_p = _qkv(xp, mod_in, w_qkv, tabs_p, lambda i: i % (seq // IN_TILE), IN_TILE, QKV_COLS, BF16)
            q_s, k_s, v_s = _qkv(xs, mod_s, w_qkv, tabs_s, lambda i: i, SAMPLE_TILE, QKV_COLS, F32)
            o_p = _attn_prompt(q_p, k_p, v_p, batch, seq)

            ngrp = N_HEADS // HEAD_GROUP

            def seq_rows(a):
                a = a.reshape(dec_seq, nseq, ngrp, HEAD_GROUP, dh).transpose(1, 2, 0, 3, 4)
                return a.reshape(nseq, ngrp, dec_seq * HEAD_GROUP, dh)

            o_s = _attn_sample(seq_rows(q_s), seq_rows(k_s), seq_rows(v_s), cache_k, cache_v, la, page_table)
            o_s = o_s.reshape(nseq, ngrp, dec_seq, HEAD_GROUP, dh).transpose(2, 0, 1, 3, 4)
            o_s = o_s.reshape(dec_seq * nseq, d).astype(BF16)
            new_kp.append(k_p.reshape(batch, seq, N_HEADS, dh))
            new_vp.append(v_p.reshape(batch, seq, N_HEADS, dh))
            new_ks.append(k_s.reshape(dec_seq, nseq, N_HEADS, dh).transpose(1, 0, 2, 3))
            new_vs.append(v_s.reshape(dec_seq, nseq, N_HEADS, dh).transpose(1, 0, 2, 3))
            x1_p, h2c_p, lg_p = post(_pro_identity, (o_p,), [pl.BlockSpec((ROW_TILE, d), lambda i: (i, 0))],
                                     xp, w_o, mod_p, lng0, lnb0, wr_hi, wr_lo, br, ROW_TILE)
            x1_s, h2c_s, lg_s = post(_pro_identity, (o_s,), [pl.BlockSpec((SAMPLE_TILE, d), lambda i: (i, 0))],
                                     xs, w_o, mod_s, lng0, lnb0, wr_hi, wr_lo, br, SAMPLE_TILE)
        else:
            lb = l // 2
            w_in = gmlp_w_in[lb].astype(BF16)
            w_out = gmlp_w_out[lb].astype(BF16)
            vg, vb = gmlp_vn_g[lb][None, :], gmlp_vn_b[lb][None, :]
            u_p, vn_p = _gmlp_in(xp, mod_in, w_in, vg, vb, IN_TILE, BF16)
            u_s, vn_s = _gmlp_in(xs, mod_s, w_in, vg, vb, SAMPLE_TILE, F32)
            new_gv.append(vn_s.reshape(dec_seq, nseq, d).transpose(1, 0, 2))
            gw = d // GMLP_GROUPS
            bexp = jnp.repeat(gmlp_b_s[lb].T, gw, axis=1)
            wexp = jnp.repeat(gmlp_w_s[lb][:, :dec_seq, :dec_seq].transpose(1, 2, 0).reshape(dec_seq * dec_seq, -1),
                              gw, axis=1)
            tile = pl.BlockSpec((ROW_TILE, d), lambda i: (i, 0))
            x1_p, h2c_p, lg_p = post(
                _pro_gmlp_prompt, (u_p, vn_p, gmlp_w_s[lb], bexp),
                [tile, tile, pl.BlockSpec(gmlp_w_s[lb].shape, lambda i: (0, 0, 0)),
                 pl.BlockSpec(bexp.shape, lambda i: (0, 0))],
                xp, w_out, mod_p, lng0, lnb0, wr_hi, wr_lo, br, ROW_TILE)
            x1_s, h2c_s, lg_s = post(
                _pro_gmlp_sample, (u_s, vn_s, wexp, bexp[:SUBLANES]),
                [pl.BlockSpec((SAMPLE_TILE, d), lambda i: (i, 0)),
                 pl.BlockSpec(vn_s.shape, lambda i: (0, 0)),
                 pl.BlockSpec(wexp.shape, lambda i: (0, 0)),
                 pl.BlockSpec((SUBLANES, d), lambda i: (0, 0))],
                xs, w_out, mod_s, lng0, lnb0, wr_hi, wr_lo, br, SAMPLE_TILE)

        xp, xs = _moe_and_norm(alpha, x1_p, x1_s, h2c_p, h2c_s, lg_p, lg_s, mod_p, mod_s, lng1, lnb1,
                               moe_w_gate, moe_w_up, moe_w_down, l)

    y_p = xp.reshape(batch, seq, d)
    y_s = xs.reshape(dec_seq, nseq, d).transpose(1, 0, 2)
    return (y_p, y_s, jnp.stack(new_kp), jnp.stack(new_vp), jnp.stack(new_ks), jnp.stack(new_vs),
            jnp.stack(new_gv))
```

```python
import functools

import jax
import jax.numpy as jnp
from jax import lax
from jax.experimental import pallas as pl
from jax.experimental.pallas import tpu as pltpu

F32, BF16, I32 = jnp.float32, jnp.bfloat16, jnp.int32

N_HEADS = 16
MOBA_BLOCK = 256
MOBA_TOPK = 3
ROPE_THETA = 500000.0
GMLP_CHUNK = 128
GMLP_GROUPS = 8
MOE_GROUPS = 4
MOE_EXPERTS_PER_GROUP = 8
LN_EPS = 1e-5

LANES = 128
SUBLANES = 8
VMEM_LIMIT = 56 * 1024 * 1024

ROW_TILE = 512
IN_TILE = 1024
QKV_COLS = 512
GMLP_IN_COLS = 512
SAMPLE_TILE = 128
MOE_ROWS = 128
COMBINE_TILE = 256
PAGES_PER_STEP = 8

NT_DIMS = (((1,), (1,)), ((), ()))


def _cparams(sem):
    return pltpu.CompilerParams(dimension_semantics=sem, vmem_limit_bytes=VMEM_LIMIT)


def _layer_norm(z, g, b):
    mu = jnp.mean(z, axis=-1, keepdims=True)
    zc = z - mu
    var = jnp.mean(zc * zc, axis=-1, keepdims=True)
    return zc * lax.rsqrt(var + LN_EPS) * g + b


def _adaln_body(c_ref, w_ref, b_ref, o_ref):
    a = jax.nn.silu(c_ref[...]).astype(BF16)
    o_ref[...] = jnp.dot(a, w_ref[...].astype(BF16), preferred_element_type=F32) + b_ref[...]


def _adaln(c_all, w_ada, b_ada):
    depth, d, n = w_ada.shape
    rows = c_all.shape[0]
    tn = 1024
    return pl.pallas_call(
        _adaln_body,
        grid=(depth, n // tn),
        in_specs=[
            pl.BlockSpec((rows, d), lambda l, j: (0, 0)),
            pl.BlockSpec((None, d, tn), lambda l, j: (l, 0, j)),
            pl.BlockSpec((None, 1, tn), lambda l, j: (l, 0, j)),
        ],
        out_specs=pl.BlockSpec((None, rows, tn), lambda l, j: (l, 0, j)),
        out_shape=jax.ShapeDtypeStruct((depth, rows, n), F32),
        compiler_params=_cparams(("arbitrary", "arbitrary")),
        name="adaln",
    )(c_all, w_ada, b_ada.reshape(depth, 1, n))


class _Mod:
    def __init__(self, arr, d, tiles_per_seq):
        self.arr = arr
        self.d = d
        self.tiles_per_seq = tiles_per_seq

    def spec(self, k, ngrid):
        d, tps = self.d, self.tiles_per_seq
        if tps is None:
            rows = self.arr.shape[0]
            if ngrid == 1:
                return pl.BlockSpec((rows, d), lambda i: (0, k))
            if ngrid == 2:
                return pl.BlockSpec((rows, d), lambda i, j: (0, k))
            return pl.BlockSpec((rows, d), lambda i, p, j: (0, k))
        if ngrid == 1:
            return pl.BlockSpec((None, 1, d), lambda i: (i // tps, 0, k))
        if ngrid == 2:
            return pl.BlockSpec((None, 1, d), lambda i, j: (i // tps, 0, k))
        return pl.BlockSpec((None, 1, d), lambda i, p, j: (i // tps, 0, k))


def _qkv_body(cols_outer, x_ref, sh_ref, sc_ref, wq_ref, wk_ref, wv_ref, c_ref, s1_ref, s2_ref,
              q_ref, k_ref, v_ref, h_scr):
    def modulate():
        h_scr[...] = (x_ref[...] * (1.0 + sc_ref[...]) + sh_ref[...]).astype(BF16)

    if cols_outer:
        modulate()
    else:
        pl.when(pl.program_id(1) == 0)(modulate)

    tn = q_ref.shape[1]
    reps = tn // LANES
    c = jnp.tile(c_ref[...], (1, reps))
    s1 = jnp.tile(s1_ref[...], (1, reps))
    s2 = jnp.tile(s2_ref[...], (1, reps))
    half = c_ref.shape[1] // 8

    def rope(y):
        return y * c + pltpu.roll(y, tn - half, 1) * s1 + pltpu.roll(y, half, 1) * s2

    h = h_scr[...]
    q_ref[...] = rope(jnp.dot(h, wq_ref[...], preferred_element_type=F32)).astype(q_ref.dtype)
    k_ref[...] = rope(jnp.dot(h, wk_ref[...], preferred_element_type=F32))
    v_ref[...] = jnp.dot(h, wv_ref[...], preferred_element_type=F32)


def _qkv(x, mod, w_qkv_bf16, tabs, tab_index, tm, tn, q_dtype, cols_outer=False):
    rows, d = x.shape
    ncol = d // tn
    c_tab, s1_tab, s2_tab = tabs
    assert not cols_outer or mod.tiles_per_seq is None

    def spec(block, index):
        if cols_outer:
            return pl.BlockSpec(block, lambda j, i: index(i, j))
        return pl.BlockSpec(block, index)

    tab_spec = spec((tm, LANES), lambda i, j: (tab_index(i), 0))
    out = spec((tm, tn), lambda i, j: (i, j))

    def w_spec(part):
        return spec((d, tn), lambda i, j: (0, part * ncol + j))

    return pl.pallas_call(
        functools.partial(_qkv_body, cols_outer),
        grid=(ncol, rows // tm) if cols_outer else (rows // tm, ncol),
        in_specs=[
            spec((tm, d), lambda i, j: (i, 0)),
            mod.spec(0, 2),
            mod.spec(1, 2),
            w_spec(0), w_spec(1), w_spec(2),
            tab_spec, tab_spec, tab_spec,
        ],
        out_specs=[out, out, out],
        out_shape=[
            jax.ShapeDtypeStruct((rows, d), q_dtype),
            jax.ShapeDtypeStruct((rows, d), F32),
            jax.ShapeDtypeStruct((rows, d), F32),
        ],
        scratch_shapes=[pltpu.VMEM((tm, d), BF16)],
        compiler_params=_cparams(("arbitrary", "arbitrary")),
        name="qkv_rope",
    )(x, mod.arr, mod.arr, w_qkv_bf16, w_qkv_bf16, w_qkv_bf16, c_tab, s1_tab, s2_tab)


def _rope_tables(pos, head_dim):
    rot = head_dim // 4
    half = rot // 2
    inv = ROPE_THETA ** (-jnp.arange(half, dtype=F32) * 2.0 / rot)
    ang = pos.astype(F32)[:, None] * inv[None, :]
    cos, sin = jnp.cos(ang), jnp.sin(ang)
    n = pos.shape[0]
    ones = jnp.ones((n, head_dim - rot), F32)
    zeros = jnp.zeros((n, head_dim - rot), F32)
    zh = jnp.zeros((n, half), F32)
    c = jnp.concatenate([cos, cos, ones], axis=1)
    s1 = jnp.concatenate([-sin, zh, zeros], axis=1)
    s2 = jnp.concatenate([zh, sin, zeros], axis=1)
    return c, s1, s2


def _attn_prompt_body(q_ref, k_ref, v_ref, o_ref, kb_scr, vb_scr, sel_scr, qs_scr):
    seq, dh = q_ref.shape
    nblk = seq // MOBA_BLOCK
    scale = dh ** -0.5
    kf = k_ref[...]
    kb_scr[...] = kf.astype(BF16)
    vb_scr[:, :dh] = v_ref[...].astype(BF16)
    vb_scr[:, dh:] = jnp.ones((seq, dh), BF16)

    km = kf.reshape(nblk, MOBA_BLOCK, dh).sum(axis=1) * (1.0 / MOBA_BLOCK)
    g = lax.dot_general(km, q_ref[...].astype(F32), NT_DIMS, precision=lax.Precision.HIGHEST,
                        preferred_element_type=F32)
    blk = lax.broadcasted_iota(I32, (nblk, seq), 0)
    cur = lax.broadcasted_iota(I32, (nblk, seq), 1) // MOBA_BLOCK
    past = blk < cur
    g = jnp.where(past, g, -jnp.inf)
    cnt = jnp.zeros((nblk, seq), F32)
    for k in range(nblk):
        gk = g[k:k + 1, :]
        beats = (gk > g) | ((gk == g) & (k < blk))
        cnt = cnt + beats.astype(F32)
    sel = ((cnt < MOBA_TOPK) & past).astype(F32)
    sel_scr[...] = jnp.concatenate([sel, jnp.zeros((LANES - nblk, seq), F32)], axis=0).T

    row = lax.broadcasted_iota(I32, (MOBA_BLOCK, MOBA_BLOCK), 0)
    col = lax.broadcasted_iota(I32, (MOBA_BLOCK, MOBA_BLOCK), 1)
    causal = row >= col

    qs_scr[...] = (q_ref[...].astype(F32) * scale).astype(BF16)

    for c in range(nblk):
        lo, hi = c * MOBA_BLOCK, (c + 1) * MOBA_BLOCK
        qc = qs_scr[lo:hi, :]
        selc = sel_scr[lo:hi, :]
        slabs = []
        for j in range(c + 1):
            s = lax.dot_general(qc, kb_scr[j * MOBA_BLOCK:(j + 1) * MOBA_BLOCK, :], NT_DIMS,
                                preferred_element_type=F32)
            if j < c:
                s = jnp.where(selc[:, j:j + 1] > 0.5, s, -jnp.inf)
            else:
                s = jnp.where(causal, s, -jnp.inf)
            slabs.append(s)
        mx = slabs[0]
        for s in slabs[1:]:
            mx = jnp.maximum(mx, s)
        m = mx.max(axis=1, keepdims=True)
        acc = jnp.zeros((MOBA_BLOCK, 2 * dh), F32)
        for j, s in enumerate(slabs):
            p = jnp.exp(s - m)
            acc = acc + jnp.dot(p.astype(BF16), vb_scr[j * MOBA_BLOCK:(j + 1) * MOBA_BLOCK, :],
                                preferred_element_type=F32)
        o_ref[lo:hi, :] = (acc[:, :dh] / acc[:, dh:]).astype(o_ref.dtype)


def _attn_prompt(q, k, v, batch, seq):
    rows, d = q.shape
    dh = d // N_HEADS
    spec = pl.BlockSpec((seq, dh), lambda b, h: (b, h))
    return pl.pallas_call(
        _attn_prompt_body,
        grid=(batch, N_HEADS),
        in_specs=[spec, spec, spec],
        out_specs=spec,
        out_shape=jax.ShapeDtypeStruct((rows, d), BF16),
        scratch_shapes=[pltpu.VMEM((seq, dh), BF16), pltpu.VMEM((seq, 2 * dh), BF16), pltpu.VMEM((seq, LANES), F32),
                        pltpu.VMEM((seq, dh), BF16)],
        compiler_params=_cparams(("arbitrary", "arbitrary")),
        name="moba_prompt",
    )(q, k, v)


HEAD_GROUP = SUBLANES


def _attn_sample_body(pt_ref, q_ref, kn_ref, vn_ref, hm_ref, *rest):
    npg = PAGES_PER_STEP
    k_refs = rest[:npg]
    v_refs = rest[npg:2 * npg]
    o_ref = rest[2 * npg]
    s_scr, p_scr, ksum_scr, bmax_scr, acc_scr, linv_scr = rest[2 * npg + 1:]

    t = pl.program_id(1)
    n_pages, ngrp, nq, grp_rows = s_scr.shape
    hg = HEAD_GROUP
    k_steps = n_pages // npg
    dh = q_ref.shape[2]
    page = grp_rows // hg
    pages_per_blk = MOBA_BLOCK // page
    blk_per_step = npg // pages_per_blk
    nblk = n_pages // pages_per_blk
    scale = dh ** -0.5

    @pl.when(t < k_steps)
    def _():
        hm = hm_ref[...]
        sums = []
        pmax = [[] for _ in range(ngrp)]
        for i in range(npg):
            kp = k_refs[i][...]
            sums.append(jnp.sum(kp, axis=0))
            for g in range(ngrp):
                kb = kp[:, g * hg:(g + 1) * hg, :].reshape(grp_rows, dh).astype(BF16)
                s = lax.dot_general(q_ref[g].astype(BF16), kb, NT_DIMS, preferred_element_type=F32) * scale + hm
                s_scr[t * npg + i, g] = s
                pmax[g].append(s.max(axis=1, keepdims=True))
        for b in range(blk_per_step):
            j = t * blk_per_step + b
            tot = sums[b * pages_per_blk]
            for i in range(1, pages_per_blk):
                tot = tot + sums[b * pages_per_blk + i]
            ksum_scr[j] = tot
            for g in range(ngrp):
                bm = pmax[g][b * pages_per_blk]
                for i in range(1, pages_per_blk):
                    bm = jnp.maximum(bm, pmax[g][b * pages_per_blk + i])
                bmax_scr[j, g] = jnp.broadcast_to(bm, (nq, LANES))

    @pl.when(t == k_steps - 1)
    def _():
        gw = nblk * hg
        width = ngrp * gw
        km = jnp.concatenate(
            [ksum_scr[:, g * hg:(g + 1) * hg, :].reshape(gw, dh) for g in range(ngrp)], axis=0) * (1.0 / MOBA_BLOCK)
        qall = jnp.concatenate([q_ref[g] for g in range(ngrp)], axis=0)
        gate = lax.dot_general(qall, km, NT_DIMS, precision=lax.Precision.HIGHEST, preferred_element_type=F32)
        row = lax.broadcasted_iota(I32, (ngrp * nq, width), 0)
        lane = lax.broadcasted_iota(I32, (ngrp * nq, width), 1)
        own = ((row // nq) == (lane // gw)) & ((row % hg) == (lane % hg))
        blk = (lane % gw) // hg
        gate = jnp.where(own, gate, -jnp.inf)
        cnt = jnp.zeros((ngrp * nq, width), F32)
        for dlt in range(1, nblk):
            lower = pltpu.roll(gate, dlt * hg, 1)
            cnt = cnt + ((blk >= dlt) & (lower >= gate)).astype(F32)
            upper = pltpu.roll(gate, width - dlt * hg, 1)
            cnt = cnt + ((blk + dlt < nblk) & (upper > gate)).astype(F32)
        sel = jnp.where(own & (cnt < MOBA_TOPK), 1.0, 0.0)

        r = lax.broadcasted_iota(I32, (nq, nq), 0)
        c = lax.broadcasted_iota(I32, (nq, nq), 1)
        new_ok = ((r % hg) == (c % hg)) & ((c // hg) <= (r // hg))
        for g in range(ngrp):
            qb = q_ref[g].astype(BF16)
            sn = lax.dot_general(qb, kn_ref[g].astype(BF16), NT_DIMS, preferred_element_type=F32) * scale
            sn = jnp.where(new_ok, sn, -jnp.inf)
            m = sn.max(axis=1, keepdims=True)
            biases = []
            for j in range(nblk):
                lo = g * gw + j * hg
                picked = sel[g * nq:(g + 1) * nq, lo:lo + hg].max(axis=1, keepdims=True)
                bias = jnp.where(picked > 0.5, 0.0, -jnp.inf)
                biases.append(bias)
                m = jnp.maximum(m, bmax_scr[j, g][:, 0:1] + bias)
            en = jnp.exp(sn - m)
            l = en.sum(axis=1, keepdims=True)
            for pg in range(n_pages):
                e = jnp.exp(s_scr[pg, g] + (biases[pg // pages_per_blk] - m))
                l = l + e.sum(axis=1, keepdims=True)
                p_scr[pg, g] = e.astype(BF16)
            linv_scr[g] = jnp.broadcast_to(1.0 / l, (nq, dh))
            acc_scr[g] = jnp.dot(en.astype(BF16), vn_ref[g].astype(BF16), preferred_element_type=F32)

    @pl.when(t >= k_steps)
    def _():
        accs = [acc_scr[g] for g in range(ngrp)]
        for i in range(npg):
            vp = v_refs[i][...]
            for g in range(ngrp):
                vb = vp[:, g * hg:(g + 1) * hg, :].reshape(grp_rows, dh).astype(BF16)
                accs[g] = accs[g] + jnp.dot(p_scr[(t - k_steps) * npg + i, g], vb, preferred_element_type=F32)
        for g in range(ngrp):
            acc_scr[g] = accs[g]

    @pl.when(t == 2 * k_steps - 1)
    def _():
        o_ref[...] = acc_scr[...] * linv_scr[...]


def _attn_sample(q_rows, k_new, v_new, cache_k, cache_v, layer, page_table):
    nseq, ngrp, nq, dh = q_rows.shape
    _, n_phys, page, heads, _ = cache_k.shape
    n_pages = page_table.shape[1]
    npg = PAGES_PER_STEP
    k_steps = n_pages // npg
    grp_rows = page * HEAD_GROUP
    rowh = jnp.arange(nq, dtype=I32)[:, None] % HEAD_GROUP
    colh = jnp.arange(grp_rows, dtype=I32)[None, :] % HEAD_GROUP
    head_mask = jnp.where(rowh == colh, 0.0, -jnp.inf).astype(F32)

    def kmap(i):
        return lambda b, t, pt: (layer, pt[b * n_pages + jnp.minimum(t, k_steps - 1) * npg + i], 0, 0, 0)

    def vmap_(i):
        def index(b, t, pt):
            in_v = t >= k_steps
            seq_i = jnp.where(in_v, b, jnp.maximum(b - 1, 0))
            step = jnp.where(in_v, t - k_steps, k_steps - 1)
            return (layer, pt[seq_i * n_pages + step * npg + i], 0, 0, 0)
        return index

    seq_spec = pl.BlockSpec((None, ngrp, nq, dh), lambda b, t, pt: (b, 0, 0, 0))
    page_block = (None, None, page, heads, dh)
    nblk = n_pages * page // MOBA_BLOCK
    grid_spec = pltpu.PrefetchScalarGridSpec(
        num_scalar_prefetch=1,
        grid=(nseq, 2 * k_steps),
        in_specs=[seq_spec, seq_spec, seq_spec,
                  pl.BlockSpec((nq, grp_rows), lambda b, t, pt: (0, 0))]
                 + [pl.BlockSpec(page_block, kmap(i)) for i in range(npg)]
                 + [pl.BlockSpec(page_block, vmap_(i)) for i in range(npg)],
        out_specs=seq_spec,
        scratch_shapes=[
            pltpu.VMEM((n_pages, ngrp, nq, grp_rows), F32),
            pltpu.VMEM((n_pages, ngrp, nq, grp_rows), BF16),
            pltpu.VMEM((nblk, heads, dh), F32),
            pltpu.VMEM((nblk, ngrp, nq, LANES), F32),
            pltpu.VMEM((ngrp, nq, dh), F32),
            pltpu.VMEM((ngrp, nq, dh), F32),
        ],
    )
    return pl.pallas_call(
        _attn_sample_body,
        grid_spec=grid_spec,
        out_shape=jax.ShapeDtypeStruct((nseq, ngrp, nq, dh), F32),
        compiler_params=_cparams(("arbitrary", "arbitrary")),
        name="moba_sample",
    )(page_table.reshape(-1), q_rows, k_new, v_new, head_mask,
      *([cache_k] * npg), *([cache_v] * npg))


def _post_body(alpha, prologue, n_pro, *refs):
    pro_refs = refs[:n_pro]
    (x_ref, w_ref, gt_ref, lng_ref, lnb_ref, sh_ref, sc_ref, wrh_ref, wrl_ref, br_ref,
     x1_ref, h2c_ref, lg_ref) = refs[n_pro:]
    a = prologue(*pro_refs)
    y = jnp.dot(a, w_ref[...], preferred_element_type=F32)
    z = alpha * x_ref[...] + (1.0 + gt_ref[...]) * y
    x1 = _layer_norm(z, lng_ref[...], lnb_ref[...])
    x1_ref[...] = x1
    h2 = x1 * (1.0 + sc_ref[...]) + sh_ref[...]
    tm, d = h2.shape
    hi = h2.astype(BF16)
    hi32 = hi.astype(F32)
    bits = lax.bitcast_convert_type(hi32, I32)
    packed = bits[:, :d // 2] | lax.shift_right_logical(bits[:, d // 2:], 16)
    nchunk = d // (2 * LANES)
    for c in range(nchunk):
        h2c_ref[pl.ds(c, tm, stride=nchunk), :] = packed[:, c * LANES:(c + 1) * LANES]
    lo = (h2 - hi32).astype(BF16)
    wh, wl = wrh_ref[...], wrl_ref[...]
    lg = (lax.dot_general(wh, hi, NT_DIMS, preferred_element_type=F32)
          + lax.dot_general(wl, hi, NT_DIMS, preferred_element_type=F32)
          + lax.dot_general(wh, lo, NT_DIMS, preferred_element_type=F32))
    lg_ref[...] = lg + br_ref[...]


def _pro_identity(a_ref):
    return a_ref[...]


def _pro_gmlp_prompt(u_ref, v_ref, ws_ref, bexp_ref):
    tm, d = u_ref.shape
    gw = d // GMLP_GROUPS
    row = lax.broadcasted_iota(I32, (GMLP_CHUNK, GMLP_CHUNK), 0)
    col = lax.broadcasted_iota(I32, (GMLP_CHUNK, GMLP_CHUNK), 1)
    tri = row >= col
    wms = [jnp.where(tri, ws_ref[g], 0.0).astype(BF16) for g in range(GMLP_GROUPS)]
    outs = []
    for ch in range(tm // GMLP_CHUNK):
        lo, hi = ch * GMLP_CHUNK, (ch + 1) * GMLP_CHUNK
        cols = [jnp.dot(wms[g], v_ref[lo:hi, g * gw:(g + 1) * gw], preferred_element_type=F32)
                for g in range(GMLP_GROUPS)]
        mixed = jnp.concatenate(cols, axis=1) + bexp_ref[...]
        outs.append((u_ref[lo:hi, :].astype(F32) * mixed).astype(BF16))
    return jnp.concatenate(outs, axis=0)


def _pro_gmlp_sample(u_ref, v_ref, wexp_ref, bexp_ref):
    s = pl.program_id(0)
    tm = u_ref.shape[0]
    n_pos = v_ref.shape[0] // tm
    mixed = bexp_ref[pl.ds(s, 1), :]
    for j in range(n_pos):
        w = wexp_ref[pl.ds(s * n_pos + j, 1), :] * (j <= s).astype(F32)
        mixed = mixed + w * v_ref[j * tm:(j + 1) * tm, :]
    return (u_ref[...].astype(F32) * mixed).astype(BF16)


def _post(alpha, prologue, pro_args, pro_specs, x, w_bf16, mod, ln_g, ln_b, wr_hi, wr_lo, br, tm):
    rows, d = x.shape
    ne = wr_hi.shape[0]
    nchunk = d // (2 * LANES)
    vec = pl.BlockSpec((1, d), lambda i: (0, 0))
    return pl.pallas_call(
        functools.partial(_post_body, alpha, prologue, len(pro_args)),
        grid=(rows // tm,),
        in_specs=list(pro_specs) + [
            pl.BlockSpec((tm, d), lambda i: (i, 0)),
            pl.BlockSpec(w_bf16.shape, lambda i: (0, 0)),
            mod.spec(2, 1), vec, vec, mod.spec(3, 1), mod.spec(4, 1),
            pl.BlockSpec((ne, d), lambda i: (0, 0)),
            pl.BlockSpec((ne, d), lambda i: (0, 0)),
            pl.BlockSpec((ne, 1), lambda i: (0, 0)),
        ],
        out_specs=[
            pl.BlockSpec((tm, d), lambda i: (i, 0)),
            pl.BlockSpec((tm * nchunk, LANES), lambda i: (i, 0)),
            pl.BlockSpec((ne, tm), lambda i: (0, i)),
        ],
        out_shape=[
            jax.ShapeDtypeStruct((rows, d), F32),
            jax.ShapeDtypeStruct((rows * nchunk, LANES), I32),
            jax.ShapeDtypeStruct((ne, rows), F32),
        ],
        compiler_params=_cparams(("arbitrary",)),
        name="proj_postnorm_router",
    )(*pro_args, x, w_bf16, mod.arr, ln_g, ln_b, mod.arr, mod.arr, wr_hi, wr_lo, br)


def _route_body(lg_ref, eid_ref, gate_ref, dest_ref, meta_ref):
    ne = MOE_GROUPS * MOE_EXPERTS_PER_GROUP
    epg = MOE_EXPERTS_PER_GROUP
    lgt = lg_ref[...]
    t = lgt.shape[1]
    row8 = lax.broadcasted_iota(I32, (SUBLANES, t), 0)
    lgp = jnp.where(row8 < MOE_GROUPS, lgt[0:SUBLANES], -jnp.inf)
    mg = lgp.max(axis=0, keepdims=True)
    grp = jnp.where(lgp == mg, row8, SUBLANES).min(axis=0, keepdims=True)
    pg = 1.0 / jnp.exp(lgp - mg).sum(axis=0, keepdims=True)

    le = jnp.zeros((epg, t), F32)
    for g in range(MOE_GROUPS):
        le = jnp.where(grp == g, lgt[SUBLANES + g * epg:SUBLANES + (g + 1) * epg], le)
    m1 = le.max(axis=0, keepdims=True)
    i1 = jnp.where(le == m1, row8, epg).min(axis=0, keepdims=True)
    le2 = jnp.where(row8 == i1, -jnp.inf, le)
    m2 = le2.max(axis=0, keepdims=True)
    i2 = jnp.where(le2 == m2, row8, epg).min(axis=0, keepdims=True)
    e = jnp.exp(m2 - m1)
    g1 = pg / (1.0 + e)
    g2 = pg * e / (1.0 + e)
    e1 = grp * epg + i1
    e2 = grp * epg + i2
    eid_ref[0:1, :] = e1
    eid_ref[1:2, :] = e2
    gate_ref[0:1, :] = g1
    gate_ref[1:2, :] = g2

    rows = lax.broadcasted_iota(I32, (ne, t), 0)
    oh1 = (rows == e1).astype(F32)
    oh2 = (rows == e2).astype(F32)
    oh = (oh1 + oh2).astype(BF16)
    cw = 512
    ur = lax.broadcasted_iota(I32, (cw, cw), 0)
    uc = lax.broadcasted_iota(I32, (cw, cw), 1)
    upper = (ur < uc).astype(BF16)
    carry = jnp.zeros((ne, 1), F32)
    pref = []
    for ci in range(t // cw):
        ohc = oh[:, ci * cw:(ci + 1) * cw]
        pref.append(jnp.dot(ohc, upper, preferred_element_type=F32) + carry)
        carry = carry + ohc.astype(F32).sum(axis=1, keepdims=True)
    cnt_before = jnp.concatenate(pref, axis=1)

    counts = carry
    shift = MOE_ROWS.bit_length() - 1
    padded = (((counts.astype(I32) + (MOE_ROWS - 1)) >> shift) << shift).astype(F32)
    lr = lax.broadcasted_iota(I32, (ne, ne), 0)
    lc = lax.broadcasted_iota(I32, (ne, ne), 1)
    lower = (lc < lr).astype(F32)
    pstart = jnp.dot(lower, jnp.broadcast_to(padded, (ne, LANES)), precision=lax.Precision.HIGHEST,
                     preferred_element_type=F32)[:, 0:1]
    pend = pstart + padded
    base = pstart + cnt_before
    d1 = (oh1 * base).sum(axis=0, keepdims=True)
    d2 = (oh2 * base).sum(axis=0, keepdims=True)
    dest_ref[0:1, :] = d1.astype(I32)
    dest_ref[1:2, :] = d2.astype(I32)

    mw = meta_ref.shape[1]
    blk_start = (lax.broadcasted_iota(I32, (ne, mw), 1) * MOE_ROWS).astype(F32)
    blk_e = jnp.minimum((pend <= blk_start).astype(F32).sum(axis=0, keepdims=True), ne - 1.0)
    last = lax.broadcasted_iota(I32, (ne, mw), 0) == ne - 1
    nused = jnp.where(last, jnp.broadcast_to(pend, (ne, mw)), 0.0).sum(axis=0, keepdims=True) * (1.0 / MOE_ROWS)
    lane_e = lax.broadcasted_iota(I32, (ne, mw), 1)
    diag = lane_e == lax.broadcasted_iota(I32, (ne, mw), 0)
    fill_lo = jnp.where(diag, jnp.broadcast_to(pstart + counts, (ne, mw)), 0.0).sum(axis=0, keepdims=True)
    fill_hi = jnp.where(diag, jnp.broadcast_to(pend, (ne, mw)), 0.0).sum(axis=0, keepdims=True)
    mrow = lax.broadcasted_iota(I32, (SUBLANES, mw), 0)
    meta = jnp.where(mrow == 0, blk_e, jnp.where(mrow == 1, nused, jnp.where(mrow == 2, fill_lo, fill_hi)))
    meta_ref[...] = meta.astype(I32)


def _route(logits_t, n_blocks_max):
    ne_pad, t = logits_t.shape
    mw = 256
    assert n_blocks_max <= mw
    return pl.pallas_call(
        _route_body,
        out_shape=[
            jax.ShapeDtypeStruct((2, t), I32),
            jax.ShapeDtypeStruct((2, t), F32),
            jax.ShapeDtypeStruct((2, t), I32),
            jax.ShapeDtypeStruct((SUBLANES, mw), I32),
        ],
        compiler_params=pltpu.CompilerParams(vmem_limit_bytes=VMEM_LIMIT),
        name="moe_route",
    )(logits_t)


def _dispatch_body(nchunk, n_prompt_tiles, fill_ref, dest_ref, hp_ref, hs_ref, xb_ref, zero_scr, sem):
    i = pl.program_id(0)
    tm = dest_ref.shape[1]
    n_ranges = fill_ref.shape[1]

    def slab(ref, row):
        return ref.at[pl.ds(pl.multiple_of(row * nchunk, nchunk), nchunk)]

    def issue(src_ref):
        def body(tok, carry):
            for k in range(2):
                pltpu.make_async_copy(slab(src_ref, tok), slab(xb_ref, dest_ref[k, tok]), sem).start()
            return carry
        lax.fori_loop(0, tm, body, 0, unroll=8)

    @pl.when(i < n_prompt_tiles)
    def _():
        issue(hp_ref)

    @pl.when(i >= n_prompt_tiles)
    def _():
        issue(hs_ref)

    for _ in range(2):
        pltpu.make_async_copy(hp_ref, xb_ref.at[pl.ds(0, tm * nchunk)], sem).wait()

    @pl.when(i == 0)
    def _():
        zero_scr[...] = jnp.zeros_like(zero_scr)
        total = 0
        for e in range(n_ranges):
            lo, hi = fill_ref[0, e], fill_ref[1, e]

            def body(r, carry):
                pltpu.make_async_copy(zero_scr, slab(xb_ref, r), sem).start()
                return carry
            lax.fori_loop(lo, hi, body, 0)
            total = total + (hi - lo)

        def wbody(_, carry):
            pltpu.make_async_copy(zero_scr, slab(xb_ref, 0), sem).wait()
            return carry
        lax.fori_loop(0, total, wbody, 0)


def _dispatch(fill, dest_tiles, h2c_p, h2c_s, n_rows, nchunk):
    n_tiles, _, tm = dest_tiles.shape
    n_prompt_tiles = h2c_p.shape[0] // (nchunk * tm)
    grid_spec = pltpu.PrefetchScalarGridSpec(
        num_scalar_prefetch=1,
        grid=(n_tiles,),
        in_specs=[
            pl.BlockSpec((None, 2, tm), lambda i, f: (i, 0, 0), memory_space=pltpu.SMEM),
            pl.BlockSpec((tm * nchunk, LANES), lambda i, f: (jnp.minimum(i, n_prompt_tiles - 1), 0)),
            pl.BlockSpec((tm * nchunk, LANES), lambda i, f: (jnp.maximum(i - n_prompt_tiles, 0), 0)),
        ],
        out_specs=pl.BlockSpec(memory_space=pl.ANY),
        scratch_shapes=[pltpu.VMEM((nchunk, LANES), h2c_p.dtype), pltpu.SemaphoreType.DMA(())],
    )
    return pl.pallas_call(
        functools.partial(_dispatch_body, nchunk, n_prompt_tiles),
        grid_spec=grid_spec,
        out_shape=jax.ShapeDtypeStruct((n_rows * nchunk, LANES), h2c_p.dtype),
        compiler_params=_cparams(("arbitrary",)),
        name="moe_dispatch",
    )(fill, dest_tiles, h2c_p, h2c_s)


def _expert_body(layer, blk_ref, nused_ref, ord_ref, uniq_ref, ndist_ref, x_ref, wg_hbm, wu_hbm, wd_hbm, y_ref,
                 wg_buf, wu_buf, wd_buf, wg_scr, wu_scr, wd_scr, sem):
    i = pl.program_id(0)
    npack = x_ref.shape[0] // MOE_ROWS
    pairs = ((wg_hbm, wg_buf), (wu_hbm, wu_buf), (wd_hbm, wd_buf))

    def copies(n, slot):
        e = uniq_ref[n]
        return [pltpu.make_async_copy(src.at[layer, e], dst.at[slot], sem.at[slot]) for src, dst in pairs]

    @pl.when(i < nused_ref[0])
    def _():
        prev = blk_ref[jnp.maximum(i - 1, 0)]

        @pl.when(i == 0)
        def _():
            for cp in copies(0, 0):
                cp.start()

        @pl.when((i == 0) | (blk_ref[i] != prev))
        def _():
            n = ord_ref[i]
            slot = n % 2
            for cp in copies(n, slot):
                cp.wait()

            @pl.when(n + 1 < ndist_ref[0])
            def _():
                for cp in copies(n + 1, 1 - slot):
                    cp.start()

            wg_scr[...] = wg_buf[slot].astype(BF16)
            wu_scr[...] = wu_buf[slot].astype(BF16)
            wd_scr[...] = wd_buf[slot].astype(BF16)

        words = jnp.concatenate([x_ref[pl.ds(c, MOE_ROWS, stride=npack), :] for c in range(npack)], axis=1)
        x_hi = lax.bitcast_convert_type(words & jnp.int32(-65536), F32)
        x_lo = lax.bitcast_convert_type(lax.shift_left(words, 16), F32)
        x = jnp.concatenate([x_hi, x_lo], axis=1).astype(BF16)
        g = jnp.dot(x, wg_scr[...], preferred_element_type=F32)
        u = jnp.dot(x, wu_scr[...], preferred_element_type=F32)
        a = (jax.nn.silu(g) * u).astype(BF16)
        y = jnp.dot(a, wd_scr[...], preferred_element_type=F32)
        ybits = lax.bitcast_convert_type(y.astype(BF16).astype(F32), I32)
        half = y.shape[1] // 2
        packed = ybits[:, :half] | lax.shift_right_logical(ybits[:, half:], 16)
        for c in range(npack):
            y_ref[pl.ds(c, MOE_ROWS, stride=npack), :] = packed[:, c * LANES:(c + 1) * LANES]

    @pl.when(i >= nused_ref[0])
    def _():
        y_ref[...] = jnp.zeros_like(y_ref)


def _experts(blk_e, nused, xb, w_gate, w_up, w_down, layer, n_blocks):
    _, ne, d, de = w_gate.shape
    rows_blk = MOE_ROWS * (d // (2 * LANES))

    ids = jnp.arange(n_blocks, dtype=I32)
    change = (ids < nused[0]) & ((ids == 0) | (blk_e != jnp.roll(blk_e, 1)))
    ordinal = jnp.cumsum(change.astype(I32)) - 1
    n_distinct = change.astype(I32).sum()[None]
    uniq = jnp.zeros((ne + 1,), I32).at[jnp.where(change, ordinal, ne)].set(blk_e)

    def blk(i, be, nu, *_):
        return jnp.minimum(i, nu[0] - 1)

    grid_spec = pltpu.PrefetchScalarGridSpec(
        num_scalar_prefetch=5,
        grid=(n_blocks,),
        in_specs=[
            pl.BlockSpec((rows_blk, LANES), lambda i, *s: (blk(i, *s), 0)),
            pl.BlockSpec(memory_space=pl.ANY),
            pl.BlockSpec(memory_space=pl.ANY),
            pl.BlockSpec(memory_space=pl.ANY),
        ],
        out_specs=pl.BlockSpec((rows_blk, LANES), lambda i, *s: (i, 0)),
        scratch_shapes=[
            pltpu.VMEM((2, d, de), F32), pltpu.VMEM((2, d, de), F32), pltpu.VMEM((2, de, d), F32),
            pltpu.VMEM((d, de), BF16), pltpu.VMEM((d, de), BF16), pltpu.VMEM((de, d), BF16),
            pltpu.SemaphoreType.DMA((2,)),
        ],
    )
    return pl.pallas_call(
        functools.partial(_expert_body, layer),
        grid_spec=grid_spec,
        out_shape=jax.ShapeDtypeStruct(xb.shape, xb.dtype),
        compiler_params=_cparams(("arbitrary",)),
        name="moe_experts",
    )(blk_e, nused, ordinal, uniq, n_distinct, xb, w_gate, w_up, w_down)


def _combine_body(alpha, dest_ref, next_ref, x_ref, gts_ref, gt_ref, lng_ref, lnb_ref, yb_ref, o_ref, buf, sem):
    i = pl.program_id(0)
    tm, d = x_ref.shape
    nchunk = d // (2 * LANES)

    def slab(ref, row):
        return ref.at[pl.ds(pl.multiple_of(row * nchunk, nchunk), nchunk)]

    def issue(idx_ref, phase):
        def body(tok, carry):
            for k in range(2):
                pltpu.make_async_copy(slab(yb_ref, idx_ref[k, tok]), slab(buf.at[phase, k], tok),
                                      sem.at[phase]).start()
            return carry
        lax.fori_loop(0, tm, body, 0, unroll=8)

    @pl.when(i == 0)
    def _():
        issue(dest_ref, 0)

    @pl.when(i + 1 < pl.num_programs(0))
    def _():
        issue(next_ref, (i + 1) % 2)

    phase = i % 2
    for k in range(2):
        pltpu.make_async_copy(yb_ref.at[pl.ds(0, tm * nchunk)], buf.at[phase, k], sem.at[phase]).wait()

    gts = gts_ref[...]
    f = jnp.zeros((tm, d), F32)
    for k in range(2):
        rows_k = buf.at[phase, k]
        words = jnp.concatenate([rows_k[pl.ds(c, tm, stride=nchunk), :] for c in range(nchunk)], axis=1)
        y_hi = lax.bitcast_convert_type(words & jnp.int32(-65536), F32)
        y_lo = lax.bitcast_convert_type(lax.shift_left(words, 16), F32)
        f = f + gts[:, k:k + 1] * jnp.concatenate([y_hi, y_lo], axis=1)
    z = alpha * x_ref[...] + (1.0 + gt_ref[...]) * f
    o_ref[...] = _layer_norm(z, lng_ref[...], lnb_ref[...])


def _combine(alpha, dest_tiles, x1, gates_t, mod, ln_g, ln_b, yb):
    rows, d = x1.shape
    n_tiles, _, tm = dest_tiles.shape
    nchunk = d // (2 * LANES)
    vec = pl.BlockSpec((1, d), lambda i: (0, 0))
    if mod.tiles_per_seq is not None:
        mod = _Mod(mod.arr, d, mod.tiles_per_seq * (ROW_TILE // tm))
    return pl.pallas_call(
        functools.partial(_combine_body, alpha),
        grid=(n_tiles,),
        in_specs=[
            pl.BlockSpec((None, 2, tm), lambda i: (i, 0, 0), memory_space=pltpu.SMEM),
            pl.BlockSpec((None, 2, tm), lambda i: (jnp.minimum(i + 1, n_tiles - 1), 0, 0), memory_space=pltpu.SMEM),
            pl.BlockSpec((tm, d), lambda i: (i, 0)),
            pl.BlockSpec((tm, 2), lambda i: (i, 0)),
            mod.spec(5, 1), vec, vec,
            pl.BlockSpec(memory_space=pl.ANY),
        ],
        out_specs=pl.BlockSpec((tm, d), lambda i: (i, 0)),
        out_shape=jax.ShapeDtypeStruct((rows, d), F32),
        scratch_shapes=[pltpu.VMEM((2, 2, tm * nchunk, LANES), yb.dtype), pltpu.SemaphoreType.DMA((2,))],
        compiler_params=_cparams(("arbitrary",)),
        name="moe_combine_postnorm",
    )(dest_tiles, dest_tiles, x1, gates_t, mod.arr, ln_g, ln_b, yb)


def _gmlp_in_body(x_ref, sh_ref, sc_ref, wu_ref, wv_ref, vg_ref, vb_ref, u_ref, v_ref, h_scr, v_scr):
    j = pl.program_id(1)
    half = v_scr.shape[0]

    @pl.when(j == 0)
    def _():
        h_scr[...] = (x_ref[...] * (1.0 + sc_ref[...]) + sh_ref[...]).astype(BF16)

    h = h_scr[...]
    u_ref[...] = jax.nn.gelu(jnp.dot(h, wu_ref[...], preferred_element_type=F32)).astype(u_ref.dtype)
    v_scr[j] = jax.nn.gelu(jnp.dot(h, wv_ref[...], preferred_element_type=F32))

    @pl.when(j == half - 1)
    def _():
        v = jnp.concatenate([v_scr[c] for c in range(half)], axis=1)
        v_ref[...] = _layer_norm(v, vg_ref[...], vb_ref[...]).astype(v_ref.dtype)


def _gmlp_in(x, mod, w_in_bf16, vn_g, vn_b, tm, v_dtype):
    rows, d = x.shape
    tn = GMLP_IN_COLS
    half = d // tn
    vec = pl.BlockSpec((1, d), lambda i, j: (0, 0))
    return pl.pallas_call(
        _gmlp_in_body,
        grid=(rows // tm, half),
        in_specs=[
            pl.BlockSpec((tm, d), lambda i, j: (i, 0)),
            mod.spec(0, 2), mod.spec(1, 2),
            pl.BlockSpec((d, tn), lambda i, j: (0, j)),
            pl.BlockSpec((d, tn), lambda i, j: (0, half + j)),
            vec, vec,
        ],
        out_specs=[
            pl.BlockSpec((tm, tn), lambda i, j: (i, j)),
            pl.BlockSpec((tm, d), lambda i, j: (i, 0)),
        ],
        out_shape=[jax.ShapeDtypeStruct((rows, d), BF16), jax.ShapeDtypeStruct((rows, d), v_dtype)],
        scratch_shapes=[pltpu.VMEM((tm, d), BF16), pltpu.VMEM((half, tm, tn), F32)],
        compiler_params=_cparams(("arbitrary", "arbitrary")),
        name="gmlp_in",
    )(x, mod.arr, mod.arr, w_in_bf16, w_in_bf16, vn_g, vn_b)


def _moe_and_norm(alpha, x1_p, x1_s, h2c_p, h2c_s, lg_p, lg_s, mod_p, mod_s, ln_g, ln_b,
                  w_gate, w_up, w_down, layer):
    t_p, d = x1_p.shape
    t_s = x1_s.shape[0]
    t = t_p + t_s
    nchunk = d // LANES
    ne = w_gate.shape[1]
    n_blocks = -(-(2 * t) // MOE_ROWS) + ne
    n_rows = n_blocks * MOE_ROWS

    eid, gates, dest, meta = _route(jnp.concatenate([lg_p, lg_s], axis=1), n_blocks)
    blk_e = meta[0, :n_blocks]
    nused = meta[1, :1]
    tail = jnp.stack([nused * MOE_ROWS, jnp.full((1,), n_rows, I32)])
    fill = jnp.concatenate([meta[2:4, :ne], tail], axis=1)

    def tiles(a, tm):
        return a.reshape(2, -1, tm).transpose(1, 0, 2)

    xb = _dispatch(fill, tiles(dest, ROW_TILE), h2c_p, h2c_s, n_rows, nchunk // 2)
    yb = _experts(blk_e, nused, xb, w_gate, w_up, w_down, layer, n_blocks)
    gates_t = gates.T
    tm = COMBINE_TILE
    x2_p = _combine(alpha, tiles(dest[:, :t_p], tm), x1_p, gates_t[:t_p], mod_p, ln_g, ln_b, yb)
    x2_s = _combine(alpha, tiles(dest[:, t_p:], min(tm, SAMPLE_TILE)), x1_s, gates_t[t_p:], mod_s, ln_g, ln_b, yb)
    return x2_p, x2_s


def _router_weights(w_rg, b_rg, w_re, b_re):
    d = w_rg.shape[0]
    pad = jnp.zeros((SUBLANES - MOE_GROUPS, d), F32)
    w = jnp.concatenate([w_rg.T, pad, w_re.T], axis=0)
    b = jnp.concatenate([b_rg, jnp.zeros((SUBLANES - MOE_GROUPS,), F32), b_re])[:, None]
    hi = w.astype(BF16)
    lo = (w - hi.astype(F32)).astype(BF16)
    return hi, lo, b


def kernel(x_prompt, x_sample, cache_k, cache_v, page_table, c_prompt, c_sample, w_ada, b_ada, ln_g, ln_b,
           attn_w_qkv, attn_w_o, gmlp_w_in, gmlp_vn_g, gmlp_vn_b, gmlp_w_s, gmlp_b_s, gmlp_w_out,
           moe_w_rg, moe_b_rg, moe_w_re, moe_b_re, moe_w_gate, moe_w_up, moe_w_down):
    batch, seq, d = x_prompt.shape
    nseq, dec_seq, _ = x_sample.shape
    depth = w_ada.shape[0]
    dh = d // N_HEADS
    past_len = page_table.shape[1] * cache_k.shape[2]
    alpha = (2.0 * depth) ** 0.25
    tiles_per_seq = seq // ROW_TILE

    xp = x_prompt.reshape(batch * seq, d)
    xs = x_sample.transpose(1, 0, 2).reshape(dec_seq * nseq, d)

    pad_rows = (-(batch + nseq)) % SUBLANES
    c_all = jnp.concatenate([c_prompt, c_sample, jnp.zeros((pad_rows, d), F32)], axis=0)
    m_all = _adaln(c_all, w_ada, b_ada)

    tabs_p = _rope_tables(jnp.arange(seq, dtype=I32), dh)
    tabs_s = _rope_tables(past_len + jnp.repeat(jnp.arange(dec_seq, dtype=I32), nseq), dh)

    new_kp, new_vp, new_ks, new_vs, new_gv = [], [], [], [], []
    for l in range(depth):
        mod_p = _Mod(m_all[l, :batch].reshape(batch, 1, 6 * d), d, tiles_per_seq)
        mod_in = _Mod(mod_p.arr, d, seq // IN_TILE)
        mod_s = _Mod(m_all[l, batch:batch + nseq], d, None)
        lng0, lnb0 = ln_g[l, 0][None, :], ln_b[l, 0][None, :]
        lng1, lnb1 = ln_g[l, 1][None, :], ln_b[l, 1][None, :]
        wr_hi, wr_lo, br = _router_weights(moe_w_rg[l], moe_b_rg[l], moe_w_re[l], moe_b_re[l])
        post = functools.partial(_post, alpha)

        if l % 2 == 0:
            la = l // 2
            w_qkv = attn_w_qkv[la].astype(BF16)
            w_o = attn_w_o[la].astype(BF16)
            q_p, k_p, v_p = _qkv(xp, mod_in, w_qkv, tabs_p, lambda i: i % (seq // IN_TILE), IN_TILE, QKV_COLS, BF16)
            q_s, k_s, v_s = _qkv(xs, mod_s, w_qkv, tabs_s, lambda i: i, SAMPLE_TILE, QKV_COLS, F32, cols_outer=True)
            o_p = _attn_prompt(q_p, k_p, v_p, batch, seq)

            ngrp = N_HEADS // HEAD_GROUP

            def seq_rows(a):
                a = a.reshape(dec_seq, nseq, ngrp, HEAD_GROUP, dh).transpose(1, 2, 0, 3, 4)
                return a.reshape(nseq, ngrp, dec_seq * HEAD_GROUP, dh)

            o_s = _attn_sample(seq_rows(q_s), seq_rows(k_s), seq_rows(v_s), cache_k, cache_v, la, page_table)
            o_s = o_s.reshape(nseq, ngrp, dec_seq, HEAD_GROUP, dh).transpose(2, 0, 1, 3, 4)
            o_s = o_s.reshape(dec_seq * nseq, d).astype(BF16)
            new_kp.append(k_p.reshape(batch, seq, N_HEADS, dh))
            new_vp.append(v_p.reshape(batch, seq, N_HEADS, dh))
            new_ks.append(k_s.reshape(dec_seq, nseq, N_HEADS, dh).transpose(1, 0, 2, 3))
            new_vs.append(v_s.reshape(dec_seq, nseq, N_HEADS, dh).transpose(1, 0, 2, 3))
            x1_p, h2c_p, lg_p = post(_pro_identity, (o_p,), [pl.BlockSpec((ROW_TILE, d), lambda i: (i, 0))],
                                     xp, w_o, mod_p, lng0, lnb0, wr_hi, wr_lo, br, ROW_TILE)
            x1_s, h2c_s, lg_s = post(_pro_identity, (o_s,), [pl.BlockSpec((SAMPLE_TILE, d), lambda i: (i, 0))],
                                     xs, w_o, mod_s, lng0, lnb0, wr_hi, wr_lo, br, SAMPLE_TILE)
        else:
            lb = l // 2
            w_in = gmlp_w_in[lb].astype(BF16)
            w_out = gmlp_w_out[lb].astype(BF16)
            vg, vb = gmlp_vn_g[lb][None, :], gmlp_vn_b[lb][None, :]
            u_p, vn_p = _gmlp_in(xp, mod_in, w_in, vg, vb, IN_TILE, BF16)
            u_s, vn_s = _gmlp_in(xs, mod_s, w_in, vg, vb, SAMPLE_TILE, F32)
            new_gv.append(vn_s.reshape(dec_seq, nseq, d).transpose(1, 0, 2))
            gw = d // GMLP_GROUPS
            bexp = jnp.repeat(gmlp_b_s[lb].T, gw, axis=1)
            wexp = jnp.repeat(gmlp_w_s[lb][:, :dec_seq, :dec_seq].transpose(1, 2, 0).reshape(dec_seq * dec_seq, -1),
                              gw, axis=1)
            tile = pl.BlockSpec((ROW_TILE, d), lambda i: (i, 0))
            x1_p, h2c_p, lg_p = post(
                _pro_gmlp_prompt, (u_p, vn_p, gmlp_w_s[lb], bexp),
                [tile, tile, pl.BlockSpec(gmlp_w_s[lb].shape, lambda i: (0, 0, 0)),
                 pl.BlockSpec(bexp.shape, lambda i: (0, 0))],
                xp, w_out, mod_p, lng0, lnb0, wr_hi, wr_lo, br, ROW_TILE)
            x1_s, h2c_s, lg_s = post(
                _pro_gmlp_sample, (u_s, vn_s, wexp, bexp[:SUBLANES]),
                [pl.BlockSpec((SAMPLE_TILE, d), lambda i: (i, 0)),
                 pl.BlockSpec(vn_s.shape, lambda i: (0, 0)),
                 pl.BlockSpec(wexp.shape, lambda i: (0, 0)),
                 pl.BlockSpec((SUBLANES, d), lambda i: (0, 0))],
                xs, w_out, mod_s, lng0, lnb0, wr_hi, wr_lo, br, SAMPLE_TILE)

        xp, xs = _moe_and_norm(alpha, x1_p, x1_s, h2c_p, h2c_s, lg_p, lg_s, mod_p, mod_s, lng1, lnb1,
                               moe_w_gate, moe_w_up, moe_w_down, l)

    y_p = xp.reshape(batch, seq, d)
    y_s = xs.reshape(dec_seq, nseq, d).transpose(1, 0, 2)
    return (y_p, y_s, jnp.stack(new_kp), jnp.stack(new_vp), jnp.stack(new_ks), jnp.stack(new_vs),
            jnp.stack(new_gv))
```

```python
import functools

import jax
import jax.numpy as jnp
from jax import lax
from jax.experimental import pallas as pl
from jax.experimental.pallas import tpu as pltpu

F32, BF16, I32 = jnp.float32, jnp.bfloat16, jnp.int32

N_HEADS = 16
MOBA_BLOCK = 256
MOBA_TOPK = 3
ROPE_THETA = 500000.0
GMLP_CHUNK = 128
GMLP_GROUPS = 8
MOE_GROUPS = 4
MOE_EXPERTS_PER_GROUP = 8
LN_EPS = 1e-5

LANES = 128
SUBLANES = 8
VMEM_LIMIT = 56 * 1024 * 1024

ROW_TILE = 512
IN_TILE = 1024
QKV_COLS = 512
GMLP_IN_COLS = 512
SAMPLE_TILE = 128
MOE_ROWS = 128
COMBINE_TILE = 512
PAGES_PER_STEP = 8

NT_DIMS = (((1,), (1,)), ((), ()))


def _cparams(sem):
    return pltpu.CompilerParams(dimension_semantics=sem, vmem_limit_bytes=VMEM_LIMIT)


def _layer_norm(z, g, b):
    mu = jnp.mean(z, axis=-1, keepdims=True)
    zc = z - mu
    var = jnp.mean(zc * zc, axis=-1, keepdims=True)
    return zc * lax.rsqrt(var + LN_EPS) * g + b


def _adaln_body(c_ref, w_ref, b_ref, o_ref):
    a = jax.nn.silu(c_ref[...]).astype(BF16)
    o_ref[...] = jnp.dot(a, w_ref[...].astype(BF16), preferred_element_type=F32) + b_ref[...]


def _adaln(c_all, w_ada, b_ada):
    depth, d, n = w_ada.shape
    rows = c_all.shape[0]
    tn = 1024
    return pl.pallas_call(
        _adaln_body,
        grid=(depth, n // tn),
        in_specs=[
            pl.BlockSpec((rows, d), lambda l, j: (0, 0)),
            pl.BlockSpec((None, d, tn), lambda l, j: (l, 0, j)),
            pl.BlockSpec((None, 1, tn), lambda l, j: (l, 0, j)),
        ],
        out_specs=pl.BlockSpec((None, rows, tn), lambda l, j: (l, 0, j)),
        out_shape=jax.ShapeDtypeStruct((depth, rows, n), F32),
        compiler_params=_cparams(("arbitrary", "arbitrary")),
        name="adaln",
    )(c_all, w_ada, b_ada.reshape(depth, 1, n))


class _Mod:
    def __init__(self, arr, d, tiles_per_seq):
        self.arr = arr
        self.d = d
        self.tiles_per_seq = tiles_per_seq

    def spec(self, k, ngrid):
        d, tps = self.d, self.tiles_per_seq
        if tps is None:
            rows = self.arr.shape[0]
            if ngrid == 1:
                return pl.BlockSpec((rows, d), lambda i: (0, k))
            if ngrid == 2:
                return pl.BlockSpec((rows, d), lambda i, j: (0, k))
            return pl.BlockSpec((rows, d), lambda i, p, j: (0, k))
        if ngrid == 1:
            return pl.BlockSpec((None, 1, d), lambda i: (i // tps, 0, k))
        if ngrid == 2:
            return pl.BlockSpec((None, 1, d), lambda i, j: (i // tps, 0, k))
        return pl.BlockSpec((None, 1, d), lambda i, p, j: (i // tps, 0, k))


def _qkv_body(cols_outer, x_ref, sh_ref, sc_ref, wq_ref, wk_ref, wv_ref, c_ref, s1_ref, s2_ref,
              q_ref, k_ref, v_ref, h_scr):
    def modulate():
        h_scr[...] = (x_ref[...] * (1.0 + sc_ref[...]) + sh_ref[...]).astype(BF16)

    if cols_outer:
        modulate()
    else:
        pl.when(pl.program_id(1) == 0)(modulate)

    tn = q_ref.shape[1]
    reps = tn // LANES
    c = jnp.tile(c_ref[...], (1, reps))
    s1 = jnp.tile(s1_ref[...], (1, reps))
    s2 = jnp.tile(s2_ref[...], (1, reps))
    half = c_ref.shape[1] // 8

    def rope(y):
        return y * c + pltpu.roll(y, tn - half, 1) * s1 + pltpu.roll(y, half, 1) * s2

    h = h_scr[...]
    q_ref[...] = rope(jnp.dot(h, wq_ref[...], preferred_element_type=F32)).astype(q_ref.dtype)
    k_ref[...] = rope(jnp.dot(h, wk_ref[...], preferred_element_type=F32))
    v_ref[...] = jnp.dot(h, wv_ref[...], preferred_element_type=F32)


def _qkv(x, mod, w_qkv_bf16, tabs, tab_index, tm, tn, q_dtype, cols_outer=False):
    rows, d = x.shape
    ncol = d // tn
    c_tab, s1_tab, s2_tab = tabs
    assert not cols_outer or mod.tiles_per_seq is None

    def spec(block, index):
        if cols_outer:
            return pl.BlockSpec(block, lambda j, i: index(i, j))
        return pl.BlockSpec(block, index)

    tab_spec = spec((tm, LANES), lambda i, j: (tab_index(i), 0))
    out = spec((tm, tn), lambda i, j: (i, j))

    def w_spec(part):
        return spec((d, tn), lambda i, j: (0, part * ncol + j))

    return pl.pallas_call(
        functools.partial(_qkv_body, cols_outer),
        grid=(ncol, rows // tm) if cols_outer else (rows // tm, ncol),
        in_specs=[
            spec((tm, d), lambda i, j: (i, 0)),
            mod.spec(0, 2),
            mod.spec(1, 2),
            w_spec(0), w_spec(1), w_spec(2),
            tab_spec, tab_spec, tab_spec,
        ],
        out_specs=[out, out, out],
        out_shape=[
            jax.ShapeDtypeStruct((rows, d), q_dtype),
            jax.ShapeDtypeStruct((rows, d), F32),
            jax.ShapeDtypeStruct((rows, d), F32),
        ],
        scratch_shapes=[pltpu.VMEM((tm, d), BF16)],
        compiler_params=_cparams(("arbitrary", "arbitrary")),
        name="qkv_rope",
    )(x, mod.arr, mod.arr, w_qkv_bf16, w_qkv_bf16, w_qkv_bf16, c_tab, s1_tab, s2_tab)


def _rope_tables(pos, head_dim):
    rot = head_dim // 4
    half = rot // 2
    inv = ROPE_THETA ** (-jnp.arange(half, dtype=F32) * 2.0 / rot)
    ang = pos.astype(F32)[:, None] * inv[None, :]
    cos, sin = jnp.cos(ang), jnp.sin(ang)
    n = pos.shape[0]
    ones = jnp.ones((n, head_dim - rot), F32)
    zeros = jnp.zeros((n, head_dim - rot), F32)
    zh = jnp.zeros((n, half), F32)
    c = jnp.concatenate([cos, cos, ones], axis=1)
    s1 = jnp.concatenate([-sin, zh, zeros], axis=1)
    s2 = jnp.concatenate([zh, sin, zeros], axis=1)
    return c, s1, s2


def _attn_prompt_body(q_ref, k_ref, v_ref, o_ref, kb_scr, vb_scr, sel_scr, qs_scr):
    seq, dh = q_ref.shape
    nblk = seq // MOBA_BLOCK
    scale = dh ** -0.5
    kf = k_ref[...]
    kb_scr[...] = kf.astype(BF16)
    vb_scr[:, :dh] = v_ref[...].astype(BF16)
    vb_scr[:, dh:] = jnp.ones((seq, dh), BF16)

    km = kf.reshape(nblk, MOBA_BLOCK, dh).sum(axis=1) * (1.0 / MOBA_BLOCK)
    g = lax.dot_general(km, q_ref[...].astype(F32), NT_DIMS, precision=lax.Precision.HIGHEST,
                        preferred_element_type=F32)
    blk = lax.broadcasted_iota(I32, (nblk, seq), 0)
    cur = lax.broadcasted_iota(I32, (nblk, seq), 1) // MOBA_BLOCK
    past = blk < cur
    g = jnp.where(past, g, -jnp.inf)
    cnt = jnp.zeros((nblk, seq), F32)
    for k in range(nblk):
        gk = g[k:k + 1, :]
        beats = (gk > g) | ((gk == g) & (k < blk))
        cnt = cnt + beats.astype(F32)
    sel = ((cnt < MOBA_TOPK) & past).astype(F32)
    sel_scr[...] = jnp.concatenate([sel, jnp.zeros((LANES - nblk, seq), F32)], axis=0).T

    row = lax.broadcasted_iota(I32, (MOBA_BLOCK, MOBA_BLOCK), 0)
    col = lax.broadcasted_iota(I32, (MOBA_BLOCK, MOBA_BLOCK), 1)
    causal = row >= col

    qs_scr[...] = (q_ref[...].astype(F32) * scale).astype(BF16)

    for c in range(nblk):
        lo, hi = c * MOBA_BLOCK, (c + 1) * MOBA_BLOCK
        qc = qs_scr[lo:hi, :]
        selc = sel_scr[lo:hi, :]
        slabs = []
        for j in range(c + 1):
            s = lax.dot_general(qc, kb_scr[j * MOBA_BLOCK:(j + 1) * MOBA_BLOCK, :], NT_DIMS,
                                preferred_element_type=F32)
            if j < c:
                s = jnp.where(selc[:, j:j + 1] > 0.5, s, -jnp.inf)
            else:
                s = jnp.where(causal, s, -jnp.inf)
            slabs.append(s)
        mx = slabs[0]
        for s in slabs[1:]:
            mx = jnp.maximum(mx, s)
        m = mx.max(axis=1, keepdims=True)
        acc = jnp.zeros((MOBA_BLOCK, 2 * dh), F32)
        for j, s in enumerate(slabs):
            p = jnp.exp(s - m)
            acc = acc + jnp.dot(p.astype(BF16), vb_scr[j * MOBA_BLOCK:(j + 1) * MOBA_BLOCK, :],
                                preferred_element_type=F32)
        o_ref[lo:hi, :] = (acc[:, :dh] / acc[:, dh:]).astype(o_ref.dtype)


def _attn_prompt(q, k, v, batch, seq):
    rows, d = q.shape
    dh = d // N_HEADS
    spec = pl.BlockSpec((seq, dh), lambda b, h: (b, h))
    return pl.pallas_call(
        _attn_prompt_body,
        grid=(batch, N_HEADS),
        in_specs=[spec, spec, spec],
        out_specs=spec,
        out_shape=jax.ShapeDtypeStruct((rows, d), BF16),
        scratch_shapes=[pltpu.VMEM((seq, dh), BF16), pltpu.VMEM((seq, 2 * dh), BF16), pltpu.VMEM((seq, LANES), F32),
                        pltpu.VMEM((seq, dh), BF16)],
        compiler_params=_cparams(("arbitrary", "arbitrary")),
        name="moba_prompt",
    )(q, k, v)


HEAD_GROUP = SUBLANES


def _attn_sample_body(pt_ref, q_ref, kn_ref, vn_ref, hm_ref, *rest):
    npg = PAGES_PER_STEP
    k_refs = rest[:npg]
    v_refs = rest[npg:2 * npg]
    o_ref = rest[2 * npg]
    s_scr, p_scr, ksum_scr, bmax_scr, acc_scr, linv_scr = rest[2 * npg + 1:]

    t = pl.program_id(1)
    n_pages, ngrp, nq, grp_rows = s_scr.shape
    hg = HEAD_GROUP
    k_steps = n_pages // npg
    dh = q_ref.shape[2]
    page = grp_rows // hg
    pages_per_blk = MOBA_BLOCK // page
    blk_per_step = npg // pages_per_blk
    nblk = n_pages // pages_per_blk
    scale = dh ** -0.5

    @pl.when(t < k_steps)
    def _():
        hm = hm_ref[...]
        sums = []
        pmax = [[] for _ in range(ngrp)]
        for i in range(npg):
            kp = k_refs[i][...]
            sums.append(jnp.sum(kp, axis=0))
            for g in range(ngrp):
                kb = kp[:, g * hg:(g + 1) * hg, :].reshape(grp_rows, dh).astype(BF16)
                s = lax.dot_general(q_ref[g].astype(BF16), kb, NT_DIMS, preferred_element_type=F32) * scale + hm
                s_scr[t * npg + i, g] = s
                pmax[g].append(s.max(axis=1, keepdims=True))
        for b in range(blk_per_step):
            j = t * blk_per_step + b
            tot = sums[b * pages_per_blk]
            for i in range(1, pages_per_blk):
                tot = tot + sums[b * pages_per_blk + i]
            ksum_scr[j] = tot
            for g in range(ngrp):
                bm = pmax[g][b * pages_per_blk]
                for i in range(1, pages_per_blk):
                    bm = jnp.maximum(bm, pmax[g][b * pages_per_blk + i])
                bmax_scr[j, g] = jnp.broadcast_to(bm, (nq, LANES))

    @pl.when(t == k_steps - 1)
    def _():
        gw = nblk * hg
        width = ngrp * gw
        km = jnp.concatenate(
            [ksum_scr[:, g * hg:(g + 1) * hg, :].reshape(gw, dh) for g in range(ngrp)], axis=0) * (1.0 / MOBA_BLOCK)
        qall = jnp.concatenate([q_ref[g] for g in range(ngrp)], axis=0)
        gate = lax.dot_general(qall, km, NT_DIMS, precision=lax.Precision.HIGHEST, preferred_element_type=F32)
        row = lax.broadcasted_iota(I32, (ngrp * nq, width), 0)
        lane = lax.broadcasted_iota(I32, (ngrp * nq, width), 1)
        own = ((row // nq) == (lane // gw)) & ((row % hg) == (lane % hg))
        blk = (lane % gw) // hg
        gate = jnp.where(own, gate, -jnp.inf)
        cnt = jnp.zeros((ngrp * nq, width), F32)
        for dlt in range(1, nblk):
            lower = pltpu.roll(gate, dlt * hg, 1)
            cnt = cnt + ((blk >= dlt) & (lower >= gate)).astype(F32)
            upper = pltpu.roll(gate, width - dlt * hg, 1)
            cnt = cnt + ((blk + dlt < nblk) & (upper > gate)).astype(F32)
        sel = jnp.where(own & (cnt < MOBA_TOPK), 1.0, 0.0)

        r = lax.broadcasted_iota(I32, (nq, nq), 0)
        c = lax.broadcasted_iota(I32, (nq, nq), 1)
        new_ok = ((r % hg) == (c % hg)) & ((c // hg) <= (r // hg))
        for g in range(ngrp):
            qb = q_ref[g].astype(BF16)
            sn = lax.dot_general(qb, kn_ref[g].astype(BF16), NT_DIMS, preferred_element_type=F32) * scale
            sn = jnp.where(new_ok, sn, -jnp.inf)
            m = sn.max(axis=1, keepdims=True)
            biases = []
            for j in range(nblk):
                lo = g * gw + j * hg
                picked = sel[g * nq:(g + 1) * nq, lo:lo + hg].max(axis=1, keepdims=True)
                bias = jnp.where(picked > 0.5, 0.0, -jnp.inf)
                biases.append(bias)
                m = jnp.maximum(m, bmax_scr[j, g][:, 0:1] + bias)
            en = jnp.exp(sn - m)
            l = en.sum(axis=1, keepdims=True)
            for pg in range(n_pages):
                e = jnp.exp(s_scr[pg, g] + (biases[pg // pages_per_blk] - m))
                l = l + e.sum(axis=1, keepdims=True)
                p_scr[pg, g] = e.astype(BF16)
            linv_scr[g] = jnp.broadcast_to(1.0 / l, (nq, dh))
            acc_scr[g] = jnp.dot(en.astype(BF16), vn_ref[g].astype(BF16), preferred_element_type=F32)

    @pl.when(t >= k_steps)
    def _():
        accs = [acc_scr[g] for g in range(ngrp)]
        for i in range(npg):
            vp = v_refs[i][...]
            for g in range(ngrp):
                vb = vp[:, g * hg:(g + 1) * hg, :].reshape(grp_rows, dh).astype(BF16)
                accs[g] = accs[g] + jnp.dot(p_scr[(t - k_steps) * npg + i, g], vb, preferred_element_type=F32)
        for g in range(ngrp):
            acc_scr[g] = accs[g]

    @pl.when(t == 2 * k_steps - 1)
    def _():
        o_ref[...] = acc_scr[...] * linv_scr[...]


def _attn_sample(q_rows, k_new, v_new, cache_k, cache_v, layer, page_table):
    nseq, ngrp, nq, dh = q_rows.shape
    _, n_phys, page, heads, _ = cache_k.shape
    n_pages = page_table.shape[1]
    npg = PAGES_PER_STEP
    k_steps = n_pages // npg
    grp_rows = page * HEAD_GROUP
    rowh = jnp.arange(nq, dtype=I32)[:, None] % HEAD_GROUP
    colh = jnp.arange(grp_rows, dtype=I32)[None, :] % HEAD_GROUP
    head_mask = jnp.where(rowh == colh, 0.0, -jnp.inf).astype(F32)

    def kmap(i):
        return lambda b, t, pt: (layer, pt[b * n_pages + jnp.minimum(t, k_steps - 1) * npg + i], 0, 0, 0)

    def vmap_(i):
        def index(b, t, pt):
            in_v = t >= k_steps
            seq_i = jnp.where(in_v, b, jnp.maximum(b - 1, 0))
            step = jnp.where(in_v, t - k_steps, k_steps - 1)
            return (layer, pt[seq_i * n_pages + step * npg + i], 0, 0, 0)
        return index

    seq_spec = pl.BlockSpec((None, ngrp, nq, dh), lambda b, t, pt: (b, 0, 0, 0))
    page_block = (None, None, page, heads, dh)
    nblk = n_pages * page // MOBA_BLOCK
    grid_spec = pltpu.PrefetchScalarGridSpec(
        num_scalar_prefetch=1,
        grid=(nseq, 2 * k_steps),
        in_specs=[seq_spec, seq_spec, seq_spec,
                  pl.BlockSpec((nq, grp_rows), lambda b, t, pt: (0, 0))]
                 + [pl.BlockSpec(page_block, kmap(i)) for i in range(npg)]
                 + [pl.BlockSpec(page_block, vmap_(i)) for i in range(npg)],
        out_specs=seq_spec,
        scratch_shapes=[
            pltpu.VMEM((n_pages, ngrp, nq, grp_rows), F32),
            pltpu.VMEM((n_pages, ngrp, nq, grp_rows), BF16),
            pltpu.VMEM((nblk, heads, dh), F32),
            pltpu.VMEM((nblk, ngrp, nq, LANES), F32),
            pltpu.VMEM((ngrp, nq, dh), F32),
            pltpu.VMEM((ngrp, nq, dh), F32),
        ],
    )
    return pl.pallas_call(
        _attn_sample_body,
        grid_spec=grid_spec,
        out_shape=jax.ShapeDtypeStruct((nseq, ngrp, nq, dh), F32),
        compiler_params=_cparams(("arbitrary", "arbitrary")),
        name="moba_sample",
    )(page_table.reshape(-1), q_rows, k_new, v_new, head_mask,
      *([cache_k] * npg), *([cache_v] * npg))


def _post_body(alpha, prologue, n_pro, *refs):
    pro_refs = refs[:n_pro]
    (x_ref, w_ref, gt_ref, lng_ref, lnb_ref, sh_ref, sc_ref, wrh_ref, wrl_ref, br_ref,
     x1_ref, h2c_ref, lg_ref) = refs[n_pro:]
    a = prologue(*pro_refs)
    y = jnp.dot(a, w_ref[...], preferred_element_type=F32)
    z = alpha * x_ref[...] + (1.0 + gt_ref[...]) * y
    x1 = _layer_norm(z, lng_ref[...], lnb_ref[...])
    x1_ref[...] = x1
    h2 = x1 * (1.0 + sc_ref[...]) + sh_ref[...]
    tm, d = h2.shape
    hi = h2.astype(BF16)
    hi32 = hi.astype(F32)
    bits = lax.bitcast_convert_type(hi32, I32)
    packed = bits[:, :d // 2] | lax.shift_right_logical(bits[:, d // 2:], 16)
    nchunk = d // (2 * LANES)
    for c in range(nchunk):
        h2c_ref[pl.ds(c, tm, stride=nchunk), :] = packed[:, c * LANES:(c + 1) * LANES]
    lo = (h2 - hi32).astype(BF16)
    wh, wl = wrh_ref[...], wrl_ref[...]
    lg = (lax.dot_general(wh, hi, NT_DIMS, preferred_element_type=F32)
          + lax.dot_general(wl, hi, NT_DIMS, preferred_element_type=F32)
          + lax.dot_general(wh, lo, NT_DIMS, preferred_element_type=F32))
    lg_ref[...] = lg + br_ref[...]


def _pro_identity(a_ref):
    return a_ref[...]


def _pro_gmlp_prompt(u_ref, v_ref, ws_ref, bexp_ref):
    tm, d = u_ref.shape
    gw = d // GMLP_GROUPS
    row = lax.broadcasted_iota(I32, (GMLP_CHUNK, GMLP_CHUNK), 0)
    col = lax.broadcasted_iota(I32, (GMLP_CHUNK, GMLP_CHUNK), 1)
    tri = row >= col
    wms = [jnp.where(tri, ws_ref[g], 0.0).astype(BF16) for g in range(GMLP_GROUPS)]
    outs = []
    for ch in range(tm // GMLP_CHUNK):
        lo, hi = ch * GMLP_CHUNK, (ch + 1) * GMLP_CHUNK
        cols = [jnp.dot(wms[g], v_ref[lo:hi, g * gw:(g + 1) * gw], preferred_element_type=F32)
                for g in range(GMLP_GROUPS)]
        mixed = jnp.concatenate(cols, axis=1) + bexp_ref[...]
        outs.append((u_ref[lo:hi, :].astype(F32) * mixed).astype(BF16))
    return jnp.concatenate(outs, axis=0)


def _pro_gmlp_sample(u_ref, v_ref, wexp_ref, bexp_ref):
    s = pl.program_id(0)
    tm = u_ref.shape[0]
    n_pos = v_ref.shape[0] // tm
    mixed = bexp_ref[pl.ds(s, 1), :]
    for j in range(n_pos):
        w = wexp_ref[pl.ds(s * n_pos + j, 1), :] * (j <= s).astype(F32)
        mixed = mixed + w * v_ref[j * tm:(j + 1) * tm, :]
    return (u_ref[...].astype(F32) * mixed).astype(BF16)


def _post(alpha, prologue, pro_args, pro_specs, x, w_bf16, mod, ln_g, ln_b, wr_hi, wr_lo, br, tm):
    rows, d = x.shape
    ne = wr_hi.shape[0]
    nchunk = d // (2 * LANES)
    vec = pl.BlockSpec((1, d), lambda i: (0, 0))
    return pl.pallas_call(
        functools.partial(_post_body, alpha, prologue, len(pro_args)),
        grid=(rows // tm,),
        in_specs=list(pro_specs) + [
            pl.BlockSpec((tm, d), lambda i: (i, 0)),
            pl.BlockSpec(w_bf16.shape, lambda i: (0, 0)),
            mod.spec(2, 1), vec, vec, mod.spec(3, 1), mod.spec(4, 1),
            pl.BlockSpec((ne, d), lambda i: (0, 0)),
            pl.BlockSpec((ne, d), lambda i: (0, 0)),
            pl.BlockSpec((ne, 1), lambda i: (0, 0)),
        ],
        out_specs=[
            pl.BlockSpec((tm, d), lambda i: (i, 0)),
            pl.BlockSpec((tm * nchunk, LANES), lambda i: (i, 0)),
            pl.BlockSpec((ne, tm), lambda i: (0, i)),
        ],
        out_shape=[
            jax.ShapeDtypeStruct((rows, d), F32),
            jax.ShapeDtypeStruct((rows * nchunk, LANES), I32),
            jax.ShapeDtypeStruct((ne, rows), F32),
        ],
        compiler_params=_cparams(("arbitrary",)),
        name="proj_postnorm_router",
    )(*pro_args, x, w_bf16, mod.arr, ln_g, ln_b, mod.arr, mod.arr, wr_hi, wr_lo, br)


def _route_body(lg_ref, eid_ref, gate_ref, dest_ref, meta_ref):
    ne = MOE_GROUPS * MOE_EXPERTS_PER_GROUP
    epg = MOE_EXPERTS_PER_GROUP
    lgt = lg_ref[...]
    t = lgt.shape[1]
    row8 = lax.broadcasted_iota(I32, (SUBLANES, t), 0)
    lgp = jnp.where(row8 < MOE_GROUPS, lgt[0:SUBLANES], -jnp.inf)
    mg = lgp.max(axis=0, keepdims=True)
    grp = jnp.where(lgp == mg, row8, SUBLANES).min(axis=0, keepdims=True)
    pg = 1.0 / jnp.exp(lgp - mg).sum(axis=0, keepdims=True)

    le = jnp.zeros((epg, t), F32)
    for g in range(MOE_GROUPS):
        le = jnp.where(grp == g, lgt[SUBLANES + g * epg:SUBLANES + (g + 1) * epg], le)
    m1 = le.max(axis=0, keepdims=True)
    i1 = jnp.where(le == m1, row8, epg).min(axis=0, keepdims=True)
    le2 = jnp.where(row8 == i1, -jnp.inf, le)
    m2 = le2.max(axis=0, keepdims=True)
    i2 = jnp.where(le2 == m2, row8, epg).min(axis=0, keepdims=True)
    e = jnp.exp(m2 - m1)
    g1 = pg / (1.0 + e)
    g2 = pg * e / (1.0 + e)
    e1 = grp * epg + i1
    e2 = grp * epg + i2
    eid_ref[0:1, :] = e1
    eid_ref[1:2, :] = e2
    gate_ref[0:1, :] = g1
    gate_ref[1:2, :] = g2

    rows = lax.broadcasted_iota(I32, (ne, t), 0)
    oh1 = (rows == e1).astype(F32)
    oh2 = (rows == e2).astype(F32)
    oh = (oh1 + oh2).astype(BF16)
    cw = 512
    ur = lax.broadcasted_iota(I32, (cw, cw), 0)
    uc = lax.broadcasted_iota(I32, (cw, cw), 1)
    upper = (ur < uc).astype(BF16)
    carry = jnp.zeros((ne, 1), F32)
    pref = []
    for ci in range(t // cw):
        ohc = oh[:, ci * cw:(ci + 1) * cw]
        pref.append(jnp.dot(ohc, upper, preferred_element_type=F32) + carry)
        carry = carry + ohc.astype(F32).sum(axis=1, keepdims=True)
    cnt_before = jnp.concatenate(pref, axis=1)

    counts = carry
    shift = MOE_ROWS.bit_length() - 1
    padded = (((counts.astype(I32) + (MOE_ROWS - 1)) >> shift) << shift).astype(F32)
    lr = lax.broadcasted_iota(I32, (ne, ne), 0)
    lc = lax.broadcasted_iota(I32, (ne, ne), 1)
    lower = (lc < lr).astype(F32)
    pstart = jnp.dot(lower, jnp.broadcast_to(padded, (ne, LANES)), precision=lax.Precision.HIGHEST,
                     preferred_element_type=F32)[:, 0:1]
    pend = pstart + padded
    base = pstart + cnt_before
    d1 = (oh1 * base).sum(axis=0, keepdims=True)
    d2 = (oh2 * base).sum(axis=0, keepdims=True)
    dest_ref[0:1, :] = d1.astype(I32)
    dest_ref[1:2, :] = d2.astype(I32)

    mw = meta_ref.shape[1]
    blk_start = (lax.broadcasted_iota(I32, (ne, mw), 1) * MOE_ROWS).astype(F32)
    blk_e = jnp.minimum((pend <= blk_start).astype(F32).sum(axis=0, keepdims=True), ne - 1.0)
    last = lax.broadcasted_iota(I32, (ne, mw), 0) == ne - 1
    nused = jnp.where(last, jnp.broadcast_to(pend, (ne, mw)), 0.0).sum(axis=0, keepdims=True) * (1.0 / MOE_ROWS)
    lane_e = lax.broadcasted_iota(I32, (ne, mw), 1)
    diag = lane_e == lax.broadcasted_iota(I32, (ne, mw), 0)
    fill_lo = jnp.where(diag, jnp.broadcast_to(pstart + counts, (ne, mw)), 0.0).sum(axis=0, keepdims=True)
    fill_hi = jnp.where(diag, jnp.broadcast_to(pend, (ne, mw)), 0.0).sum(axis=0, keepdims=True)
    mrow = lax.broadcasted_iota(I32, (SUBLANES, mw), 0)
    meta = jnp.where(mrow == 0, blk_e, jnp.where(mrow == 1, nused, jnp.where(mrow == 2, fill_lo, fill_hi)))
    meta_ref[...] = meta.astype(I32)


def _route(logits_t, n_blocks_max):
    ne_pad, t = logits_t.shape
    mw = 256
    assert n_blocks_max <= mw
    return pl.pallas_call(
        _route_body,
        out_shape=[
            jax.ShapeDtypeStruct((2, t), I32),
            jax.ShapeDtypeStruct((2, t), F32),
            jax.ShapeDtypeStruct((2, t), I32),
            jax.ShapeDtypeStruct((SUBLANES, mw), I32),
        ],
        compiler_params=pltpu.CompilerParams(vmem_limit_bytes=VMEM_LIMIT),
        name="moe_route",
    )(logits_t)


def _dispatch_body(nchunk, n_prompt_tiles, fill_ref, dest_ref, hp_ref, hs_ref, xb_ref, zero_scr, sem):
    i = pl.program_id(0)
    tm = dest_ref.shape[1]
    n_ranges = fill_ref.shape[1]

    def slab(ref, row):
        return ref.at[pl.ds(pl.multiple_of(row * nchunk, nchunk), nchunk)]

    def issue(src_ref):
        def body(tok, carry):
            for k in range(2):
                pltpu.make_async_copy(slab(src_ref, tok), slab(xb_ref, dest_ref[k, tok]), sem).start()
            return carry
        lax.fori_loop(0, tm, body, 0, unroll=8)

    @pl.when(i < n_prompt_tiles)
    def _():
        issue(hp_ref)

    @pl.when(i >= n_prompt_tiles)
    def _():
        issue(hs_ref)

    for _ in range(2):
        pltpu.make_async_copy(hp_ref, xb_ref.at[pl.ds(0, tm * nchunk)], sem).wait()

    @pl.when(i == 0)
    def _():
        zero_scr[...] = jnp.zeros_like(zero_scr)
        run_sizes = [MOE_ROWS >> (b + 1) for b in range(MOE_ROWS.bit_length() - 1)]

        def run_copy(row, n_rows):
            start = pl.multiple_of(row * nchunk, nchunk)
            return pltpu.make_async_copy(zero_scr.at[pl.ds(0, n_rows * nchunk)],
                                         xb_ref.at[pl.ds(start, n_rows * nchunk)], sem)

        def expert_runs(action):
            for e in range(n_ranges - 1):
                row = fill_ref[0, e]
                n = fill_ref[1, e] - row
                for size in run_sizes:
                    take = (n & size) != 0
                    pl.when(take)(functools.partial(action, row, size))
                    row = row + jnp.where(take, size, 0)

        tail_lo = fill_ref[0, n_ranges - 1]
        tail_blocks = (fill_ref[1, n_ranges - 1] - tail_lo) >> (MOE_ROWS.bit_length() - 1)

        def tail_runs(action):
            def body(b, carry):
                action(tail_lo + b * MOE_ROWS, MOE_ROWS)
                return carry
            lax.fori_loop(0, tail_blocks, body, 0)

        for runs in (expert_runs, tail_runs):
            runs(lambda row, size: run_copy(row, size).start())
        for runs in (expert_runs, tail_runs):
            runs(lambda row, size: run_copy(row, size).wait())


def _dispatch(fill, dest_tiles, h2c_p, h2c_s, n_rows, nchunk):
    n_tiles, _, tm = dest_tiles.shape
    n_prompt_tiles = h2c_p.shape[0] // (nchunk * tm)
    grid_spec = pltpu.PrefetchScalarGridSpec(
        num_scalar_prefetch=1,
        grid=(n_tiles,),
        in_specs=[
            pl.BlockSpec((None, 2, tm), lambda i, f: (i, 0, 0), memory_space=pltpu.SMEM),
            pl.BlockSpec((tm * nchunk, LANES), lambda i, f: (jnp.minimum(i, n_prompt_tiles - 1), 0)),
            pl.BlockSpec((tm * nchunk, LANES), lambda i, f: (jnp.maximum(i - n_prompt_tiles, 0), 0)),
        ],
        out_specs=pl.BlockSpec(memory_space=pl.ANY),
        scratch_shapes=[pltpu.VMEM((MOE_ROWS * nchunk, LANES), h2c_p.dtype), pltpu.SemaphoreType.DMA(())],
    )
    return pl.pallas_call(
        functools.partial(_dispatch_body, nchunk, n_prompt_tiles),
        grid_spec=grid_spec,
        out_shape=jax.ShapeDtypeStruct((n_rows * nchunk, LANES), h2c_p.dtype),
        compiler_params=_cparams(("arbitrary",)),
        name="moe_dispatch",
    )(fill, dest_tiles, h2c_p, h2c_s)


def _expert_body(layer, blk_ref, nused_ref, ord_ref, uniq_ref, ndist_ref, x_ref, wg_hbm, wu_hbm, wd_hbm, y_ref,
                 wg_buf, wu_buf, wd_buf, wg_scr, wu_scr, wd_scr, sem):
    i = pl.program_id(0)
    npack = x_ref.shape[0] // MOE_ROWS
    pairs = ((wg_hbm, wg_buf), (wu_hbm, wu_buf), (wd_hbm, wd_buf))

    def copies(n, slot):
        e = uniq_ref[n]
        return [pltpu.make_async_copy(src.at[layer, e], dst.at[slot], sem.at[slot]) for src, dst in pairs]

    @pl.when(i < nused_ref[0])
    def _():
        prev = blk_ref[jnp.maximum(i - 1, 0)]

        @pl.when(i == 0)
        def _():
            for cp in copies(0, 0):
                cp.start()

        @pl.when((i == 0) | (blk_ref[i] != prev))
        def _():
            n = ord_ref[i]
            slot = n % 2
            for cp in copies(n, slot):
                cp.wait()

            @pl.when(n + 1 < ndist_ref[0])
            def _():
                for cp in copies(n + 1, 1 - slot):
                    cp.start()

            wg_scr[...] = wg_buf[slot].astype(BF16)
            wu_scr[...] = wu_buf[slot].astype(BF16)
            wd_scr[...] = wd_buf[slot].astype(BF16)

        words = jnp.concatenate([x_ref[pl.ds(c, MOE_ROWS, stride=npack), :] for c in range(npack)], axis=1)
        x_hi = lax.bitcast_convert_type(words & jnp.int32(-65536), F32)
        x_lo = lax.bitcast_convert_type(lax.shift_left(words, 16), F32)
        x = jnp.concatenate([x_hi, x_lo], axis=1).astype(BF16)
        g = jnp.dot(x, wg_scr[...], preferred_element_type=F32)
        u = jnp.dot(x, wu_scr[...], preferred_element_type=F32)
        a = (jax.nn.silu(g) * u).astype(BF16)
        y = jnp.dot(a, wd_scr[...], preferred_element_type=F32)
        ybits = lax.bitcast_convert_type(y.astype(BF16).astype(F32), I32)
        half = y.shape[1] // 2
        packed = ybits[:, :half] | lax.shift_right_logical(ybits[:, half:], 16)
        for c in range(npack):
            y_ref[pl.ds(c, MOE_ROWS, stride=npack), :] = packed[:, c * LANES:(c + 1) * LANES]

    @pl.when(i >= nused_ref[0])
    def _():
        y_ref[...] = jnp.zeros_like(y_ref)


def _experts(blk_e, nused, xb, w_gate, w_up, w_down, layer, n_blocks):
    _, ne, d, de = w_gate.shape
    rows_blk = MOE_ROWS * (d // (2 * LANES))

    ids = jnp.arange(n_blocks, dtype=I32)
    change = (ids < nused[0]) & ((ids == 0) | (blk_e != jnp.roll(blk_e, 1)))
    ordinal = jnp.cumsum(change.astype(I32)) - 1
    n_distinct = change.astype(I32).sum()[None]
    uniq = jnp.zeros((ne + 1,), I32).at[jnp.where(change, ordinal, ne)].set(blk_e)

    def blk(i, be, nu, *_):
        return jnp.minimum(i, nu[0] - 1)

    grid_spec = pltpu.PrefetchScalarGridSpec(
        num_scalar_prefetch=5,
        grid=(n_blocks,),
        in_specs=[
            pl.BlockSpec((rows_blk, LANES), lambda i, *s: (blk(i, *s), 0)),
            pl.BlockSpec(memory_space=pl.ANY),
            pl.BlockSpec(memory_space=pl.ANY),
            pl.BlockSpec(memory_space=pl.ANY),
        ],
        out_specs=pl.BlockSpec((rows_blk, LANES), lambda i, *s: (i, 0)),
        scratch_shapes=[
            pltpu.VMEM((2, d, de), F32), pltpu.VMEM((2, d, de), F32), pltpu.VMEM((2, de, d), F32),
            pltpu.VMEM((d, de), BF16), pltpu.VMEM((d, de), BF16), pltpu.VMEM((de, d), BF16),
            pltpu.SemaphoreType.DMA((2,)),
        ],
    )
    return pl.pallas_call(
        functools.partial(_expert_body, layer),
        grid_spec=grid_spec,
        out_shape=jax.ShapeDtypeStruct(xb.shape, xb.dtype),
        compiler_params=_cparams(("arbitrary",)),
        name="moe_experts",
    )(blk_e, nused, ordinal, uniq, n_distinct, xb, w_gate, w_up, w_down)


def _combine_body(alpha, dest_ref, next_ref, x_ref, gts_ref, gt_ref, lng_ref, lnb_ref, yb_ref, o_ref, buf, sem):
    i = pl.program_id(0)
    tm, d = x_ref.shape
    nchunk = d // (2 * LANES)

    def slab(ref, row):
        return ref.at[pl.ds(pl.multiple_of(row * nchunk, nchunk), nchunk)]

    def issue(idx_ref, phase):
        def body(tok, carry):
            for k in range(2):
                pltpu.make_async_copy(slab(yb_ref, idx_ref[k, tok]), slab(buf.at[phase, k], tok),
                                      sem.at[phase]).start()
            return carry
        lax.fori_loop(0, tm, body, 0, unroll=8)

    @pl.when(i == 0)
    def _():
        issue(dest_ref, 0)

    @pl.when(i + 1 < pl.num_programs(0))
    def _():
        issue(next_ref, (i + 1) % 2)

    phase = i % 2
    for k in range(2):
        pltpu.make_async_copy(yb_ref.at[pl.ds(0, tm * nchunk)], buf.at[phase, k], sem.at[phase]).wait()

    gts = gts_ref[...]
    f = jnp.zeros((tm, d), F32)
    for k in range(2):
        rows_k = buf.at[phase, k]
        words = jnp.concatenate([rows_k[pl.ds(c, tm, stride=nchunk), :] for c in range(nchunk)], axis=1)
        y_hi = lax.bitcast_convert_type(words & jnp.int32(-65536), F32)
        y_lo = lax.bitcast_convert_type(lax.shift_left(words, 16), F32)
        f = f + gts[:, k:k + 1] * jnp.concatenate([y_hi, y_lo], axis=1)
    z = alpha * x_ref[...] + (1.0 + gt_ref[...]) * f
    o_ref[...] = _layer_norm(z, lng_ref[...], lnb_ref[...])


def _combine(alpha, dest_tiles, x1, gates_t, mod, ln_g, ln_b, yb):
    rows, d = x1.shape
    n_tiles, _, tm = dest_tiles.shape
    nchunk = d // (2 * LANES)
    vec = pl.BlockSpec((1, d), lambda i: (0, 0))
    if mod.tiles_per_seq is not None:
        mod = _Mod(mod.arr, d, mod.tiles_per_seq * (ROW_TILE // tm))
    return pl.pallas_call(
        functools.partial(_combine_body, alpha),
        grid=(n_tiles,),
        in_specs=[
            pl.BlockSpec((None, 2, tm), lambda i: (i, 0, 0), memory_space=pltpu.SMEM),
            pl.BlockSpec((None, 2, tm), lambda i: (jnp.minimum(i + 1, n_tiles - 1), 0, 0), memory_space=pltpu.SMEM),
            pl.BlockSpec((tm, d), lambda i: (i, 0)),
            pl.BlockSpec((tm, 2), lambda i: (i, 0)),
            mod.spec(5, 1), vec, vec,
            pl.BlockSpec(memory_space=pl.ANY),
        ],
        out_specs=pl.BlockSpec((tm, d), lambda i: (i, 0)),
        out_shape=jax.ShapeDtypeStruct((rows, d), F32),
        scratch_shapes=[pltpu.VMEM((2, 2, tm * nchunk, LANES), yb.dtype), pltpu.SemaphoreType.DMA((2,))],
        compiler_params=_cparams(("arbitrary",)),
        name="moe_combine_postnorm",
    )(dest_tiles, dest_tiles, x1, gates_t, mod.arr, ln_g, ln_b, yb)


def _gmlp_in_body(x_ref, sh_ref, sc_ref, wu_ref, wv_ref, vg_ref, vb_ref, u_ref, v_ref, h_scr, v_scr):
    j = pl.program_id(1)
    half = v_scr.shape[0]

    @pl.when(j == 0)
    def _():
        h_scr[...] = (x_ref[...] * (1.0 + sc_ref[...]) + sh_ref[...]).astype(BF16)

    h = h_scr[...]
    u_ref[...] = jax.nn.gelu(jnp.dot(h, wu_ref[...], preferred_element_type=F32)).astype(u_ref.dtype)
    v_scr[j] = jax.nn.gelu(jnp.dot(h, wv_ref[...], preferred_element_type=F32))

    @pl.when(j == half - 1)
    def _():
        v = jnp.concatenate([v_scr[c] for c in range(half)], axis=1)
        v_ref[...] = _layer_norm(v, vg_ref[...], vb_ref[...]).astype(v_ref.dtype)


def _gmlp_in(x, mod, w_in_bf16, vn_g, vn_b, tm, v_dtype):
    rows, d = x.shape
    tn = GMLP_IN_COLS
    half = d // tn
    vec = pl.BlockSpec((1, d), lambda i, j: (0, 0))
    return pl.pallas_call(
        _gmlp_in_body,
        grid=(rows // tm, half),
        in_specs=[
            pl.BlockSpec((tm, d), lambda i, j: (i, 0)),
            mod.spec(0, 2), mod.spec(1, 2),
            pl.BlockSpec((d, tn), lambda i, j: (0, j)),
            pl.BlockSpec((d, tn), lambda i, j: (0, half + j)),
            vec, vec,
        ],
        out_specs=[
            pl.BlockSpec((tm, tn), lambda i, j: (i, j)),
            pl.BlockSpec((tm, d), lambda i, j: (i, 0)),
        ],
        out_shape=[jax.ShapeDtypeStruct((rows, d), BF16), jax.ShapeDtypeStruct((rows, d), v_dtype)],
        scratch_shapes=[pltpu.VMEM((tm, d), BF16), pltpu.VMEM((half, tm, tn), F32)],
        compiler_params=_cparams(("arbitrary", "arbitrary")),
        name="gmlp_in",
    )(x, mod.arr, mod.arr, w_in_bf16, w_in_bf16, vn_g, vn_b)


def _moe_and_norm(alpha, x1_p, x1_s, h2c_p, h2c_s, lg_p, lg_s, mod_p, mod_s, ln_g, ln_b,
                  w_gate, w_up, w_down, layer):
    t_p, d = x1_p.shape
    t_s = x1_s.shape[0]
    t = t_p + t_s
    nchunk = d // LANES
    ne = w_gate.shape[1]
    n_blocks = -(-(2 * t) // MOE_ROWS) + ne
    n_rows = n_blocks * MOE_ROWS

    eid, gates, dest, meta = _route(jnp.concatenate([lg_p, lg_s], axis=1), n_blocks)
    blk_e = meta[0, :n_blocks]
    nused = meta[1, :1]
    tail = jnp.stack([nused * MOE_ROWS, jnp.full((1,), n_rows, I32)])
    fill = jnp.concatenate([meta[2:4, :ne], tail], axis=1)

    def tiles(a, tm):
        return a.reshape(2, -1, tm).transpose(1, 0, 2)

    xb = _dispatch(fill, tiles(dest, ROW_TILE), h2c_p, h2c_s, n_rows, nchunk // 2)
    yb = _experts(blk_e, nused, xb, w_gate, w_up, w_down, layer, n_blocks)
    gates_t = gates.T
    tm = COMBINE_TILE
    x2_p = _combine(alpha, tiles(dest[:, :t_p], tm), x1_p, gates_t[:t_p], mod_p, ln_g, ln_b, yb)
    x2_s = _combine(alpha, tiles(dest[:, t_p:], min(tm, SAMPLE_TILE)), x1_s, gates_t[t_p:], mod_s, ln_g, ln_b, yb)
    return x2_p, x2_s


def _router_weights(w_rg, b_rg, w_re, b_re):
    d = w_rg.shape[0]
    pad = jnp.zeros((SUBLANES - MOE_GROUPS, d), F32)
    w = jnp.concatenate([w_rg.T, pad, w_re.T], axis=0)
    b = jnp.concatenate([b_rg, jnp.zeros((SUBLANES - MOE_GROUPS,), F32), b_re])[:, None]
    hi = w.astype(BF16)
    lo = (w - hi.astype(F32)).astype(BF16)
    return hi, lo, b


def kernel(x_prompt, x_sample, cache_k, cache_v, page_table, c_prompt, c_sample, w_ada, b_ada, ln_g, ln_b,
           attn_w_qkv, attn_w_o, gmlp_w_in, gmlp_vn_g, gmlp_vn_b, gmlp_w_s, gmlp_b_s, gmlp_w_out,
           moe_w_rg, moe_b_rg, moe_w_re, moe_b_re, moe_w_gate, moe_w_up, moe_w_down):
    batch, seq, d = x_prompt.shape
    nseq, dec_seq, _ = x_sample.shape
    depth = w_ada.shape[0]
    dh = d // N_HEADS
    past_len = page_table.shape[1] * cache_k.shape[2]
    alpha = (2.0 * depth) ** 0.25
    tiles_per_seq = seq // ROW_TILE

    xp = x_prompt.reshape(batch * seq, d)
    xs = x_sample.transpose(1, 0, 2).reshape(dec_seq * nseq, d)

    pad_rows = (-(batch + nseq)) % SUBLANES
    c_all = jnp.concatenate([c_prompt, c_sample, jnp.zeros((pad_rows, d), F32)], axis=0)
    m_all = _adaln(c_all, w_ada, b_ada)

    tabs_p = _rope_tables(jnp.arange(seq, dtype=I32), dh)
    tabs_s = _rope_tables(past_len + jnp.repeat(jnp.arange(dec_seq, dtype=I32), nseq), dh)

    new_kp, new_vp, new_ks, new_vs, new_gv = [], [], [], [], []
    for l in range(depth):
        mod_p = _Mod(m_all[l, :batch].reshape(batch, 1, 6 * d), d, tiles_per_seq)
        mod_in = _Mod(mod_p.arr, d, seq // IN_TILE)
        mod_s = _Mod(m_all[l, batch:batch + nseq], d, None)
        lng0, lnb0 = ln_g[l, 0][None, :], ln_b[l, 0][None, :]
        lng1, lnb1 = ln_g[l, 1][None, :], ln_b[l, 1][None, :]
        wr_hi, wr_lo, br = _router_weights(moe_w_rg[l], moe_b_rg[l], moe_w_re[l], moe_b_re[l])
        post = functools.partial(_post, alpha)

        if l % 2 == 0:
            la = l // 2
            w_qkv = attn_w_qkv[la].astype(BF16)
            w_o = attn_w_o[la].astype(BF16)
            q_p, k_p, v_p = _qkv(xp, mod_in, w_qkv, tabs_p, lambda i: i % (seq // IN_TILE), IN_TILE, QKV_COLS, BF16)
            q_s, k_s, v_s = _qkv(xs, mod_s, w_qkv, tabs_s, lambda i: i, SAMPLE_TILE, QKV_COLS, F32, cols_outer=True)
            o_p = _attn_prompt(q_p, k_p, v_p, batch, seq)

            ngrp = N_HEADS // HEAD_GROUP

            def seq_rows(a):
                a = a.reshape(dec_seq, nseq, ngrp, HEAD_GROUP, dh).transpose(1, 2, 0, 3, 4)
                return a.reshape(nseq, ngrp, dec_seq * HEAD_GROUP, dh)

            o_s = _attn_sample(seq_rows(q_s), seq_rows(k_s), seq_rows(v_s), cache_k, cache_v, la, page_table)
            o_s = o_s.reshape(nseq, ngrp, dec_seq, HEAD_GROUP, dh).transpose(2, 0, 1, 3, 4)
            o_s = o_s.reshape(dec_seq * nseq, d).astype(BF16)
            new_kp.append(k_p.reshape(batch, seq, N_HEADS, dh))
            new_vp.append(v_p.reshape(batch, seq, N_HEADS, dh))
            new_ks.append(k_s.reshape(dec_seq, nseq, N_HEADS, dh).transpose(1, 0, 2, 3))
            new_vs.append(v_s.reshape(dec_seq, nseq, N_HEADS, dh).transpose(1, 0, 2, 3))
            x1_p, h2c_p, lg_p = post(_pro_identity, (o_p,), [pl.BlockSpec((ROW_TILE, d), lambda i: (i, 0))],
                                     xp, w_o, mod_p, lng0, lnb0, wr_hi, wr_lo, br, ROW_TILE)
            x1_s, h2c_s, lg_s = post(_pro_identity, (o_s,), [pl.BlockSpec((SAMPLE_TILE, d), lambda i: (i, 0))],
                                     xs, w_o, mod_s, lng0, lnb0, wr_hi, wr_lo, br, SAMPLE_TILE)
        else:
            lb = l // 2
            w_in = gmlp_w_in[lb].astype(BF16)
            w_out = gmlp_w_out[lb].astype(BF16)
            vg, vb = gmlp_vn_g[lb][None, :], gmlp_vn_b[lb][None, :]
            u_p, vn_p = _gmlp_in(xp, mod_in, w_in, vg, vb, IN_TILE, BF16)
            u_s, vn_s = _gmlp_in(xs, mod_s, w_in, vg, vb, SAMPLE_TILE, F32)
            new_gv.append(vn_s.reshape(dec_seq, nseq, d).transpose(1, 0, 2))
            gw = d // GMLP_GROUPS
            bexp = jnp.repeat(gmlp_b_s[lb].T, gw, axis=1)
            wexp = jnp.repeat(gmlp_w_s[lb][:, :dec_seq, :dec_seq].transpose(1, 2, 0).reshape(dec_seq * dec_seq, -1),
                              gw, axis=1)
            tile = pl.BlockSpec((ROW_TILE, d), lambda i: (i, 0))
            x1_p, h2c_p, lg_p = post(
                _pro_gmlp_prompt, (u_p, vn_p, gmlp_w_s[lb], bexp),
                [tile, tile, pl.BlockSpec(gmlp_w_s[lb].shape, lambda i: (0, 0, 0)),
                 pl.BlockSpec(bexp.shape, lambda i: (0, 0))],
                xp, w_out, mod_p, lng0, lnb0, wr_hi, wr_lo, br, ROW_TILE)
            x1_s, h2c_s, lg_s = post(
                _pro_gmlp_sample, (u_s, vn_s, wexp, bexp[:SUBLANES]),
                [pl.BlockSpec((SAMPLE_TILE, d), lambda i: (i, 0)),
                 pl.BlockSpec(vn_s.shape, lambda i: (0, 0)),
                 pl.BlockSpec(wexp.shape, lambda i: (0, 0)),
                 pl.BlockSpec((SUBLANES, d), lambda i: (0, 0))],
                xs, w_out, mod_s, lng0, lnb0, wr_hi, wr_lo, br, SAMPLE_TILE)

        xp, xs = _moe_and_norm(alpha, x1_p, x1_s, h2c_p, h2c_s, lg_p, lg_s, mod_p, mod_s, lng1, lnb1,
                               moe_w_gate, moe_w_up, moe_w_down, l)

    y_p = xp.reshape(batch, seq, d)
    y_s = xs.reshape(dec_seq, nseq, d).transpose(1, 0, 2)
    return (y_p, y_s, jnp.stack(new_kp), jnp.stack(new_vp), jnp.stack(new_ks), jnp.stack(new_vs),
            jnp.stack(new_gv))
```

```python
import functools

import jax
import jax.numpy as jnp
from jax import lax
from jax.experimental import pallas as pl
from jax.experimental.pallas import tpu as pltpu

F32, BF16, I32 = jnp.float32, jnp.bfloat16, jnp.int32

N_HEADS = 16
MOBA_BLOCK = 256
MOBA_TOPK = 3
ROPE_THETA = 500000.0
GMLP_CHUNK = 128
GMLP_GROUPS = 8
MOE_GROUPS = 4
MOE_EXPERTS_PER_GROUP = 8
LN_EPS = 1e-5

LANES = 128
SUBLANES = 8
VMEM_LIMIT = 56 * 1024 * 1024

ROW_TILE = 512
IN_TILE = 1024
QKV_COLS = 512
GMLP_IN_COLS = 512
SAMPLE_TILE = 128
MOE_ROWS = 128
COMBINE_TILE = 256
PAGES_PER_STEP = 8

NT_DIMS = (((1,), (1,)), ((), ()))


def _cparams(sem):
    return pltpu.CompilerParams(dimension_semantics=sem, vmem_limit_bytes=VMEM_LIMIT)


def _layer_norm(z, g, b):
    mu = jnp.mean(z, axis=-1, keepdims=True)
    zc = z - mu
    var = jnp.mean(zc * zc, axis=-1, keepdims=True)
    return zc * lax.rsqrt(var + LN_EPS) * g + b


def _adaln_body(c_ref, w_ref, b_ref, o_ref):
    a = jax.nn.silu(c_ref[...]).astype(BF16)
    o_ref[...] = jnp.dot(a, w_ref[...].astype(BF16), preferred_element_type=F32) + b_ref[...]


def _adaln(c_all, w_ada, b_ada):
    depth, d, n = w_ada.shape
    rows = c_all.shape[0]
    tn = 1024
    return pl.pallas_call(
        _adaln_body,
        grid=(depth, n // tn),
        in_specs=[
            pl.BlockSpec((rows, d), lambda l, j: (0, 0)),
            pl.BlockSpec((None, d, tn), lambda l, j: (l, 0, j)),
            pl.BlockSpec((None, 1, tn), lambda l, j: (l, 0, j)),
        ],
        out_specs=pl.BlockSpec((None, rows, tn), lambda l, j: (l, 0, j)),
        out_shape=jax.ShapeDtypeStruct((depth, rows, n), F32),
        compiler_params=_cparams(("arbitrary", "arbitrary")),
        name="adaln",
    )(c_all, w_ada, b_ada.reshape(depth, 1, n))


class _Mod:
    def __init__(self, arr, d, tiles_per_seq):
        self.arr = arr
        self.d = d
        self.tiles_per_seq = tiles_per_seq

    def spec(self, k, ngrid):
        d, tps = self.d, self.tiles_per_seq
        if tps is None:
            rows = self.arr.shape[0]
            if ngrid == 1:
                return pl.BlockSpec((rows, d), lambda i: (0, k))
            if ngrid == 2:
                return pl.BlockSpec((rows, d), lambda i, j: (0, k))
            return pl.BlockSpec((rows, d), lambda i, p, j: (0, k))
        if ngrid == 1:
            return pl.BlockSpec((None, 1, d), lambda i: (i // tps, 0, k))
        if ngrid == 2:
            return pl.BlockSpec((None, 1, d), lambda i, j: (i // tps, 0, k))
        return pl.BlockSpec((None, 1, d), lambda i, p, j: (i // tps, 0, k))


def _qkv_body(cols_outer, x_ref, sh_ref, sc_ref, wq_ref, wk_ref, wv_ref, c_ref, s1_ref, s2_ref,
              q_ref, k_ref, v_ref, h_scr):
    def modulate():
        h_scr[...] = (x_ref[...] * (1.0 + sc_ref[...]) + sh_ref[...]).astype(BF16)

    if cols_outer:
        modulate()
    else:
        pl.when(pl.program_id(1) == 0)(modulate)

    tn = q_ref.shape[1]
    reps = tn // LANES
    c = jnp.tile(c_ref[...], (1, reps))
    s1 = jnp.tile(s1_ref[...], (1, reps))
    s2 = jnp.tile(s2_ref[...], (1, reps))
    half = c_ref.shape[1] // 8

    def rope(y):
        return y * c + pltpu.roll(y, tn - half, 1) * s1 + pltpu.roll(y, half, 1) * s2

    h = h_scr[...]
    q_ref[...] = rope(jnp.dot(h, wq_ref[...], preferred_element_type=F32)).astype(q_ref.dtype)
    k_ref[...] = rope(jnp.dot(h, wk_ref[...], preferred_element_type=F32))
    v_ref[...] = jnp.dot(h, wv_ref[...], preferred_element_type=F32)


def _qkv(x, mod, w_qkv_bf16, tabs, tab_index, tm, tn, q_dtype, cols_outer=False):
    rows, d = x.shape
    ncol = d // tn
    c_tab, s1_tab, s2_tab = tabs
    assert not cols_outer or mod.tiles_per_seq is None

    def spec(block, index):
        if cols_outer:
            return pl.BlockSpec(block, lambda j, i: index(i, j))
        return pl.BlockSpec(block, index)

    tab_spec = spec((tm, LANES), lambda i, j: (tab_index(i), 0))
    out = spec((tm, tn), lambda i, j: (i, j))

    def w_spec(part):
        return spec((d, tn), lambda i, j: (0, part * ncol + j))

    return pl.pallas_call(
        functools.partial(_qkv_body, cols_outer),
        grid=(ncol, rows // tm) if cols_outer else (rows // tm, ncol),
        in_specs=[
            spec((tm, d), lambda i, j: (i, 0)),
            mod.spec(0, 2),
            mod.spec(1, 2),
            w_spec(0), w_spec(1), w_spec(2),
            tab_spec, tab_spec, tab_spec,
        ],
        out_specs=[out, out, out],
        out_shape=[
            jax.ShapeDtypeStruct((rows, d), q_dtype),
            jax.ShapeDtypeStruct((rows, d), F32),
            jax.ShapeDtypeStruct((rows, d), F32),
        ],
        scratch_shapes=[pltpu.VMEM((tm, d), BF16)],
        compiler_params=_cparams(("arbitrary", "arbitrary")),
        name="qkv_rope",
    )(x, mod.arr, mod.arr, w_qkv_bf16, w_qkv_bf16, w_qkv_bf16, c_tab, s1_tab, s2_tab)


def _rope_tables(pos, head_dim):
    rot = head_dim // 4
    half = rot // 2
    inv = ROPE_THETA ** (-jnp.arange(half, dtype=F32) * 2.0 / rot)
    ang = pos.astype(F32)[:, None] * inv[None, :]
    cos, sin = jnp.cos(ang), jnp.sin(ang)
    n = pos.shape[0]
    ones = jnp.ones((n, head_dim - rot), F32)
    zeros = jnp.zeros((n, head_dim - rot), F32)
    zh = jnp.zeros((n, half), F32)
    c = jnp.concatenate([cos, cos, ones], axis=1)
    s1 = jnp.concatenate([-sin, zh, zeros], axis=1)
    s2 = jnp.concatenate([zh, sin, zeros], axis=1)
    return c, s1, s2


def _attn_prompt_body(q_ref, k_ref, v_ref, o_ref, kb_scr, vb_scr, sel_scr, qs_scr):
    seq, dh = q_ref.shape
    nblk = seq // MOBA_BLOCK
    scale = dh ** -0.5
    kf = k_ref[...]
    kb_scr[...] = kf.astype(BF16)
    vb_scr[:, :dh] = v_ref[...].astype(BF16)
    vb_scr[:, dh:] = jnp.ones((seq, dh), BF16)

    km = kf.reshape(nblk, MOBA_BLOCK, dh).sum(axis=1) * (1.0 / MOBA_BLOCK)
    g = lax.dot_general(km, q_ref[...].astype(F32), NT_DIMS, precision=lax.Precision.HIGHEST,
                        preferred_element_type=F32)
    blk = lax.broadcasted_iota(I32, (nblk, seq), 0)
    cur = lax.broadcasted_iota(I32, (nblk, seq), 1) // MOBA_BLOCK
    past = blk < cur
    g = jnp.where(past, g, -jnp.inf)
    cnt = jnp.zeros((nblk, seq), F32)
    for k in range(nblk):
        gk = g[k:k + 1, :]
        beats = (gk > g) | ((gk == g) & (k < blk))
        cnt = cnt + beats.astype(F32)
    sel = ((cnt < MOBA_TOPK) & past).astype(F32)
    sel_scr[...] = jnp.concatenate([sel, jnp.zeros((LANES - nblk, seq), F32)], axis=0).T

    row = lax.broadcasted_iota(I32, (MOBA_BLOCK, MOBA_BLOCK), 0)
    col = lax.broadcasted_iota(I32, (MOBA_BLOCK, MOBA_BLOCK), 1)
    causal = row >= col

    qs_scr[...] = (q_ref[...].astype(F32) * scale).astype(BF16)

    for c in range(nblk):
        lo, hi = c * MOBA_BLOCK, (c + 1) * MOBA_BLOCK
        qc = qs_scr[lo:hi, :]
        selc = sel_scr[lo:hi, :]
        slabs = []
        for j in range(c + 1):
            s = lax.dot_general(qc, kb_scr[j * MOBA_BLOCK:(j + 1) * MOBA_BLOCK, :], NT_DIMS,
                                preferred_element_type=F32)
            if j < c:
                s = jnp.where(selc[:, j:j + 1] > 0.5, s, -jnp.inf)
            else:
                s = jnp.where(causal, s, -jnp.inf)
            slabs.append(s)
        mx = slabs[0]
        for s in slabs[1:]:
            mx = jnp.maximum(mx, s)
        m = mx.max(axis=1, keepdims=True)
        acc = jnp.zeros((MOBA_BLOCK, 2 * dh), F32)
        for j, s in enumerate(slabs):
            p = jnp.exp(s - m)
            acc = acc + jnp.dot(p.astype(BF16), vb_scr[j * MOBA_BLOCK:(j + 1) * MOBA_BLOCK, :],
                                preferred_element_type=F32)
        o_ref[lo:hi, :] = (acc[:, :dh] / acc[:, dh:]).astype(o_ref.dtype)


def _attn_prompt(q, k, v, batch, seq):
    rows, d = q.shape
    dh = d // N_HEADS
    spec = pl.BlockSpec((seq, dh), lambda b, h: (b, h))
    return pl.pallas_call(
        _attn_prompt_body,
        grid=(batch, N_HEADS),
        in_specs=[spec, spec, spec],
        out_specs=spec,
        out_shape=jax.ShapeDtypeStruct((rows, d), BF16),
        scratch_shapes=[pltpu.VMEM((seq, dh), BF16), pltpu.VMEM((seq, 2 * dh), BF16), pltpu.VMEM((seq, LANES), F32),
                        pltpu.VMEM((seq, dh), BF16)],
        compiler_params=_cparams(("arbitrary", "arbitrary")),
        name="moba_prompt",
    )(q, k, v)


HEAD_GROUP = SUBLANES


def _attn_sample_body(pt_ref, q_ref, kn_ref, vn_ref, hm_ref, *rest):
    npg = PAGES_PER_STEP
    k_refs = rest[:npg]
    v_refs = rest[npg:2 * npg]
    o_ref = rest[2 * npg]
    s_scr, p_scr, ksum_scr, bmax_scr, acc_scr, linv_scr = rest[2 * npg + 1:]

    t = pl.program_id(1)
    n_pages, ngrp, nq, grp_rows = s_scr.shape
    hg = HEAD_GROUP
    k_steps = n_pages // npg
    dh = q_ref.shape[2]
    page = grp_rows // hg
    pages_per_blk = MOBA_BLOCK // page
    blk_per_step = npg // pages_per_blk
    nblk = n_pages // pages_per_blk
    scale = dh ** -0.5

    @pl.when(t < k_steps)
    def _():
        hm = hm_ref[...]
        sums = []
        pmax = [[] for _ in range(ngrp)]
        for i in range(npg):
            kp = k_refs[i][...]
            sums.append(jnp.sum(kp, axis=0))
            for g in range(ngrp):
                kb = kp[:, g * hg:(g + 1) * hg, :].reshape(grp_rows, dh).astype(BF16)
                s = lax.dot_general(q_ref[g].astype(BF16), kb, NT_DIMS, preferred_element_type=F32) * scale + hm
                s_scr[t * npg + i, g] = s
                pmax[g].append(s.max(axis=1, keepdims=True))
        for b in range(blk_per_step):
            j = t * blk_per_step + b
            tot = sums[b * pages_per_blk]
            for i in range(1, pages_per_blk):
                tot = tot + sums[b * pages_per_blk + i]
            ksum_scr[j] = tot
            for g in range(ngrp):
                bm = pmax[g][b * pages_per_blk]
                for i in range(1, pages_per_blk):
                    bm = jnp.maximum(bm, pmax[g][b * pages_per_blk + i])
                bmax_scr[j, g] = jnp.broadcast_to(bm, (nq, LANES))

    @pl.when(t == k_steps - 1)
    def _():
        gw = nblk * hg
        width = ngrp * gw
        km = jnp.concatenate(
            [ksum_scr[:, g * hg:(g + 1) * hg, :].reshape(gw, dh) for g in range(ngrp)], axis=0) * (1.0 / MOBA_BLOCK)
        qall = jnp.concatenate([q_ref[g] for g in range(ngrp)], axis=0)
        gate = lax.dot_general(qall, km, NT_DIMS, precision=lax.Precision.HIGHEST, preferred_element_type=F32)
        row = lax.broadcasted_iota(I32, (ngrp * nq, width), 0)
        lane = lax.broadcasted_iota(I32, (ngrp * nq, width), 1)
        own = ((row // nq) == (lane // gw)) & ((row % hg) == (lane % hg))
        blk = (lane % gw) // hg
        gate = jnp.where(own, gate, -jnp.inf)
        cnt = jnp.zeros((ngrp * nq, width), F32)
        for dlt in range(1, nblk):
            lower = pltpu.roll(gate, dlt * hg, 1)
            cnt = cnt + ((blk >= dlt) & (lower >= gate)).astype(F32)
            upper = pltpu.roll(gate, width - dlt * hg, 1)
            cnt = cnt + ((blk + dlt < nblk) & (upper > gate)).astype(F32)
        sel = jnp.where(own & (cnt < MOBA_TOPK), 1.0, 0.0)

        r = lax.broadcasted_iota(I32, (nq, nq), 0)
        c = lax.broadcasted_iota(I32, (nq, nq), 1)
        new_ok = ((r % hg) == (c % hg)) & ((c // hg) <= (r // hg))
        for g in range(ngrp):
            qb = q_ref[g].astype(BF16)
            sn = lax.dot_general(qb, kn_ref[g].astype(BF16), NT_DIMS, preferred_element_type=F32) * scale
            sn = jnp.where(new_ok, sn, -jnp.inf)
            m = sn.max(axis=1, keepdims=True)
            biases = []
            for j in range(nblk):
                lo = g * gw + j * hg
                picked = sel[g * nq:(g + 1) * nq, lo:lo + hg].max(axis=1, keepdims=True)
                bias = jnp.where(picked > 0.5, 0.0, -jnp.inf)
                biases.append(bias)
                m = jnp.maximum(m, bmax_scr[j, g][:, 0:1] + bias)
            en = jnp.exp(sn - m)
            l = en.sum(axis=1, keepdims=True)
            for pg in range(n_pages):
                e = jnp.exp(s_scr[pg, g] + (biases[pg // pages_per_blk] - m))
                l = l + e.sum(axis=1, keepdims=True)
                p_scr[pg, g] = e.astype(BF16)
            linv_scr[g] = jnp.broadcast_to(1.0 / l, (nq, dh))
            acc_scr[g] = jnp.dot(en.astype(BF16), vn_ref[g].astype(BF16), preferred_element_type=F32)

    @pl.when(t >= k_steps)
    def _():
        accs = [acc_scr[g] for g in range(ngrp)]
        for i in range(npg):
            vp = v_refs[i][...]
            for g in range(ngrp):
                vb = vp[:, g * hg:(g + 1) * hg, :].reshape(grp_rows, dh).astype(BF16)
                accs[g] = accs[g] + jnp.dot(p_scr[(t - k_steps) * npg + i, g], vb, preferred_element_type=F32)
        for g in range(ngrp):
            acc_scr[g] = accs[g]

    @pl.when(t == 2 * k_steps - 1)
    def _():
        o_ref[...] = acc_scr[...] * linv_scr[...]


def _attn_sample(q_rows, k_new, v_new, cache_k, cache_v, layer, page_table):
    nseq, ngrp, nq, dh = q_rows.shape
    _, n_phys, page, heads, _ = cache_k.shape
    n_pages = page_table.shape[1]
    npg = PAGES_PER_STEP
    k_steps = n_pages // npg
    grp_rows = page * HEAD_GROUP
    rowh = jnp.arange(nq, dtype=I32)[:, None] % HEAD_GROUP
    colh = jnp.arange(grp_rows, dtype=I32)[None, :] % HEAD_GROUP
    head_mask = jnp.where(rowh == colh, 0.0, -jnp.inf).astype(F32)

    def kmap(i):
        return lambda b, t, pt: (layer, pt[b * n_pages + jnp.minimum(t, k_steps - 1) * npg + i], 0, 0, 0)

    def vmap_(i):
        def index(b, t, pt):
            in_v = t >= k_steps
            seq_i = jnp.where(in_v, b, jnp.maximum(b - 1, 0))
            step = jnp.where(in_v, t - k_steps, k_steps - 1)
            return (layer, pt[seq_i * n_pages + step * npg + i], 0, 0, 0)
        return index

    seq_spec = pl.BlockSpec((None, ngrp, nq, dh), lambda b, t, pt: (b, 0, 0, 0))
    page_block = (None, None, page, heads, dh)
    nblk = n_pages * page // MOBA_BLOCK
    grid_spec = pltpu.PrefetchScalarGridSpec(
        num_scalar_prefetch=1,
        grid=(nseq, 2 * k_steps),
        in_specs=[seq_spec, seq_spec, seq_spec,
                  pl.BlockSpec((nq, grp_rows), lambda b, t, pt: (0, 0))]
                 + [pl.BlockSpec(page_block, kmap(i)) for i in range(npg)]
                 + [pl.BlockSpec(page_block, vmap_(i)) for i in range(npg)],
        out_specs=seq_spec,
        scratch_shapes=[
            pltpu.VMEM((n_pages, ngrp, nq, grp_rows), F32),
            pltpu.VMEM((n_pages, ngrp, nq, grp_rows), BF16),
            pltpu.VMEM((nblk, heads, dh), F32),
            pltpu.VMEM((nblk, ngrp, nq, LANES), F32),
            pltpu.VMEM((ngrp, nq, dh), F32),
            pltpu.VMEM((ngrp, nq, dh), F32),
        ],
    )
    return pl.pallas_call(
        _attn_sample_body,
        grid_spec=grid_spec,
        out_shape=jax.ShapeDtypeStruct((nseq, ngrp, nq, dh), F32),
        compiler_params=_cparams(("arbitrary", "arbitrary")),
        name="moba_sample",
    )(page_table.reshape(-1), q_rows, k_new, v_new, head_mask,
      *([cache_k] * npg), *([cache_v] * npg))


def _post_body(alpha, prologue, n_pro, *refs):
    pro_refs = refs[:n_pro]
    (x_ref, w_ref, gt_ref, lng_ref, lnb_ref, sh_ref, sc_ref, wrh_ref, wrl_ref, br_ref,
     x1_ref, h2c_ref, lg_ref) = refs[n_pro:]
    a = prologue(*pro_refs)
    y = jnp.dot(a, w_ref[...], preferred_element_type=F32)
    z = alpha * x_ref[...] + (1.0 + gt_ref[...]) * y
    x1 = _layer_norm(z, lng_ref[...], lnb_ref[...])
    x1_ref[...] = x1
    h2 = x1 * (1.0 + sc_ref[...]) + sh_ref[...]
    tm, d = h2.shape
    hi = h2.astype(BF16)
    hi32 = hi.astype(F32)
    bits = lax.bitcast_convert_type(hi32, I32)
    packed = bits[:, :d // 2] | lax.shift_right_logical(bits[:, d // 2:], 16)
    nchunk = d // (2 * LANES)
    for c in range(nchunk):
        h2c_ref[pl.ds(c, tm, stride=nchunk), :] = packed[:, c * LANES:(c + 1) * LANES]
    lo = (h2 - hi32).astype(BF16)
    wh, wl = wrh_ref[...], wrl_ref[...]
    lg = (lax.dot_general(wh, hi, NT_DIMS, preferred_element_type=F32)
          + lax.dot_general(wl, hi, NT_DIMS, preferred_element_type=F32)
          + lax.dot_general(wh, lo, NT_DIMS, preferred_element_type=F32))
    lg_ref[...] = lg + br_ref[...]


def _pro_identity(a_ref):
    return a_ref[...]


def _pro_gmlp_prompt(u_ref, v_ref, ws_ref, bexp_ref):
    tm, d = u_ref.shape
    gw = d // GMLP_GROUPS
    row = lax.broadcasted_iota(I32, (GMLP_CHUNK, GMLP_CHUNK), 0)
    col = lax.broadcasted_iota(I32, (GMLP_CHUNK, GMLP_CHUNK), 1)
    tri = row >= col
    wms = [jnp.where(tri, ws_ref[g], 0.0).astype(BF16) for g in range(GMLP_GROUPS)]
    outs = []
    for ch in range(tm // GMLP_CHUNK):
        lo, hi = ch * GMLP_CHUNK, (ch + 1) * GMLP_CHUNK
        cols = [jnp.dot(wms[g], v_ref[lo:hi, g * gw:(g + 1) * gw], preferred_element_type=F32)
                for g in range(GMLP_GROUPS)]
        mixed = jnp.concatenate(cols, axis=1) + bexp_ref[...]
        outs.append((u_ref[lo:hi, :].astype(F32) * mixed).astype(BF16))
    return jnp.concatenate(outs, axis=0)


def _pro_gmlp_sample(u_ref, v_ref, wexp_ref, bexp_ref):
    s = pl.program_id(0)
    tm = u_ref.shape[0]
    n_pos = v_ref.shape[0] // tm
    mixed = bexp_ref[pl.ds(s, 1), :]
    for j in range(n_pos):
        w = wexp_ref[pl.ds(s * n_pos + j, 1), :] * (j <= s).astype(F32)
        mixed = mixed + w * v_ref[j * tm:(j + 1) * tm, :]
    return (u_ref[...].astype(F32) * mixed).astype(BF16)


def _post(alpha, prologue, pro_args, pro_specs, x, w_bf16, mod, ln_g, ln_b, wr_hi, wr_lo, br, tm):
    rows, d = x.shape
    ne = wr_hi.shape[0]
    nchunk = d // (2 * LANES)
    vec = pl.BlockSpec((1, d), lambda i: (0, 0))
    return pl.pallas_call(
        functools.partial(_post_body, alpha, prologue, len(pro_args)),
        grid=(rows // tm,),
        in_specs=list(pro_specs) + [
            pl.BlockSpec((tm, d), lambda i: (i, 0)),
            pl.BlockSpec(w_bf16.shape, lambda i: (0, 0)),
            mod.spec(2, 1), vec, vec, mod.spec(3, 1), mod.spec(4, 1),
            pl.BlockSpec((ne, d), lambda i: (0, 0)),
            pl.BlockSpec((ne, d), lambda i: (0, 0)),
            pl.BlockSpec((ne, 1), lambda i: (0, 0)),
        ],
        out_specs=[
            pl.BlockSpec((tm, d), lambda i: (i, 0)),
            pl.BlockSpec((tm * nchunk, LANES), lambda i: (i, 0)),
            pl.BlockSpec((ne, tm), lambda i: (0, i)),
        ],
        out_shape=[
            jax.ShapeDtypeStruct((rows, d), F32),
            jax.ShapeDtypeStruct((rows * nchunk, LANES), I32),
            jax.ShapeDtypeStruct((ne, rows), F32),
        ],
        compiler_params=_cparams(("arbitrary",)),
        name="proj_postnorm_router",
    )(*pro_args, x, w_bf16, mod.arr, ln_g, ln_b, mod.arr, mod.arr, wr_hi, wr_lo, br)


def _route_body(lg_ref, eid_ref, gate_ref, dest_ref, meta_ref):
    ne = MOE_GROUPS * MOE_EXPERTS_PER_GROUP
    epg = MOE_EXPERTS_PER_GROUP
    lgt = lg_ref[...]
    t = lgt.shape[1]
    row8 = lax.broadcasted_iota(I32, (SUBLANES, t), 0)
    lgp = jnp.where(row8 < MOE_GROUPS, lgt[0:SUBLANES], -jnp.inf)
    mg = lgp.max(axis=0, keepdims=True)
    grp = jnp.where(lgp == mg, row8, SUBLANES).min(axis=0, keepdims=True)
    pg = 1.0 / jnp.exp(lgp - mg).sum(axis=0, keepdims=True)

    le = jnp.zeros((epg, t), F32)
    for g in range(MOE_GROUPS):
        le = jnp.where(grp == g, lgt[SUBLANES + g * epg:SUBLANES + (g + 1) * epg], le)
    m1 = le.max(axis=0, keepdims=True)
    i1 = jnp.where(le == m1, row8, epg).min(axis=0, keepdims=True)
    le2 = jnp.where(row8 == i1, -jnp.inf, le)
    m2 = le2.max(axis=0, keepdims=True)
    i2 = jnp.where(le2 == m2, row8, epg).min(axis=0, keepdims=True)
    e = jnp.exp(m2 - m1)
    g1 = pg / (1.0 + e)
    g2 = pg * e / (1.0 + e)
    e1 = grp * epg + i1
    e2 = grp * epg + i2
    eid_ref[0:1, :] = e1
    eid_ref[1:2, :] = e2
    gate_ref[0:1, :] = g1
    gate_ref[1:2, :] = g2

    rows = lax.broadcasted_iota(I32, (ne, t), 0)
    oh1 = (rows == e1).astype(F32)
    oh2 = (rows == e2).astype(F32)
    oh = (oh1 + oh2).astype(BF16)
    cw = 512
    ur = lax.broadcasted_iota(I32, (cw, cw), 0)
    uc = lax.broadcasted_iota(I32, (cw, cw), 1)
    upper = (ur < uc).astype(BF16)
    carry = jnp.zeros((ne, 1), F32)
    pref = []
    for ci in range(t // cw):
        ohc = oh[:, ci * cw:(ci + 1) * cw]
        pref.append(jnp.dot(ohc, upper, preferred_element_type=F32) + carry)
        carry = carry + ohc.astype(F32).sum(axis=1, keepdims=True)
    cnt_before = jnp.concatenate(pref, axis=1)

    counts = carry
    shift = MOE_ROWS.bit_length() - 1
    padded = (((counts.astype(I32) + (MOE_ROWS - 1)) >> shift) << shift).astype(F32)
    lr = lax.broadcasted_iota(I32, (ne, ne), 0)
    lc = lax.broadcasted_iota(I32, (ne, ne), 1)
    lower = (lc < lr).astype(F32)
    pstart = jnp.dot(lower, jnp.broadcast_to(padded, (ne, LANES)), precision=lax.Precision.HIGHEST,
                     preferred_element_type=F32)[:, 0:1]
    pend = pstart + padded
    base = pstart + cnt_before
    d1 = (oh1 * base).sum(axis=0, keepdims=True)
    d2 = (oh2 * base).sum(axis=0, keepdims=True)
    dest_ref[0:1, :] = d1.astype(I32)
    dest_ref[1:2, :] = d2.astype(I32)

    mw = meta_ref.shape[1]
    blk_start = (lax.broadcasted_iota(I32, (ne, mw), 1) * MOE_ROWS).astype(F32)
    blk_e = jnp.minimum((pend <= blk_start).astype(F32).sum(axis=0, keepdims=True), ne - 1.0)
    last = lax.broadcasted_iota(I32, (ne, mw), 0) == ne - 1
    nused = jnp.where(last, jnp.broadcast_to(pend, (ne, mw)), 0.0).sum(axis=0, keepdims=True) * (1.0 / MOE_ROWS)
    lane_e = lax.broadcasted_iota(I32, (ne, mw), 1)
    diag = lane_e == lax.broadcasted_iota(I32, (ne, mw), 0)
    fill_lo = jnp.where(diag, jnp.broadcast_to(pstart + counts, (ne, mw)), 0.0).sum(axis=0, keepdims=True)
    fill_hi = jnp.where(diag, jnp.broadcast_to(pend, (ne, mw)), 0.0).sum(axis=0, keepdims=True)
    mrow = lax.broadcasted_iota(I32, (SUBLANES, mw), 0)
    meta = jnp.where(mrow == 0, blk_e, jnp.where(mrow == 1, nused, jnp.where(mrow == 2, fill_lo, fill_hi)))
    meta_ref[...] = meta.astype(I32)


def _route(logits_t, n_blocks_max):
    ne_pad, t = logits_t.shape
    mw = 256
    assert n_blocks_max <= mw
    return pl.pallas_call(
        _route_body,
        out_shape=[
            jax.ShapeDtypeStruct((2, t), I32),
            jax.ShapeDtypeStruct((2, t), F32),
            jax.ShapeDtypeStruct((2, t), I32),
            jax.ShapeDtypeStruct((SUBLANES, mw), I32),
        ],
        compiler_params=pltpu.CompilerParams(vmem_limit_bytes=VMEM_LIMIT),
        name="moe_route",
    )(logits_t)


def _dispatch_body(nchunk, n_prompt_tiles, fill_ref, dest_ref, hp_ref, hs_ref, xb_ref, zero_scr, sem):
    i = pl.program_id(0)
    tm = dest_ref.shape[1]
    n_ranges = fill_ref.shape[1]

    def slab(ref, row):
        return ref.at[pl.ds(pl.multiple_of(row * nchunk, nchunk), nchunk)]

    def issue(src_ref):
        def body(tok, carry):
            for k in range(2):
                pltpu.make_async_copy(slab(src_ref, tok), slab(xb_ref, dest_ref[k, tok]), sem).start(priority=k)
            return carry
        lax.fori_loop(0, tm, body, 0, unroll=8)

    @pl.when(i < n_prompt_tiles)
    def _():
        issue(hp_ref)

    @pl.when(i >= n_prompt_tiles)
    def _():
        issue(hs_ref)

    for _ in range(2):
        pltpu.make_async_copy(hp_ref, xb_ref.at[pl.ds(0, tm * nchunk)], sem).wait()

    @pl.when(i == 0)
    def _():
        zero_scr[...] = jnp.zeros_like(zero_scr)
        run_sizes = [MOE_ROWS >> (b + 1) for b in range(MOE_ROWS.bit_length() - 1)]

        def run_copy(row, n_rows):
            start = pl.multiple_of(row * nchunk, nchunk)
            return pltpu.make_async_copy(zero_scr.at[pl.ds(0, n_rows * nchunk)],
                                         xb_ref.at[pl.ds(start, n_rows * nchunk)], sem)

        def expert_runs(action):
            for e in range(n_ranges - 1):
                row = fill_ref[0, e]
                n = fill_ref[1, e] - row
                for size in run_sizes:
                    take = (n & size) != 0
                    pl.when(take)(functools.partial(action, row, size))
                    row = row + jnp.where(take, size, 0)

        tail_lo = fill_ref[0, n_ranges - 1]
        tail_blocks = (fill_ref[1, n_ranges - 1] - tail_lo) >> (MOE_ROWS.bit_length() - 1)

        def tail_runs(action):
            def body(b, carry):
                action(tail_lo + b * MOE_ROWS, MOE_ROWS)
                return carry
            lax.fori_loop(0, tail_blocks, body, 0)

        for runs in (expert_runs, tail_runs):
            runs(lambda row, size: run_copy(row, size).start())
        for runs in (expert_runs, tail_runs):
            runs(lambda row, size: run_copy(row, size).wait())


def _dispatch(fill, dest_tiles, h2c_p, h2c_s, n_rows, nchunk):
    n_tiles, _, tm = dest_tiles.shape
    n_prompt_tiles = h2c_p.shape[0] // (nchunk * tm)
    grid_spec = pltpu.PrefetchScalarGridSpec(
        num_scalar_prefetch=1,
        grid=(n_tiles,),
        in_specs=[
            pl.BlockSpec((None, 2, tm), lambda i, f: (i, 0, 0), memory_space=pltpu.SMEM),
            pl.BlockSpec((tm * nchunk, LANES), lambda i, f: (jnp.minimum(i, n_prompt_tiles - 1), 0)),
            pl.BlockSpec((tm * nchunk, LANES), lambda i, f: (jnp.maximum(i - n_prompt_tiles, 0), 0)),
        ],
        out_specs=pl.BlockSpec(memory_space=pl.ANY),
        scratch_shapes=[pltpu.VMEM((MOE_ROWS * nchunk, LANES), h2c_p.dtype), pltpu.SemaphoreType.DMA(())],
    )
    return pl.pallas_call(
        functools.partial(_dispatch_body, nchunk, n_prompt_tiles),
        grid_spec=grid_spec,
        out_shape=jax.ShapeDtypeStruct((n_rows * nchunk, LANES), h2c_p.dtype),
        compiler_params=_cparams(("arbitrary",)),
        name="moe_dispatch",
    )(fill, dest_tiles, h2c_p, h2c_s)


def _expert_body(layer, blk_ref, nused_ref, ord_ref, uniq_ref, ndist_ref, x_ref, wg_hbm, wu_hbm, wd_hbm, y_ref,
                 wg_buf, wu_buf, wd_buf, wg_scr, wu_scr, wd_scr, sem):
    i = pl.program_id(0)
    npack = x_ref.shape[0] // MOE_ROWS
    pairs = ((wg_hbm, wg_buf), (wu_hbm, wu_buf), (wd_hbm, wd_buf))

    def copies(n, slot):
        e = uniq_ref[n]
        return [pltpu.make_async_copy(src.at[layer, e], dst.at[slot], sem.at[slot]) for src, dst in pairs]

    @pl.when(i < nused_ref[0])
    def _():
        prev = blk_ref[jnp.maximum(i - 1, 0)]

        @pl.when(i == 0)
        def _():
            for cp in copies(0, 0):
                cp.start()

        @pl.when((i == 0) | (blk_ref[i] != prev))
        def _():
            n = ord_ref[i]
            slot = n % 2
            for cp in copies(n, slot):
                cp.wait()

            @pl.when(n + 1 < ndist_ref[0])
            def _():
                for cp in copies(n + 1, 1 - slot):
                    cp.start()

            wg_scr[...] = wg_buf[slot].astype(BF16)
            wu_scr[...] = wu_buf[slot].astype(BF16)
            wd_scr[...] = wd_buf[slot].astype(BF16)

        words = jnp.concatenate([x_ref[pl.ds(c, MOE_ROWS, stride=npack), :] for c in range(npack)], axis=1)
        x_hi = lax.bitcast_convert_type(words & jnp.int32(-65536), F32)
        x_lo = lax.bitcast_convert_type(lax.shift_left(words, 16), F32)
        x = jnp.concatenate([x_hi, x_lo], axis=1).astype(BF16)
        g = jnp.dot(x, wg_scr[...], preferred_element_type=F32)
        u = jnp.dot(x, wu_scr[...], preferred_element_type=F32)
        a = (jax.nn.silu(g) * u).astype(BF16)
        y = jnp.dot(a, wd_scr[...], preferred_element_type=F32)
        ybits = lax.bitcast_convert_type(y.astype(BF16).astype(F32), I32)
        half = y.shape[1] // 2
        packed = ybits[:, :half] | lax.shift_right_logical(ybits[:, half:], 16)
        for c in range(npack):
            y_ref[pl.ds(c, MOE_ROWS, stride=npack), :] = packed[:, c * LANES:(c + 1) * LANES]

    @pl.when(i >= nused_ref[0])
    def _():
        y_ref[...] = jnp.zeros_like(y_ref)


def _experts(blk_e, nused, xb, w_gate, w_up, w_down, layer, n_blocks):
    _, ne, d, de = w_gate.shape
    rows_blk = MOE_ROWS * (d // (2 * LANES))

    ids = jnp.arange(n_blocks, dtype=I32)
    change = (ids < nused[0]) & ((ids == 0) | (blk_e != jnp.roll(blk_e, 1)))
    ordinal = jnp.cumsum(change.astype(I32)) - 1
    n_distinct = change.astype(I32).sum()[None]
    uniq = jnp.zeros((ne + 1,), I32).at[jnp.where(change, ordinal, ne)].set(blk_e)

    def blk(i, be, nu, *_):
        return jnp.minimum(i, nu[0] - 1)

    grid_spec = pltpu.PrefetchScalarGridSpec(
        num_scalar_prefetch=5,
        grid=(n_blocks,),
        in_specs=[
            pl.BlockSpec((rows_blk, LANES), lambda i, *s: (blk(i, *s), 0)),
            pl.BlockSpec(memory_space=pl.ANY),
            pl.BlockSpec(memory_space=pl.ANY),
            pl.BlockSpec(memory_space=pl.ANY),
        ],
        out_specs=pl.BlockSpec((rows_blk, LANES), lambda i, *s: (i, 0)),
        scratch_shapes=[
            pltpu.VMEM((2, d, de), F32), pltpu.VMEM((2, d, de), F32), pltpu.VMEM((2, de, d), F32),
            pltpu.VMEM((d, de), BF16), pltpu.VMEM((d, de), BF16), pltpu.VMEM((de, d), BF16),
            pltpu.SemaphoreType.DMA((2,)),
        ],
    )
    return pl.pallas_call(
        functools.partial(_expert_body, layer),
        grid_spec=grid_spec,
        out_shape=jax.ShapeDtypeStruct(xb.shape, xb.dtype),
        compiler_params=_cparams(("arbitrary",)),
        name="moe_experts",
    )(blk_e, nused, ordinal, uniq, n_distinct, xb, w_gate, w_up, w_down)


def _combine_body(alpha, dest_ref, next_ref, x_ref, gts_ref, gt_ref, lng_ref, lnb_ref, yb_ref, o_ref, buf, sem):
    i = pl.program_id(0)
    tm, d = x_ref.shape
    nchunk = d // (2 * LANES)

    def slab(ref, row):
        return ref.at[pl.ds(pl.multiple_of(row * nchunk, nchunk), nchunk)]

    def issue(idx_ref, phase):
        def body(tok, carry):
            for k in range(2):
                pltpu.make_async_copy(slab(yb_ref, idx_ref[k, tok]), slab(buf.at[phase, k], tok),
                                      sem.at[phase]).start(priority=k)
            return carry
        lax.fori_loop(0, tm, body, 0, unroll=8)

    @pl.when(i == 0)
    def _():
        issue(dest_ref, 0)

    @pl.when(i + 1 < pl.num_programs(0))
    def _():
        issue(next_ref, (i + 1) % 2)

    phase = i % 2
    for k in range(2):
        pltpu.make_async_copy(yb_ref.at[pl.ds(0, tm * nchunk)], buf.at[phase, k], sem.at[phase]).wait()

    gts = gts_ref[...]
    f = jnp.zeros((tm, d), F32)
    for k in range(2):
        rows_k = buf.at[phase, k]
        words = jnp.concatenate([rows_k[pl.ds(c, tm, stride=nchunk), :] for c in range(nchunk)], axis=1)
        y_hi = lax.bitcast_convert_type(words & jnp.int32(-65536), F32)
        y_lo = lax.bitcast_convert_type(lax.shift_left(words, 16), F32)
        f = f + gts[:, k:k + 1] * jnp.concatenate([y_hi, y_lo], axis=1)
    z = alpha * x_ref[...] + (1.0 + gt_ref[...]) * f
    o_ref[...] = _layer_norm(z, lng_ref[...], lnb_ref[...])


def _combine(alpha, dest_tiles, x1, gates_t, mod, ln_g, ln_b, yb):
    rows, d = x1.shape
    n_tiles, _, tm = dest_tiles.shape
    nchunk = d // (2 * LANES)
    vec = pl.BlockSpec((1, d), lambda i: (0, 0))
    if mod.tiles_per_seq is not None:
        mod = _Mod(mod.arr, d, mod.tiles_per_seq * (ROW_TILE // tm))
    return pl.pallas_call(
        functools.partial(_combine_body, alpha),
        grid=(n_tiles,),
        in_specs=[
            pl.BlockSpec((None, 2, tm), lambda i: (i, 0, 0), memory_space=pltpu.SMEM),
            pl.BlockSpec((None, 2, tm), lambda i: (jnp.minimum(i + 1, n_tiles - 1), 0, 0), memory_space=pltpu.SMEM),
            pl.BlockSpec((tm, d), lambda i: (i, 0)),
            pl.BlockSpec((tm, 2), lambda i: (i, 0)),
            mod.spec(5, 1), vec, vec,
            pl.BlockSpec(memory_space=pl.ANY),
        ],
        out_specs=pl.BlockSpec((tm, d), lambda i: (i, 0)),
        out_shape=jax.ShapeDtypeStruct((rows, d), F32),
        scratch_shapes=[pltpu.VMEM((2, 2, tm * nchunk, LANES), yb.dtype), pltpu.SemaphoreType.DMA((2,))],
        compiler_params=_cparams(("arbitrary",)),
        name="moe_combine_postnorm",
    )(dest_tiles, dest_tiles, x1, gates_t, mod.arr, ln_g, ln_b, yb)


def _gmlp_in_body(x_ref, sh_ref, sc_ref, wu_ref, wv_ref, vg_ref, vb_ref, u_ref, v_ref, h_scr, v_scr):
    j = pl.program_id(1)
    half = v_scr.shape[0]

    @pl.when(j == 0)
    def _():
        h_scr[...] = (x_ref[...] * (1.0 + sc_ref[...]) + sh_ref[...]).astype(BF16)

    h = h_scr[...]
    u_ref[...] = jax.nn.gelu(jnp.dot(h, wu_ref[...], preferred_element_type=F32)).astype(u_ref.dtype)
    v_scr[j] = jax.nn.gelu(jnp.dot(h, wv_ref[...], preferred_element_type=F32))

    @pl.when(j == half - 1)
    def _():
        v = jnp.concatenate([v_scr[c] for c in range(half)], axis=1)
        v_ref[...] = _layer_norm(v, vg_ref[...], vb_ref[...]).astype(v_ref.dtype)


def _gmlp_in(x, mod, w_in_bf16, vn_g, vn_b, tm, v_dtype):
    rows, d = x.shape
    tn = GMLP_IN_COLS
    half = d // tn
    vec = pl.BlockSpec((1, d), lambda i, j: (0, 0))
    return pl.pallas_call(
        _gmlp_in_body,
        grid=(rows // tm, half),
        in_specs=[
            pl.BlockSpec((tm, d), lambda i, j: (i, 0)),
            mod.spec(0, 2), mod.spec(1, 2),
            pl.BlockSpec((d, tn), lambda i, j: (0, j)),
            pl.BlockSpec((d, tn), lambda i, j: (0, half + j)),
            vec, vec,
        ],
        out_specs=[
            pl.BlockSpec((tm, tn), lambda i, j: (i, j)),
            pl.BlockSpec((tm, d), lambda i, j: (i, 0)),
        ],
        out_shape=[jax.ShapeDtypeStruct((rows, d), BF16), jax.ShapeDtypeStruct((rows, d), v_dtype)],
        scratch_shapes=[pltpu.VMEM((tm, d), BF16), pltpu.VMEM((half, tm, tn), F32)],
        compiler_params=_cparams(("arbitrary", "arbitrary")),
        name="gmlp_in",
    )(x, mod.arr, mod.arr, w_in_bf16, w_in_bf16, vn_g, vn_b)


def _moe_and_norm(alpha, x1_p, x1_s, h2c_p, h2c_s, lg_p, lg_s, mod_p, mod_s, ln_g, ln_b,
                  w_gate, w_up, w_down, layer):
    t_p, d = x1_p.shape
    t_s = x1_s.shape[0]
    t = t_p + t_s
    nchunk = d // LANES
    ne = w_gate.shape[1]
    n_blocks = -(-(2 * t) // MOE_ROWS) + ne
    n_rows = n_blocks * MOE_ROWS

    eid, gates, dest, meta = _route(jnp.concatenate([lg_p, lg_s], axis=1), n_blocks)
    blk_e = meta[0, :n_blocks]
    nused = meta[1, :1]
    tail = jnp.stack([nused * MOE_ROWS, jnp.full((1,), n_rows, I32)])
    fill = jnp.concatenate([meta[2:4, :ne], tail], axis=1)

    def tiles(a, tm):
        return a.reshape(2, -1, tm).transpose(1, 0, 2)

    xb = _dispatch(fill, tiles(dest, ROW_TILE), h2c_p, h2c_s, n_rows, nchunk // 2)
    yb = _experts(blk_e, nused, xb, w_gate, w_up, w_down, layer, n_blocks)
    gates_t = gates.T
    tm = COMBINE_TILE
    x2_p = _combine(alpha, tiles(dest[:, :t_p], tm), x1_p, gates_t[:t_p], mod_p, ln_g, ln_b, yb)
    x2_s = _combine(alpha, tiles(dest[:, t_p:], min(tm, SAMPLE_TILE)), x1_s, gates_t[t_p:], mod_s, ln_g, ln_b, yb)
    return x2_p, x2_s


def _router_weights(w_rg, b_rg, w_re, b_re):
    d = w_rg.shape[0]
    pad = jnp.zeros((SUBLANES - MOE_GROUPS, d), F32)
    w = jnp.concatenate([w_rg.T, pad, w_re.T], axis=0)
    b = jnp.concatenate([b_rg, jnp.zeros((SUBLANES - MOE_GROUPS,), F32), b_re])[:, None]
    hi = w.astype(BF16)
    lo = (w - hi.astype(F32)).astype(BF16)
    return hi, lo, b


def kernel(x_prompt, x_sample, cache_k, cache_v, page_table, c_prompt, c_sample, w_ada, b_ada, ln_g, ln_b,
           attn_w_qkv, attn_w_o, gmlp_w_in, gmlp_vn_g, gmlp_vn_b, gmlp_w_s, gmlp_b_s, gmlp_w_out,
           moe_w_rg, moe_b_rg, moe_w_re, moe_b_re, moe_w_gate, moe_w_up, moe_w_down):
    batch, seq, d = x_prompt.shape
    nseq, dec_seq, _ = x_sample.shape
    depth = w_ada.shape[0]
    dh = d // N_HEADS
    past_len = page_table.shape[1] * cache_k.shape[2]
    alpha = (2.0 * depth) ** 0.25
    tiles_per_seq = seq // ROW_TILE

    xp = x_prompt.reshape(batch * seq, d)
    xs = x_sample.transpose(1, 0, 2).reshape(dec_seq * nseq, d)

    pad_rows = (-(batch + nseq)) % SUBLANES
    c_all = jnp.concatenate([c_prompt, c_sample, jnp.zeros((pad_rows, d), F32)], axis=0)
    m_all = _adaln(c_all, w_ada, b_ada)

    tabs_p = _rope_tables(jnp.arange(seq, dtype=I32), dh)
    tabs_s = _rope_tables(past_len + jnp.repeat(jnp.arange(dec_seq, dtype=I32), nseq), dh)

    new_kp, new_vp, new_ks, new_vs, new_gv = [], [], [], [], []
    for l in range(depth):
        mod_p = _Mod(m_all[l, :batch].reshape(batch, 1, 6 * d), d, tiles_per_seq)
        mod_in = _Mod(mod_p.arr, d, seq // IN_TILE)
        mod_s = _Mod(m_all[l, batch:batch + nseq], d, None)
        lng0, lnb0 = ln_g[l, 0][None, :], ln_b[l, 0][None, :]
        lng1, lnb1 = ln_g[l, 1][None, :], ln_b[l, 1][None, :]
        wr_hi, wr_lo, br = _router_weights(moe_w_rg[l], moe_b_rg[l], moe_w_re[l], moe_b_re[l])
        post = functools.partial(_post, alpha)

        if l % 2 == 0:
            la = l // 2
            w_qkv = attn_w_qkv[la].astype(BF16)
            w_o = attn_w_o[la].astype(BF16)
            q_p, k_p, v_p = _qkv(xp, mod_in, w_qkv, tabs_p, lambda i: i % (seq // IN_TILE), IN_TILE, QKV_COLS, BF16)
            q_s, k_s, v_s = _qkv(xs, mod_s, w_qkv, tabs_s, lambda i: i, SAMPLE_TILE, QKV_COLS, F32, cols_outer=True)
            o_p = _attn_prompt(q_p, k_p, v_p, batch, seq)

            ngrp = N_HEADS // HEAD_GROUP

            def seq_rows(a):
                a = a.reshape(dec_seq, nseq, ngrp, HEAD_GROUP, dh).transpose(1, 2, 0, 3, 4)
                return a.reshape(nseq, ngrp, dec_seq * HEAD_GROUP, dh)

            o_s = _attn_sample(seq_rows(q_s), seq_rows(k_s), seq_rows(v_s), cache_k, cache_v, la, page_table)
            o_s = o_s.reshape(nseq, ngrp, dec_seq, HEAD_GROUP, dh).transpose(2, 0, 1, 3, 4)
            o_s = o_s.reshape(dec_seq * nseq, d).astype(BF16)
            new_kp.append(k_p.reshape(batch, seq, N_HEADS, dh))
            new_vp.append(v_p.reshape(batch, seq, N_HEADS, dh))
            new_ks.append(k_s.reshape(dec_seq, nseq, N_HEADS, dh).transpose(1, 0, 2, 3))
            new_vs.append(v_s.reshape(dec_seq, nseq, N_HEADS, dh).transpose(1, 0, 2, 3))
            x1_p, h2c_p, lg_p = post(_pro_identity, (o_p,), [pl.BlockSpec((ROW_TILE, d), lambda i: (i, 0))],
                                     xp, w_o, mod_p, lng0, lnb0, wr_hi, wr_lo, br, ROW_TILE)
            x1_s, h2c_s, lg_s = post(_pro_identity, (o_s,), [pl.BlockSpec((SAMPLE_TILE, d), lambda i: (i, 0))],
                                     xs, w_o, mod_s, lng0, lnb0, wr_hi, wr_lo, br, SAMPLE_TILE)
        else:
            lb = l // 2
            w_in = gmlp_w_in[lb].astype(BF16)
            w_out = gmlp_w_out[lb].astype(BF16)
            vg, vb = gmlp_vn_g[lb][None, :], gmlp_vn_b[lb][None, :]
            u_p, vn_p = _gmlp_in(xp, mod_in, w_in, vg, vb, IN_TILE, BF16)
            u_s, vn_s = _gmlp_in(xs, mod_s, w_in, vg, vb, SAMPLE_TILE, F32)
            new_gv.append(vn_s.reshape(dec_seq, nseq, d).transpose(1, 0, 2))
            gw = d // GMLP_GROUPS
            bexp = jnp.repeat(gmlp_b_s[lb].T, gw, axis=1)
            wexp = jnp.repeat(gmlp_w_s[lb][:, :dec_seq, :dec_seq].transpose(1, 2, 0).reshape(dec_seq * dec_seq, -1),
                              gw, axis=1)
            tile = pl.BlockSpec((ROW_TILE, d), lambda i: (i, 0))
            x1_p, h2c_p, lg_p = post(
                _pro_gmlp_prompt, (u_p, vn_p, gmlp_w_s[lb], bexp),
                [tile, tile, pl.BlockSpec(gmlp_w_s[lb].shape, lambda i: (0, 0, 0)),
                 pl.BlockSpec(bexp.shape, lambda i: (0, 0))],
                xp, w_out, mod_p, lng0, lnb0, wr_hi, wr_lo, br, ROW_TILE)
            x1_s, h2c_s, lg_s = post(
                _pro_gmlp_sample, (u_s, vn_s, wexp, bexp[:SUBLANES]),
                [pl.BlockSpec((SAMPLE_TILE, d), lambda i: (i, 0)),
                 pl.BlockSpec(vn_s.shape, lambda i: (0, 0)),
                 pl.BlockSpec(wexp.shape, lambda i: (0, 0)),
                 pl.BlockSpec((SUBLANES, d), lambda i: (0, 0))],
                xs, w_out, mod_s, lng0, lnb0, wr_hi, wr_lo, br, SAMPLE_TILE)

        xp, xs = _moe_and_norm(alpha, x1_p, x1_s, h2c_p, h2c_s, lg_p, lg_s, mod_p, mod_s, lng1, lnb1,
                               moe_w_gate, moe_w_up, moe_w_down, l)

    y_p = xp.reshape(batch, seq, d)
    y_s = xs.reshape(dec_seq, nseq, d).transpose(1, 0, 2)
    return (y_p, y_s, jnp.stack(new_kp), jnp.stack(new_vp), jnp.stack(new_ks), jnp.stack(new_vs),
            jnp.stack(new_gv))
```
